```python
import math
import jax
import jax.numpy as jnp
from jax import lax
import numpy as np

D_MODEL = 1024
BATCH = 16
SEQ = 2048
DEPTH = 2
DEC_BATCH = 32
DEC_SEQ = 8
PAST_LEN = 16384
PAGE_SIZE = 128

N_A_LAYERS = DEPTH // 2
N_B_LAYERS = DEPTH - N_A_LAYERS
SSM_EXPAND = 2
D_INNER = SSM_EXPAND * D_MODEL
SSM_HEAD_DIM = 64
SSM_HEADS = D_INNER // SSM_HEAD_DIM
SSM_GROUPS = 4
HEADS_PER_GROUP = SSM_HEADS // SSM_GROUPS
D_STATE = 128
CONV_W = 4
CONV_DIM = D_INNER + 2 * SSM_GROUPS * D_STATE
SSM_IN_DIM = D_INNER + CONV_DIM + SSM_HEADS
SSD_CHUNK = 128
N_HEADS = 16
HEAD_DIM = D_MODEL // N_HEADS
MOBA_BLOCK = 256
MOBA_TOPK = 3
Q_CHUNK = 16
ROPE_THETA = 10000.0
N_EXPERTS = 16
N_EXPERT_GROUPS = 4
EXPERTS_PER_GROUP = N_EXPERTS // N_EXPERT_GROUPS
MOE_TOP_K = 2
D_EXPERT = 1024
MOE_ROW_BLOCK = 128
DEEPNORM_ALPHA = (2.0 * DEPTH) ** 0.25
DEEPNORM_BETA = (8.0 * DEPTH) ** -0.25
LN_EPS = 1e-5
RMS_EPS = 1e-6

kernel_name = "yoco_mamba2_moba_grouped_moe_step"


def _layer_norm(x, g, b):
    xf = x.astype(jnp.float32)
    mu = xf.mean(-1, keepdims=True)
    var = jnp.square(xf - mu).mean(-1, keepdims=True)
    return ((xf - mu) * lax.rsqrt(var + LN_EPS) * g + b).astype(x.dtype)


def _post_norm(x, sub, g, b):
    return _layer_norm(DEEPNORM_ALPHA * x + sub.astype(x.dtype), g, b)


def _rope(x, pos):
    half = HEAD_DIM // 2
    inv = ROPE_THETA ** (-jnp.arange(half, dtype=jnp.float32) / half)
    ang = pos.astype(jnp.float32)[:, None] * inv
    cos = jnp.cos(ang)[None, :, None, :]
    sin = jnp.sin(ang)[None, :, None, :]
    xf = x.astype(jnp.float32)
    x1, x2 = xf[..., :half], xf[..., half:]
    return jnp.concatenate([x1 * cos - x2 * sin, x2 * cos + x1 * sin], -1).astype(x.dtype)


def _causal_conv(xbc, prefix, w, b):
    seq_len = xbc.shape[1]
    xp = jnp.concatenate([prefix.astype(xbc.dtype), xbc], axis=1)
    out = b
    for k in range(CONV_W):
        out = out + xp[:, k:k + seq_len] * w[k]
    return out, xp[:, seq_len:]


def _ssd_scan(x, dt, a, bm, cm, s0):
    bsz, seq_len = x.shape[:2]
    c = math.gcd(seq_len, SSD_CHUNK)
    nc = seq_len // c

    def chunks(t):
        return jnp.moveaxis(t.reshape((bsz, nc, c) + t.shape[2:]), 1, 0)

    xg = x.reshape(bsz, seq_len, SSM_GROUPS, HEADS_PER_GROUP, SSM_HEAD_DIM)
    dtg = dt.reshape(bsz, seq_len, SSM_GROUPS, HEADS_PER_GROUP)
    a_g = a.reshape(SSM_GROUPS, HEADS_PER_GROUP)
    causal = jnp.tril(jnp.ones((c, c), bool))[None, :, :, None, None]

    def step(s, inp):
        xc, dtc, bc, cc = inp
        acs = jnp.cumsum(dtc * a_g, axis=1)
        seg = acs[:, :, None] - acs[:, None]
        decay_in = jnp.exp(jnp.where(causal, seg, -jnp.inf))
        cb = jnp.einsum('blgn,bsgn->blsg', cc, bc)
        xdt = xc * dtc[..., None]
        y = jnp.einsum('blsg,blsge,bsgep->blgep', cb, decay_in, xdt)
        y = y + jnp.einsum('blgn,bgepn->blgep', cc, s) * jnp.exp(acs)[..., None]
        decay_out = jnp.exp(acs[:, -1:] - acs)
        s = (s * jnp.exp(acs[:, -1])[..., None, None]
             + jnp.einsum('bsgn,bsgep->bgepn', bc, xdt * decay_out[..., None]))
        return s, y

    s0g = s0.reshape(bsz, SSM_GROUPS, HEADS_PER_GROUP, SSM_HEAD_DIM, D_STATE)
    s_fin, ys = lax.scan(step, s0g, (chunks(xg), chunks(dtg), chunks(bm), chunks(cm)))
    y = jnp.moveaxis(ys, 0, 1).reshape(bsz, seq_len, SSM_HEADS, SSM_HEAD_DIM)
    return y, s_fin.reshape(bsz, SSM_HEADS, SSM_HEAD_DIM, D_STATE)


def _mamba2(x, ssm0, conv0, w_in, w_conv, b_conv, dt_bias, a_log, d_skip, norm_w, w_out):
    bsz, seq_len, _ = x.shape
    zxbcdt = x @ w_in
    z, xbc, dt = jnp.split(zxbcdt, [D_INNER, D_INNER + CONV_DIM], axis=-1)
    xbc, conv_new = _causal_conv(xbc, conv0, w_conv, b_conv)
    xbc = jax.nn.silu(xbc)
    xs, bm, cm = jnp.split(xbc, [D_INNER, D_INNER + SSM_GROUPS * D_STATE], axis=-1)
    xs = xs.reshape(bsz, seq_len, SSM_HEADS, SSM_HEAD_DIM).astype(jnp.float32)
    bm = bm.reshape(bsz, seq_len, SSM_GROUPS, D_STATE).astype(jnp.float32)
    cm = cm.reshape(bsz, seq_len, SSM_GROUPS, D_STATE).astype(jnp.float32)
    dt = jax.nn.softplus(dt.astype(jnp.float32) + dt_bias.astype(jnp.float32))
    a = -jnp.exp(a_log.astype(jnp.float32))
    y, ssm_new = _ssd_scan(xs, dt, a, bm, cm, ssm0.astype(jnp.float32))
    y = y + d_skip.astype(jnp.float32)[:, None] * xs
    y = y.reshape(bsz, seq_len, D_INNER) * jax.nn.silu(z.astype(jnp.float32))
    yg = y.reshape(bsz, seq_len, SSM_GROUPS, D_INNER // SSM_GROUPS)
    yg = yg * lax.rsqrt(jnp.mean(yg * yg, -1, keepdims=True) + RMS_EPS)
    y = (yg.reshape(bsz, seq_len, D_INNER) * norm_w).astype(x.dtype)
    return y @ w_out, ssm_new, conv_new


def _shared_kv(h, w_kv, pos):
    bsz, seq_len, _ = h.shape
    kv = (h @ w_kv).reshape(bsz, seq_len, 2, N_HEADS, HEAD_DIM)
    k = _rope(kv[:, :, 0], pos)
    return k.transpose(0, 2, 1, 3), kv[:, :, 1].transpose(0, 2, 1, 3)


def _queries(h, w_q, pos):
    bsz, seq_len, _ = h.shape
    q = (h @ w_q).reshape(bsz, seq_len, N_HEADS, HEAD_DIM)
    return _rope(q, pos).transpose(0, 2, 1, 3)


def _merge_heads(o):
    bsz, h, seq_len, hd = o.shape
    return o.transpose(0, 2, 1, 3).reshape(bsz, seq_len, h * hd)


def _moba_attend(q, sel, own_k, own_v, own_mask):
    qf = q.astype(jnp.float32) * (HEAD_DIM ** -0.5)
    s_own = jnp.einsum('bhqd,bhkd->bhqk', qf, own_k.astype(jnp.float32))
    s_own = jnp.where(own_mask, s_own, -jnp.inf)
    if sel is None:
        p = jax.nn.softmax(s_own, axis=-1)
        return jnp.einsum('bhqk,bhkd->bhqd', p, own_v.astype(jnp.float32)).astype(q.dtype)
    sel_k, sel_v, sel_valid = sel
    s_sel = jnp.einsum('bhqd,bhqtkd->bhqtk', qf, sel_k.astype(jnp.float32))
    s_sel = jnp.where(sel_valid[:, :, None], s_sel, -jnp.inf)
    bsz, h, nq, nt, nk = s_sel.shape
    p = jax.nn.softmax(jnp.concatenate([s_sel.reshape(bsz, h, nq, nt * nk), s_own], -1), axis=-1)
    p_sel = p[..., :nt * nk].reshape(bsz, h, nq, nt, nk)
    p_own = p[..., nt * nk:]
    out = (jnp.einsum('bhqtk,bhqtkd->bhqd', p_sel, sel_v.astype(jnp.float32))
           + jnp.einsum('bhqk,bhkd->bhqd', p_own, own_v.astype(jnp.float32)))
    return out.astype(q.dtype)


def _moba_prompt(q, k, v):
    bsz, h, seq_len, hd = q.shape
    nb = -(-seq_len // MOBA_BLOCK)
    pad = nb * MOBA_BLOCK - seq_len
    kb = jnp.pad(k, ((0, 0), (0, 0), (0, pad), (0, 0))).reshape(bsz, h, nb, MOBA_BLOCK, hd)
    vb = jnp.pad(v, ((0, 0), (0, 0), (0, pad), (0, 0))).reshape(bsz, h, nb, MOBA_BLOCK, hd)
    topk = min(MOBA_TOPK, (seq_len - 1) // MOBA_BLOCK)
    n_past = jnp.arange(seq_len) // MOBA_BLOCK
    nc = seq_len // Q_CHUNK
    q_ch = jnp.moveaxis(q.reshape(bsz, h, nc, Q_CHUNK, hd), 2, 0)
    starts = jnp.arange(nc) * Q_CHUNK
    xs = (q_ch, starts)
    if topk > 0:
        k_mean = kb.astype(jnp.float32).mean(axis=3)
        gate = jnp.einsum('bhqd,bhnd->bhqn', q.astype(jnp.float32), k_mean)
        gate = jnp.where(jnp.arange(nb)[None, :] < n_past[:, None], gate, -jnp.inf)
        _, idx = lax.top_k(gate, topk)
        valid = jnp.arange(topk)[None, :] < n_past[:, None]
        xs = xs + (jnp.moveaxis(idx.reshape(bsz, h, nc, Q_CHUNK, topk), 2, 0),
                   valid.reshape(nc, Q_CHUNK, topk))
    bi = jnp.arange(bsz)[:, None, None, None]
    hi = jnp.arange(h)[None, :, None, None]

    def attend(args):
        qc, start = args[0], args[1]
        blk = start // MOBA_BLOCK
        own_k = lax.dynamic_index_in_dim(kb, blk, axis=2, keepdims=False)
        own_v = lax.dynamic_index_in_dim(vb, blk, axis=2, keepdims=False)
        kpos = blk * MOBA_BLOCK + jnp.arange(MOBA_BLOCK)
        qpos = start + jnp.arange(Q_CHUNK)
        mask = kpos[None, :] <= qpos[:, None]
        sel = None
        if topk > 0:
            ic, vc = args[2], args[3]
            sel = (kb[bi, hi, ic], vb[bi, hi, ic], vc)
        return _moba_attend(qc, sel, own_k, own_v, mask)

    out = lax.map(attend, xs)
    return jnp.moveaxis(out, 0, 2).reshape(bsz, h, seq_len, hd)


def _moba_sample(q, k_new, v_new, cache_k, cache_v, page_table):
    bsz, h, nq, hd = q.shape
    n_pages = page_table.shape[1]
    past_len = n_pages * PAGE_SIZE
    ppb = MOBA_BLOCK // PAGE_SIZE
    n_full = past_len // MOBA_BLOCK
    topk = min(MOBA_TOPK, n_full)
    n_part = (past_len - n_full * MOBA_BLOCK) // PAGE_SIZE
    own_pages = page_table[:, n_full * ppb:]
    own_k = cache_k[own_pages].transpose(0, 2, 1, 3, 4).reshape(bsz, h, n_part * PAGE_SIZE, hd)
    own_v = cache_v[own_pages].transpose(0, 2, 1, 3, 4).reshape(bsz, h, n_part * PAGE_SIZE, hd)
    own_k = jnp.concatenate([own_k.astype(k_new.dtype), k_new], axis=2)
    own_v = jnp.concatenate([own_v.astype(v_new.dtype), v_new], axis=2)
    mask = jnp.concatenate([jnp.ones((nq, n_part * PAGE_SIZE), bool),
                            jnp.tril(jnp.ones((nq, nq), bool))], axis=1)
    sel = None
    if topk > 0:
        def page_sums(pt_row):
            return cache_k[pt_row].astype(jnp.float32).sum(axis=2)
        psum = lax.map(page_sums, page_table[:, :n_full * ppb])
        k_mean = psum.reshape(bsz, n_full, ppb, h, hd).sum(2).transpose(0, 2, 1, 3) / MOBA_BLOCK
        gate = jnp.einsum('bhqd,bhnd->bhqn', q.astype(jnp.float32), k_mean)
        _, idx = lax.top_k(gate, topk)
        lp = idx[..., None] * ppb + jnp.arange(ppb)
        phys = page_table[jnp.arange(bsz)[:, None, None, None, None], lp]
        hi = jnp.arange(h)[None, :, None, None, None]
        sel_k = cache_k[phys, hi].reshape(bsz, h, nq, topk, MOBA_BLOCK, hd)
        sel_v = cache_v[phys, hi].reshape(bsz, h, nq, topk, MOBA_BLOCK, hd)
        sel = (sel_k, sel_v, jnp.ones((nq, topk), bool))
    return _moba_attend(q, sel, own_k, own_v, mask)


def _route(x2d, w_router, b_router):
    n_tok = x2d.shape[0]
    s = jax.nn.sigmoid((x2d @ w_router).astype(jnp.float32))
    sb = (s + b_router.astype(jnp.float32)).reshape(n_tok, N_EXPERT_GROUPS, EXPERTS_PER_GROUP)
    g_score = lax.top_k(sb, MOE_TOP_K)[0].sum(-1)
    g = jnp.argmax(g_score, axis=-1)
    sb_g = jnp.take_along_axis(sb, g[:, None, None], axis=1)[:, 0]
    _, local = lax.top_k(sb_g, MOE_TOP_K)
    idx = g[:, None] * EXPERTS_PER_GROUP + local
    w = jnp.take_along_axis(s, idx, axis=1)
    return idx, w / w.sum(-1, keepdims=True)


def _moe(x, w_router, b_router, w_gate, w_up, w_down):
    bsz, seq_len, d = x.shape
    x2d = x.reshape(-1, d)
    n_tok = x2d.shape[0]
    idx, w = _route(x2d, w_router, b_router)
    n_assign = n_tok * MOE_TOP_K
    flat_e = idx.reshape(-1)
    order = jnp.argsort(flat_e)
    e_sorted = flat_e[order]
    tok = order // MOE_TOP_K
    counts = jnp.bincount(flat_e, length=N_EXPERTS)
    padded = (counts + MOE_ROW_BLOCK - 1) // MOE_ROW_BLOCK * MOE_ROW_BLOCK
    pad_end = jnp.cumsum(padded)
    pad_start = pad_end - padded
    start = jnp.cumsum(counts) - counts
    dest = pad_start[e_sorted] + jnp.arange(n_assign) - start[e_sorted]
    n_blk = -(-n_assign // MOE_ROW_BLOCK) + N_EXPERTS
    row_tok = jnp.zeros((n_blk * MOE_ROW_BLOCK,), jnp.int32).at[dest].set(tok.astype(jnp.int32))
    blk_e = jnp.minimum(jnp.searchsorted(pad_end, jnp.arange(n_blk) * MOE_ROW_BLOCK, side='right'),
                        N_EXPERTS - 1)
    xs = x2d[row_tok].reshape(n_blk, MOE_ROW_BLOCK, d)

    def expert_block(args):
        xb, e = args
        hdn = jax.nn.silu(xb @ w_gate[e]) * (xb @ w_up[e])
        return hdn @ w_down[e]

    ys = lax.map(expert_block, (xs, blk_e)).reshape(-1, d)
    contrib = ys[dest] * w.reshape(-1)[order][:, None].astype(ys.dtype)
    return jax.ops.segment_sum(contrib, tok, num_segments=n_tok).reshape(bsz, seq_len, d)


def setup_inputs(seed: int = 0) -> dict:
    key = jax.random.key(seed)
    ks = jax.random.split(key, 25)
    f32 = jnp.float32

    def nrm(k, shape, scale):
        return jax.random.normal(k, shape, f32) * scale

    n_pages = PAST_LEN // PAGE_SIZE
    used = DEC_BATCH * n_pages
    n_phys = used + max(1, used // 4)
    page_table = jax.random.permutation(ks[0], n_phys)[:used].reshape(DEC_BATCH, n_pages).astype(jnp.int32)
    dt0 = jnp.exp(jax.random.uniform(ks[12], (N_A_LAYERS, SSM_HEADS), f32,
                                     minval=math.log(1e-3), maxval=math.log(1e-1)))
    return {
        'x_prompt': nrm(ks[1], (BATCH, SEQ, D_MODEL), 1.0),
        'x_sample': nrm(ks[2], (DEC_BATCH, DEC_SEQ, D_MODEL), 1.0),
        'state_ssm': nrm(ks[3], (N_A_LAYERS, DEC_BATCH, SSM_HEADS, SSM_HEAD_DIM, D_STATE), 0.1),
        'state_conv': nrm(ks[4], (N_A_LAYERS, DEC_BATCH, CONV_W - 1, CONV_DIM), 1.0),
        'cache_k': nrm(ks[5], (n_phys, N_HEADS, PAGE_SIZE, HEAD_DIM), 1.0),
        'cache_v': nrm(ks[6], (n_phys, N_HEADS, PAGE_SIZE, HEAD_DIM), 1.0),
        'page_table': page_table,
        'ln_gain': 1.0 + nrm(ks[7], (DEPTH, 2, D_MODEL), 0.02),
        'ln_bias': nrm(ks[8], (DEPTH, 2, D_MODEL), 0.02),
        'ssm_w_in': nrm(ks[9], (N_A_LAYERS, D_MODEL, SSM_IN_DIM), D_MODEL ** -0.5),
        'ssm_w_conv': nrm(ks[10], (N_A_LAYERS, CONV_W, CONV_DIM), CONV_W ** -0.5),
        'ssm_b_conv': nrm(ks[11], (N_A_LAYERS, CONV_DIM), 0.02),
        'ssm_dt_bias': dt0 + jnp.log(-jnp.expm1(-dt0)),
        'ssm_a_log': jnp.log(jax.random.uniform(ks[13], (N_A_LAYERS, SSM_HEADS), f32, minval=1.0, maxval=16.0)),
        'ssm_d': 1.0 + nrm(ks[14], (N_A_LAYERS, SSM_HEADS), 0.1),
        'ssm_norm_w': 1.0 + nrm(ks[15], (N_A_LAYERS, D_INNER), 0.02),
        'ssm_w_out': nrm(ks[16], (N_A_LAYERS, D_INNER, D_MODEL), D_INNER ** -0.5 * DEEPNORM_BETA),
        'attn_w_q': nrm(ks[17], (N_B_LAYERS, D_MODEL, N_HEADS * HEAD_DIM), D_MODEL ** -0.5),
        'attn_w_kv': nrm(ks[18], (D_MODEL, 2 * N_HEADS * HEAD_DIM), D_MODEL ** -0.5),
        'attn_w_o': nrm(ks[19], (N_B_LAYERS, N_HEADS * HEAD_DIM, D_MODEL), (N_HEADS * HEAD_DIM) ** -0.5 * DEEPNORM_BETA),
        'router_w': nrm(ks[20], (D_MODEL, N_EXPERTS), D_MODEL ** -0.5),
        'router_b': nrm(ks[21], (N_EXPERTS,), 0.01),
        'moe_w_gate': nrm(ks[22], (DEPTH, N_EXPERTS, D_MODEL, D_EXPERT), D_MODEL ** -0.5),
        'moe_w_up': nrm(ks[23], (DEPTH, N_EXPERTS, D_MODEL, D_EXPERT), D_MODEL ** -0.5),
        'moe_w_down': nrm(ks[24], (DEPTH, N_EXPERTS, D_EXPERT, D_MODEL), D_EXPERT ** -0.5 * DEEPNORM_BETA),
    }


def reference(x_prompt, x_sample, state_ssm, state_conv, cache_k, cache_v, page_table,
              ln_gain, ln_bias, ssm_w_in, ssm_w_conv, ssm_b_conv, ssm_dt_bias, ssm_a_log,
              ssm_d, ssm_norm_w, ssm_w_out, attn_w_q, attn_w_kv, attn_w_o,
              router_w, router_b, moe_w_gate, moe_w_up, moe_w_down):
    bp, lp_len, _ = x_prompt.shape
    ls_len = x_sample.shape[1]
    past_len = page_table.shape[1] * PAGE_SIZE
    pos_p = jnp.arange(lp_len)
    pos_s = past_len + jnp.arange(ls_len)
    hp, hs = x_prompt, x_sample
    ssm_p, conv_p, ssm_s, conv_s = [], [], [], []
    for l in range(DEPTH):
        if l < N_A_LAYERS:
            w = (ssm_w_in[l], ssm_w_conv[l], ssm_b_conv[l], ssm_dt_bias[l], ssm_a_log[l],
                 ssm_d[l], ssm_norm_w[l], ssm_w_out[l])
            zero_ssm = jnp.zeros((bp, SSM_HEADS, SSM_HEAD_DIM, D_STATE), jnp.float32)
            zero_conv = jnp.zeros((bp, CONV_W - 1, CONV_DIM), hp.dtype)
            mp, sp, cp = _mamba2(hp, zero_ssm, zero_conv, *w)
            ms, ss, cs = _mamba2(hs, state_ssm[l], state_conv[l], *w)
            ssm_p.append(sp)
            conv_p.append(cp)
            ssm_s.append(ss)
            conv_s.append(cs)
        else:
            j = l - N_A_LAYERS
            if j == 0:
                k_p, v_p = _shared_kv(hp, attn_w_kv, pos_p)
                k_s, v_s = _shared_kv(hs, attn_w_kv, pos_s)
            o_p = _moba_prompt(_queries(hp, attn_w_q[j], pos_p), k_p, v_p)
            o_s = _moba_sample(_queries(hs, attn_w_q[j], pos_s), k_s, v_s, cache_k, cache_v, page_table)
            mp = _merge_heads(o_p) @ attn_w_o[j]
            ms = _merge_heads(o_s) @ attn_w_o[j]
        hp = _post_norm(hp, mp, ln_gain[l, 0], ln_bias[l, 0])
        hs = _post_norm(hs, ms, ln_gain[l, 0], ln_bias[l, 0])
        hp = _post_norm(hp, _moe(hp, router_w, router_b, moe_w_gate[l], moe_w_up[l], moe_w_down[l]),
                        ln_gain[l, 1], ln_bias[l, 1])
        hs = _post_norm(hs, _moe(hs, router_w, router_b, moe_w_gate[l], moe_w_up[l], moe_w_down[l]),
                        ln_gain[l, 1], ln_bias[l, 1])
    return (hp, hs, jnp.stack(ssm_p), jnp.stack(conv_p), k_p, v_p,
            jnp.stack(ssm_s), jnp.stack(conv_s), k_s, v_s)
```

```python
import functools
import math

import jax
import jax.numpy as jnp
from jax import lax
from jax.experimental import pallas as pl
from jax.experimental.pallas import tpu as pltpu

f32 = jnp.float32
bf16 = jnp.bfloat16
HIGHEST = lax.Precision.HIGHEST

D_MODEL = 1024
DEPTH = 2
PAGE_SIZE = 128
N_A_LAYERS = 1
SSM_HEAD_DIM = 64
D_INNER = 2048
SSM_HEADS = D_INNER // SSM_HEAD_DIM
SSM_GROUPS = 4
D_STATE = 128
CONV_W = 4
CONV_DIM = D_INNER + 2 * SSM_GROUPS * D_STATE
SSD_CHUNK = 128
N_HEADS = 16
HEAD_DIM = 64
MOBA_BLOCK = 256
MOBA_TOPK = 3
ROPE_THETA = 10000.0
N_EXPERTS = 16
N_EXPERT_GROUPS = 4
EXPERTS_PER_GROUP = 4
MOE_TOP_K = 2
D_EXPERT = 1024
DEEPNORM_ALPHA = (2.0 * DEPTH) ** 0.25
LN_EPS = 1e-5
RMS_EPS = 1e-6

LANES = 128
SUBLANES = 8
VMEM_LIMIT = 48 * 1024 * 1024

TOKEN_TILE = 256
EXPERT_TILE = 256
COL_CHUNK = 512
PAGES_PER_STEP = 8
HEAD_PAIRS = N_HEADS // 2
SSM_PAIRS = SSM_HEADS // 2


def _params(sem):
    return pltpu.CompilerParams(dimension_semantics=sem, vmem_limit_bytes=VMEM_LIMIT)


def _sigmoid(x):
    return 1.0 / (1.0 + jnp.exp(-x))


def _softplus(x):
    return jnp.maximum(x, 0.0) + jnp.log1p(jnp.exp(-jnp.abs(x)))


def _layer_norm_rows(v, g, b):
    mu = jnp.mean(v, axis=-1, keepdims=True)
    d = v - mu
    var = jnp.mean(d * d, axis=-1, keepdims=True)
    return d * lax.rsqrt(var + LN_EPS) * g + b


def _in_proj_kernel(x_ref, wz_ref, wx_ref, wdt_ref, z_ref, xbc_ref, dt_ref):
    x = x_ref[...]
    xb = x.astype(bf16)
    for c in range(0, D_INNER, COL_CHUNK):
        z_ref[:, c:c + COL_CHUNK] = jnp.dot(xb, wz_ref[:, c:c + COL_CHUNK],
                                            preferred_element_type=f32)
    for c in range(0, CONV_DIM, COL_CHUNK):
        xbc_ref[:, c:c + COL_CHUNK] = jnp.dot(xb, wx_ref[:, c:c + COL_CHUNK],
                                              preferred_element_type=f32)
    dt_ref[...] = jnp.dot(x, wdt_ref[...], preferred_element_type=f32, precision=HIGHEST)


def _in_proj(x, wz, wx, wdt, interpret=False):
    n = x.shape[0]
    tm = TOKEN_TILE
    return pl.pallas_call(
        _in_proj_kernel,
        out_shape=(jax.ShapeDtypeStruct((n, D_INNER), f32),
                   jax.ShapeDtypeStruct((n, CONV_DIM), f32),
                   jax.ShapeDtypeStruct((n, LANES), f32)),
        grid=(n // tm,),
        in_specs=[pl.BlockSpec((tm, D_MODEL), lambda i: (i, 0)),
                  pl.BlockSpec((D_MODEL, D_INNER), lambda i: (0, 0)),
                  pl.BlockSpec((D_MODEL, CONV_DIM), lambda i: (0, 0)),
                  pl.BlockSpec((D_MODEL, LANES), lambda i: (0, 0))],
        out_specs=(pl.BlockSpec((tm, D_INNER), lambda i: (i, 0)),
                   pl.BlockSpec((tm, CONV_DIM), lambda i: (i, 0)),
                   pl.BlockSpec((tm, LANES), lambda i: (i, 0))),
        compiler_params=_params(("parallel",)),
        name="ssm_in_proj",
        interpret=interpret,
    )(x, wz, wx, wdt)


def _ssd_kernel(*refs, c_in, nc, has_init):
    T = SSD_CHUNK
    if has_init:
        (z_ref, xbc_ref, dt_ref, conv0_ref, st0_ref, wconv_ref, bconv_ref, dtb_ref, alog_ref,
         d_ref, nw_ref, _y_alias, y_ref, st_out_ref, xbuf, xc, ybuf, st) = refs
    else:
        (z_ref, xbc_ref, dt_ref, wconv_ref, bconv_ref, dtb_ref, alog_ref,
         d_ref, nw_ref, y_ref, st_out_ref, xbuf, xc, ybuf, st) = refs
    c = pl.program_id(1)

    @pl.when(c == 0)
    def _init():
        if has_init:
            xbuf[0:SUBLANES, :] = conv0_ref[0]
            st[...] = st0_ref[0]
        else:
            xbuf[0:SUBLANES, :] = jnp.zeros((SUBLANES, CONV_DIM), f32)
            st[...] = jnp.zeros(st.shape, f32)

    xbuf[SUBLANES:SUBLANES + c_in, :] = xbc_ref[...]
    if c_in < T:
        xbuf[SUBLANES + c_in:SUBLANES + T, :] = jnp.zeros((T - c_in, CONV_DIM), f32)

    base = SUBLANES - (CONV_W - 1)
    for c0 in range(0, CONV_DIM, COL_CHUNK):
        cs = slice(c0, c0 + COL_CHUNK)
        acc = bconv_ref[:, cs] + xbuf[base:base + T, cs] * wconv_ref[0:1, cs]
        for k in range(1, CONV_W):
            acc = acc + xbuf[base + k:base + k + T, cs] * wconv_ref[k:k + 1, cs]
        xc[:, cs] = acc * _sigmoid(acc)
    if nc > 1:
        xbuf[0:SUBLANES, :] = xbuf[T:T + SUBLANES, :]

    def pad_rows(v):
        if c_in == T:
            return v
        return jnp.concatenate([v, jnp.zeros((T - c_in, v.shape[1]), v.dtype)], axis=0)

    dtv = pad_rows(_softplus(dt_ref[...] + dtb_ref[...]))
    a = -jnp.exp(alog_ref[...])
    da = dtv * a
    row = lax.broadcasted_iota(jnp.int32, (T, T), 0)
    col = lax.broadcasted_iota(jnp.int32, (T, T), 1)
    causal = col <= row
    tri = causal.astype(f32)
    acs = jnp.dot(tri, da, preferred_element_type=f32, precision=HIGHEST)
    acs_t = acs.T
    dt_t = dtv.T
    w_all = dt_t * jnp.exp(acs_t[:, T - 1:T] - acs_t)
    lo = lax.broadcasted_iota(jnp.int32, (1, LANES), 1) < SSM_HEAD_DIM

    for g in range(SSM_GROUPS):
        bm = xc[:, D_INNER + g * D_STATE:D_INNER + (g + 1) * D_STATE]
        cm = xc[:, D_INNER + (SSM_GROUPS + g) * D_STATE:D_INNER + (SSM_GROUPS + g + 1) * D_STATE]
        cb = lax.dot_general(cm.astype(bf16), bm.astype(bf16), (((1,), (1,)), ((), ())),
                             preferred_element_type=f32)
        bm_t = bm.T
        for e in range(SSM_PAIRS // SSM_GROUPS):
            pr = g * (SSM_PAIRS // SSM_GROUPS) + e
            ls = slice(pr * LANES, (pr + 1) * LANES)
            x_pair = xc[:, ls]
            xb = x_pair.astype(bf16)
            rhs = jnp.concatenate([xb, st[pr].astype(bf16)], axis=0)
            r, u, ea_last = [], [], []
            for hh in range(2):
                h = 2 * pr + hh
                a_b = jnp.broadcast_to(acs[:, h:h + 1], (T, LANES))
                seg = a_b - acs_t[h:h + 1, :]
                dec = jnp.exp(jnp.where(causal, seg, -jnp.inf))
                m_h = cb * dec * dt_t[h:h + 1, :]
                e_a = jnp.exp(a_b)
                lhs = jnp.concatenate([m_h.astype(bf16), (cm * e_a).astype(bf16)], axis=1)
                r.append(jnp.dot(lhs, rhs, preferred_element_type=f32))
                lhs_s = (bm_t * w_all[h:h + 1, :]).astype(bf16)
                u.append(jnp.dot(lhs_s, xb, preferred_element_type=f32))
                ea_last.append(e_a[T - 1:T, :])
            st[pr] = st[pr] * jnp.where(lo, ea_last[0], ea_last[1]) + jnp.where(lo, u[0], u[1])
            yv = jnp.where(lo, r[0], r[1]) + d_ref[:, ls] * x_pair
            zz = pad_rows(z_ref[:, ls])
            ybuf[:, ls] = yv * (zz * _sigmoid(zz))
        gw = D_INNER // SSM_GROUPS
        gs = slice(g * gw, (g + 1) * gw)
        yg = ybuf[:, gs]
        ms = jnp.mean(yg * yg, axis=-1, keepdims=True)
        y_ref[:, gs] = (yg * lax.rsqrt(ms + RMS_EPS) * nw_ref[:, gs])[0:c_in]

    @pl.when(c == nc - 1)
    def _fin():
        st_out_ref[0] = st[...]


def _ssd(z, xbc, dt, conv0, st0, y_alias, wconv, bconv, dtb, alog, d_exp, nw, *,
         n_seq, seq_len, row_off, interpret=False):
    T = SSD_CHUNK
    n = z.shape[0]
    has_init = conv0 is not None
    if seq_len >= T:
        c_in, nc = T, seq_len // T
    else:
        c_in, nc = seq_len, 1
    off = row_off // c_in

    def rows(bi, ci):
        return (off + bi * nc + ci, 0)

    const = lambda bi, ci: (0, 0)
    in_specs = [pl.BlockSpec((c_in, D_INNER), rows),
                pl.BlockSpec((c_in, CONV_DIM), rows),
                pl.BlockSpec((c_in, LANES), rows)]
    args = [z, xbc, dt]
    if has_init:
        in_specs += [pl.BlockSpec((1, SUBLANES, CONV_DIM), lambda bi, ci: (bi, 0, 0)),
                     pl.BlockSpec((1, SSM_PAIRS, D_STATE, LANES), lambda bi, ci: (bi, 0, 0, 0))]
        args += [conv0, st0]
    in_specs += [pl.BlockSpec((SUBLANES, CONV_DIM), const),
                 pl.BlockSpec((1, CONV_DIM), const),
                 pl.BlockSpec((1, LANES), const),
                 pl.BlockSpec((1, LANES), const),
                 pl.BlockSpec((1, D_INNER), const),
                 pl.BlockSpec((1, D_INNER), const)]
    args += [wconv, bconv, dtb, alog, d_exp, nw]
    aliases = {}
    if has_init:
        in_specs.append(pl.BlockSpec(memory_space=pl.ANY))
        args.append(y_alias)
        aliases = {len(args) - 1: 0}
    return pl.pallas_call(
        functools.partial(_ssd_kernel, c_in=c_in, nc=nc, has_init=has_init),
        out_shape=(jax.ShapeDtypeStruct((n, D_INNER), f32),
                   jax.ShapeDtypeStruct((n_seq, SSM_PAIRS, D_STATE, LANES), f32)),
        grid=(n_seq, nc),
        in_specs=in_specs,
        out_specs=(pl.BlockSpec((c_in, D_INNER), rows),
                   pl.BlockSpec((1, SSM_PAIRS, D_STATE, LANES), lambda bi, ci: (bi, 0, 0, 0))),
        scratch_shapes=[pltpu.VMEM((T + 2 * SUBLANES, CONV_DIM), f32),
                        pltpu.VMEM((T, CONV_DIM), f32),
                        pltpu.VMEM((T, D_INNER), f32),
                        pltpu.VMEM((SSM_PAIRS, D_STATE, LANES), f32)],
        input_output_aliases=aliases,
        compiler_params=_params(("parallel", "arbitrary")),
        name="ssd_scan_init" if has_init else "ssd_scan",
        interpret=interpret,
    )(*args)


def _state_to_pairs(s):
    b = s.shape[0]
    s = s.reshape(b, SSM_PAIRS, 2, SSM_HEAD_DIM, D_STATE)
    return s.transpose(0, 1, 4, 2, 3).reshape(b, SSM_PAIRS, D_STATE, 2 * SSM_HEAD_DIM)


def _pairs_to_state(s):
    b = s.shape[0]
    s = s.reshape(b, SSM_PAIRS, D_STATE, 2, SSM_HEAD_DIM)
    return s.transpose(0, 1, 3, 4, 2).reshape(b, SSM_HEADS, SSM_HEAD_DIM, D_STATE)


def _proj_ln_kernel(y_ref, w_ref, x_ref, g_ref, b_ref, o_ref):
    yb = y_ref[...].astype(bf16)
    for c in range(0, D_MODEL, COL_CHUNK):
        cs = slice(c, c + COL_CHUNK)
        o_ref[:, cs] = DEEPNORM_ALPHA * x_ref[:, cs] + jnp.dot(yb, w_ref[:, cs],
                                                               preferred_element_type=f32)
    o_ref[...] = _layer_norm_rows(o_ref[...], g_ref[...], b_ref[...])


def _proj_ln(y, w, x, g, b, interpret=False):
    n, k = y.shape
    tm = TOKEN_TILE
    return pl.pallas_call(
        _proj_ln_kernel,
        out_shape=jax.ShapeDtypeStruct((n, D_MODEL), f32),
        grid=(n // tm,),
        in_specs=[pl.BlockSpec((tm, k), lambda i: (i, 0)),
                  pl.BlockSpec((k, D_MODEL), lambda i: (0, 0)),
                  pl.BlockSpec((tm, D_MODEL), lambda i: (i, 0)),
                  pl.BlockSpec((1, D_MODEL), lambda i: (0, 0)),
                  pl.BlockSpec((1, D_MODEL), lambda i: (0, 0))],
        out_specs=pl.BlockSpec((tm, D_MODEL), lambda i: (i, 0)),
        compiler_params=_params(("parallel",)),
        name="proj_postnorm",
        interpret=interpret,
    )(y, w, x, g, b)


def _router_kernel(x_ref, wt_ref, b_ref, e_ref, w_ref):
    logits = lax.dot_general(wt_ref[...], x_ref[...], (((1,), (1,)), ((), ())),
                             preferred_element_type=f32, precision=HIGHEST)
    s = _sigmoid(logits)
    sb = s + b_ref[...]
    srow = [s[i:i + 1, :] for i in range(N_EXPERTS)]
    brow = [sb[i:i + 1, :] for i in range(N_EXPERTS)]
    gscore = []
    for g in range(N_EXPERT_GROUPS):
        v = brow[g * EXPERTS_PER_GROUP:(g + 1) * EXPERTS_PER_GROUP]
        best = None
        for i in range(EXPERTS_PER_GROUP):
            for j in range(i + 1, EXPERTS_PER_GROUP):
                hi = jnp.maximum(v[i], v[j])
                lo_ = jnp.minimum(v[i], v[j])
                p = hi + lo_
                best = p if best is None else jnp.maximum(best, p)
        gscore.append(best)
    gi = jnp.zeros_like(gscore[0], dtype=jnp.int32)
    gbest = gscore[0]
    for g in range(1, N_EXPERT_GROUPS):
        upd = gscore[g] > gbest
        gi = jnp.where(upd, g, gi)
        gbest = jnp.where(upd, gscore[g], gbest)
    vb, vs = [], []
    for k in range(EXPERTS_PER_GROUP):
        tb, ts = brow[k], srow[k]
        for g in range(1, N_EXPERT_GROUPS):
            tb = jnp.where(gi == g, brow[g * EXPERTS_PER_GROUP + k], tb)
            ts = jnp.where(gi == g, srow[g * EXPERTS_PER_GROUP + k], ts)
        vb.append(tb)
        vs.append(ts)
    i1 = jnp.zeros_like(gi)
    b1, s1 = vb[0], vs[0]
    for k in range(1, EXPERTS_PER_GROUP):
        upd = vb[k] > b1
        i1 = jnp.where(upd, k, i1)
        b1 = jnp.where(upd, vb[k], b1)
        s1 = jnp.where(upd, vs[k], s1)
    i2 = jnp.full_like(gi, -1)
    b2 = jnp.full_like(b1, -jnp.inf)
    s2 = jnp.zeros_like(s1)
    for k in range(EXPERTS_PER_GROUP):
        upd = (i1 != k) & ((vb[k] > b2) | (i2 < 0))
        i2 = jnp.where(upd, k, i2)
        b2 = jnp.where(upd, vb[k], b2)
        s2 = jnp.where(upd, vs[k], s2)
    den = s1 + s2
    tm = gi.shape[1]
    zi = jnp.zeros((SUBLANES - 2, tm), jnp.int32)
    zf = jnp.zeros((SUBLANES - 2, tm), f32)
    e_ref[...] = jnp.concatenate([gi * EXPERTS_PER_GROUP + i1, gi * EXPERTS_PER_GROUP + i2, zi], axis=0)
    w_ref[...] = jnp.concatenate([s1 / den, s2 / den, zf], axis=0)


def _router(x, wt, b, interpret=False):
    n = x.shape[0]
    tm = TOKEN_TILE
    return pl.pallas_call(
        _router_kernel,
        out_shape=(jax.ShapeDtypeStruct((SUBLANES, n), jnp.int32),
                   jax.ShapeDtypeStruct((SUBLANES, n), f32)),
        grid=(n // tm,),
        in_specs=[pl.BlockSpec((tm, D_MODEL), lambda i: (i, 0)),
                  pl.BlockSpec((N_EXPERTS, D_MODEL), lambda i: (0, 0)),
                  pl.BlockSpec((N_EXPERTS, 1), lambda i: (0, 0))],
        out_specs=(pl.BlockSpec((SUBLANES, tm), lambda i: (0, i)),
                   pl.BlockSpec((SUBLANES, tm), lambda i: (0, i))),
        compiler_params=_params(("parallel",)),
        name="moe_router",
        interpret=interpret,
    )(x, wt, b)


def _expert_kernel(blk_e_ref, n_used_ref, x_ref, wg_ref, wu_ref, wd_ref, o_ref, h_scr):
    i = pl.program_id(0)

    @pl.when(i < n_used_ref[0])
    def _():
        xb = x_ref[...].astype(bf16)
        for c in range(0, D_EXPERT, COL_CHUNK):
            cs = slice(c, c + COL_CHUNK)
            hg = jnp.dot(xb, wg_ref[0, :, cs], preferred_element_type=f32)
            hu = jnp.dot(xb, wu_ref[0, :, cs], preferred_element_type=f32)
            h_scr[:, cs] = (hg * _sigmoid(hg) * hu).astype(bf16)
        hb = h_scr[...]
        for c in range(0, D_MODEL, COL_CHUNK):
            cs = slice(c, c + COL_CHUNK)
            o_ref[:, cs] = jnp.dot(hb, wd_ref[0, :, cs], preferred_element_type=f32)

    @pl.when(i >= n_used_ref[0])
    def _():
        o_ref[...] = jnp.zeros(o_ref.shape, f32)


def _experts(blk_e, n_used, xs, wg, wu, wd, interpret=False):
    r = xs.shape[0]
    tm = EXPERT_TILE
    wspec = lambda i, be, nu: (be[i], 0, 0)
    return pl.pallas_call(
        _expert_kernel,
        out_shape=jax.ShapeDtypeStruct((r, D_MODEL), f32),
        grid_spec=pltpu.PrefetchScalarGridSpec(
            num_scalar_prefetch=2,
            grid=(r // tm,),
            in_specs=[pl.BlockSpec((tm, D_MODEL), lambda i, be, nu: (i, 0)),
                      pl.BlockSpec((1, D_MODEL, D_EXPERT), wspec),
                      pl.BlockSpec((1, D_MODEL, D_EXPERT), wspec),
                      pl.BlockSpec((1, D_EXPERT, D_MODEL), wspec)],
            out_specs=pl.BlockSpec((tm, D_MODEL), lambda i, be, nu: (i, 0)),
            scratch_shapes=[pltpu.VMEM((tm, D_EXPERT), bf16)]),
        compiler_params=_params(("arbitrary",)),
        name="moe_experts",
        interpret=interpret,
    )(blk_e, n_used, xs, wg, wu, wd)


def _combine_ln_kernel(x_ref, y0_ref, y1_ref, w_ref, g_ref, b_ref, o_ref):
    w = w_ref[...]
    v = DEEPNORM_ALPHA * x_ref[...] + (y0_ref[...] * w[:, 0:1] + y1_ref[...] * w[:, 1:2])
    o_ref[...] = _layer_norm_rows(v, g_ref[...], b_ref[...])


def _combine_ln(x, y0, y1, w_col, g, b, interpret=False):
    n = x.shape[0]
    tm = TOKEN_TILE
    row = lambda i: (i, 0)
    const = lambda i: (0, 0)
    return pl.pallas_call(
        _combine_ln_kernel,
        out_shape=jax.ShapeDtypeStruct((n, D_MODEL), f32),
        grid=(n // tm,),
        in_specs=[pl.BlockSpec((tm, D_MODEL), row),
                  pl.BlockSpec((tm, D_MODEL), row),
                  pl.BlockSpec((tm, D_MODEL), row),
                  pl.BlockSpec((tm, SUBLANES), row),
                  pl.BlockSpec((1, D_MODEL), const),
                  pl.BlockSpec((1, D_MODEL), const)],
        out_specs=pl.BlockSpec((tm, D_MODEL), row),
        compiler_params=_params(("parallel",)),
        name="moe_combine_postnorm",
        interpret=interpret,
    )(x, y0, y1, w_col, g, b)


def _moe(x, router_wt, router_b, wg, wu, wd, g, b, interpret=False):
    n = x.shape[0]
    e_idx, e_w = _router(x, router_wt, router_b, interpret)
    flat_e = e_idx[:MOE_TOP_K].reshape(-1)
    n_assign = n * MOE_TOP_K
    onehot = (flat_e[:, None] == jnp.arange(N_EXPERTS, dtype=jnp.int32)[None, :]).astype(jnp.int32)
    csum = jnp.cumsum(onehot, axis=0)
    rank = jnp.take_along_axis(csum, flat_e[:, None], axis=1)[:, 0] - 1
    counts = csum[-1]
    tm = EXPERT_TILE
    padded = (counts + tm - 1) // tm * tm
    pad_end = jnp.cumsum(padded)
    pad_start = pad_end - padded
    dest = pad_start[flat_e] + rank
    n_blk = (n_assign + N_EXPERTS * (tm - 1)) // tm + 1
    tok = jnp.tile(jnp.arange(n, dtype=jnp.int32), MOE_TOP_K)
    row_tok = jnp.zeros((n_blk * tm,), jnp.int32).at[dest].set(tok)
    blk_e = jnp.minimum(jnp.searchsorted(pad_end, jnp.arange(n_blk, dtype=jnp.int32) * tm, side='right'),
                        N_EXPERTS - 1).astype(jnp.int32)
    n_used = (pad_end[-1:] // tm).astype(jnp.int32)
    xs = jnp.take(x, row_tok, axis=0)
    ys = _experts(blk_e, n_used, xs, wg, wu, wd, interpret)
    y0 = jnp.take(ys, dest[:n], axis=0)
    y1 = jnp.take(ys, dest[n:], axis=0)
    return _combine_ln(x, y0, y1, e_w.T, g, b, interpret)


def _rope_slab(t, cos, sin_signed, lo_half):
    swapped = jnp.where(lo_half, pltpu.roll(t, LANES - HEAD_DIM // 2, 1),
                        pltpu.roll(t, HEAD_DIM // 2, 1))
    return t * cos + swapped * sin_signed


def _qkv_kernel(x_ref, wq_ref, wk_ref, wv_ref, cos_ref, sin_ref, q_ref, k_ref, v_ref):
    xb = x_ref[...].astype(bf16)
    cos = cos_ref[...]
    sin = sin_ref[...]
    lane = lax.broadcasted_iota(jnp.int32, (1, LANES), 1)
    lo_half = (lane % HEAD_DIM) < (HEAD_DIM // 2)
    for w_ref, o_ref in ((wq_ref, q_ref), (wk_ref, k_ref)):
        for c in range(0, D_MODEL, COL_CHUNK):
            t = jnp.dot(xb, w_ref[:, c:c + COL_CHUNK], preferred_element_type=f32)
            for s in range(0, COL_CHUNK, LANES):
                o_ref[:, c + s:c + s + LANES] = _rope_slab(t[:, s:s + LANES], cos, sin, lo_half)
    for c in range(0, D_MODEL, COL_CHUNK):
        v_ref[:, c:c + COL_CHUNK] = jnp.dot(xb, wv_ref[:, c:c + COL_CHUNK],
                                            preferred_element_type=f32)


def _qkv(x, wq, wk, wv, cos, sin, interpret=False):
    n = x.shape[0]
    tm = TOKEN_TILE
    row = lambda i: (i, 0)
    const = lambda i: (0, 0)
    out = jax.ShapeDtypeStruct((n, D_MODEL), f32)
    return pl.pallas_call(
        _qkv_kernel,
        out_shape=(out, out, out),
        grid=(n // tm,),
        in_specs=[pl.BlockSpec((tm, D_MODEL), row),
                  pl.BlockSpec((D_MODEL, D_MODEL), const),
                  pl.BlockSpec((D_MODEL, D_MODEL), const),
                  pl.BlockSpec((D_MODEL, D_MODEL), const),
                  pl.BlockSpec((tm, LANES), row),
                  pl.BlockSpec((tm, LANES), row)],
        out_specs=(pl.BlockSpec((tm, D_MODEL), row),) * 3,
        compiler_params=_params(("parallel",)),
        name="qkv_rope",
        interpret=interpret,
    )(x, wq, wk, wv, cos, sin)


def _rope_tables(pos):
    half = HEAD_DIM // 2
    inv = ROPE_THETA ** (-jnp.arange(half, dtype=f32) / half)
    ang = pos.astype(f32)[:, None] * inv
    cos, sin = jnp.cos(ang), jnp.sin(ang)
    cos_t = jnp.tile(cos, (1, LANES // half))
    sin_t = jnp.tile(jnp.concatenate([-sin, sin], axis=1), (1, LANES // HEAD_DIM))
    return cos_t, sin_t


def _moba_prompt_kernel(q_ref, k_ref, v_ref, o_ref, kmean, m_scr, l_scr, acc_scr, *, nb):
    BLK = MOBA_BLOCK
    i = pl.program_id(2)
    lane = lax.broadcasted_iota(jnp.int32, (1, LANES), 1)
    lo = lane < HEAD_DIM

    @pl.when(i == 0)
    def _means():
        for j in range(nb):
            kmean[j:j + 1, :] = jnp.mean(k_ref[j * BLK:(j + 1) * BLK, :], axis=0, keepdims=True)

    qp = q_ref[...]
    row = lax.broadcasted_iota(jnp.int32, (BLK, BLK), 0)
    col = lax.broadcasted_iota(jnp.int32, (BLK, BLK), 1)
    diag_mask = col <= row
    km = kmean[...]
    blk_lane = lax.broadcasted_iota(jnp.int32, (1, nb), 1)
    scale = HEAD_DIM ** -0.5

    for hh in range(2):
        qh = jnp.where(lo, qp, 0.0) if hh == 0 else jnp.where(lo, 0.0, qp)
        gate = lax.dot_general(qh, km, (((1,), (1,)), ((), ())),
                               preferred_element_type=f32, precision=HIGHEST)
        gate = jnp.where(blk_lane < i, gate, -jnp.inf)
        cnt = jnp.zeros((BLK, nb), f32)
        for j in range(nb):
            cj = gate[:, j:j + 1]
            ahead = (cj > gate) | ((cj == gate) & (blk_lane > j))
            cnt = cnt + jnp.where(ahead, 1.0, 0.0)
        sel = jnp.where((cnt < MOBA_TOPK) & (blk_lane < i), 1.0, 0.0)

        qs = (qh * scale).astype(bf16)
        start = pl.multiple_of(i * BLK, BLK)
        k_own = k_ref[pl.ds(start, BLK), :].astype(bf16)
        v_own = v_ref[pl.ds(start, BLK), :].astype(bf16)
        s = lax.dot_general(qs, k_own, (((1,), (1,)), ((), ())), preferred_element_type=f32)
        s = jnp.where(diag_mask, s, -jnp.inf)
        m0 = jnp.max(s, axis=-1, keepdims=True)
        p = jnp.exp(s - m0)
        m_scr[hh] = m0
        l_scr[hh] = jnp.sum(p, axis=-1, keepdims=True)
        acc_scr[hh] = jnp.dot(p.astype(bf16), v_own, preferred_element_type=f32)

        for j in range(nb - 1):
            @pl.when(j < i)
            def _past(j=j, hh=hh, qs=qs, sel=sel):
                kj = k_ref[j * BLK:(j + 1) * BLK, :].astype(bf16)
                vj = v_ref[j * BLK:(j + 1) * BLK, :].astype(bf16)
                sj = lax.dot_general(qs, kj, (((1,), (1,)), ((), ())), preferred_element_type=f32)
                sj = jnp.where(sel[:, j:j + 1] > 0.5, sj, -jnp.inf)
                m_old = m_scr[hh]
                m_new = jnp.maximum(m_old, jnp.max(sj, axis=-1, keepdims=True))
                alpha = jnp.exp(m_old - m_new)
                pj = jnp.exp(sj - m_new)
                l_scr[hh] = alpha * l_scr[hh] + jnp.sum(pj, axis=-1, keepdims=True)
                acc_scr[hh] = alpha * acc_scr[hh] + jnp.dot(pj.astype(bf16), vj,
                                                            preferred_element_type=f32)
                m_scr[hh] = m_new

    o0 = acc_scr[0] / l_scr[0]
    o1 = acc_scr[1] / l_scr[1]
    o_ref[...] = jnp.where(lo, o0, o1)


def _moba_prompt(q, k, v, n_seq, seq_len, interpret=False):
    BLK = MOBA_BLOCK
    nb = seq_len // BLK
    return pl.pallas_call(
        functools.partial(_moba_prompt_kernel, nb=nb),
        out_shape=jax.ShapeDtypeStruct((n_seq * seq_len, D_MODEL), f32),
        grid=(n_seq, HEAD_PAIRS, nb),
        in_specs=[pl.BlockSpec((BLK, LANES), lambda b, hp, i: (b * nb + i, hp)),
                  pl.BlockSpec((seq_len, LANES), lambda b, hp, i: (b, hp)),
                  pl.BlockSpec((seq_len, LANES), lambda b, hp, i: (b, hp))],
        out_specs=pl.BlockSpec((BLK, LANES), lambda b, hp, i: (b * nb + i, hp)),
        scratch_shapes=[pltpu.VMEM((nb, LANES), f32),
                        pltpu.VMEM((2, BLK, 1), f32),
                        pltpu.VMEM((2, BLK, 1), f32),
                        pltpu.VMEM((2, BLK, LANES), f32)],
        compiler_params=_params(("parallel", "parallel", "arbitrary")),
        name="moba_prompt_attn",
        interpret=interpret,
    )(q, k, v)


def _page_sum_kernel(pt_ref, *refs):
    pages, o_ref = refs[:PAGES_PER_STEP], refs[PAGES_PER_STEP]
    ppb = MOBA_BLOCK // PAGE_SIZE
    for blk in range(PAGES_PER_STEP // ppb):
        acc = None
        for r in range(ppb):
            s = jnp.sum(pages[blk * ppb + r][0], axis=1)
            acc = s if acc is None else acc + s
        o_ref[0, blk] = acc


def _page_sums(page_table, cache_k, n_blocks, interpret=False):
    bsz, n_pages = page_table.shape
    ppb = MOBA_BLOCK // PAGE_SIZE
    steps = n_blocks * ppb // PAGES_PER_STEP
    bps = PAGES_PER_STEP // ppb
    pt_flat = page_table.reshape(-1)

    def page_map(r):
        return lambda b, s, pt: (pt[b * n_pages + s * PAGES_PER_STEP + r], 0, 0, 0)

    return pl.pallas_call(
        _page_sum_kernel,
        out_shape=jax.ShapeDtypeStruct((bsz, n_blocks, N_HEADS, HEAD_DIM), f32),
        grid_spec=pltpu.PrefetchScalarGridSpec(
            num_scalar_prefetch=1,
            grid=(bsz, steps),
            in_specs=[pl.BlockSpec((1, N_HEADS, PAGE_SIZE, HEAD_DIM), page_map(r))
                      for r in range(PAGES_PER_STEP)],
            out_specs=pl.BlockSpec((1, bps, N_HEADS, HEAD_DIM), lambda b, s, pt: (b, s, 0, 0))),
        compiler_params=_params(("parallel", "arbitrary")),
        name="moba_page_sums",
        interpret=interpret,
    )(pt_flat, *([cache_k] * PAGES_PER_STEP))


def _sample_topk_kernel(q_ref, km_ref, idx_ref, *, n_blocks, topk):
    lane = lax.broadcasted_iota(jnp.int32, (1, n_blocks), 1)
    out_lane = lax.broadcasted_iota(jnp.int32, (1, LANES), 1)
    for h in range(N_HEADS):
        g = lax.dot_general(q_ref[0, h], km_ref[0, h], (((1,), (1,)), ((), ())),
                            preferred_element_type=f32, precision=HIGHEST)
        out = jnp.zeros((g.shape[0], LANES), jnp.int32)
        for t in range(topk):
            m = jnp.max(g, axis=-1, keepdims=True)
            idx = jnp.min(jnp.where(g == m, lane, n_blocks), axis=-1, keepdims=True)
            out = jnp.where(out_lane == t, idx, out)
            g = jnp.where(lane == idx, -jnp.inf, g)
        idx_ref[0, h] = out


def _sample_topk(q4, kmean, topk, interpret=False):
    bsz, _, nq, _ = q4.shape
    n_blocks = kmean.shape[2]
    return pl.pallas_call(
        functools.partial(_sample_topk_kernel, n_blocks=n_blocks, topk=topk),
        out_shape=jax.ShapeDtypeStruct((bsz, N_HEADS, nq, LANES), jnp.int32),
        grid=(bsz,),
        in_specs=[pl.BlockSpec((1, N_HEADS, nq, HEAD_DIM), lambda b: (b, 0, 0, 0)),
                  pl.BlockSpec((1, N_HEADS, n_blocks, HEAD_DIM), lambda b: (b, 0, 0, 0))],
        out_specs=pl.BlockSpec((1, N_HEADS, nq, LANES), lambda b: (b, 0, 0, 0)),
        compiler_params=_params(("parallel",)),
        name="moba_sample_topk",
        interpret=interpret,
    )(q4, kmean)


def _sample_attn_kernel(phys_ref, q_ref, kn_ref, vn_ref, *refs, n_sel):
    k_pages = refs[:n_sel]
    v_pages = refs[n_sel:2 * n_sel]
    o_ref = refs[2 * n_sel]
    qi = pl.program_id(2)
    nq = q_ref.shape[2]
    scale = HEAD_DIM ** -0.5
    q_row = q_ref[0, 0, pl.ds(qi, 1), :] * scale
    qb = jnp.broadcast_to(q_row, (SUBLANES, HEAD_DIM)).astype(bf16)
    dims = (((1,), (1,)), ((), ()))
    s_own = lax.dot_general(qb, kn_ref[0, 0].astype(bf16), dims, preferred_element_type=f32)
    key = lax.broadcasted_iota(jnp.int32, (SUBLANES, nq), 1)
    s_own = jnp.where(key <= qi, s_own, -jnp.inf)
    s_sel = [lax.dot_general(qb, kp[0, 0].astype(bf16), dims, preferred_element_type=f32)
             for kp in k_pages]
    m = jnp.max(s_own, axis=-1, keepdims=True)
    for s in s_sel:
        m = jnp.maximum(m, jnp.max(s, axis=-1, keepdims=True))
    p_own = jnp.exp(s_own - m)
    den = jnp.sum(p_own, axis=-1, keepdims=True)
    acc = jnp.dot(p_own.astype(bf16), vn_ref[0, 0].astype(bf16), preferred_element_type=f32)
    for s, vp in zip(s_sel, v_pages):
        p = jnp.exp(s - m)
        den = den + jnp.sum(p, axis=-1, keepdims=True)
        acc = acc + jnp.dot(p.astype(bf16), vp[0, 0].astype(bf16), preferred_element_type=f32)
    o_ref[0, 0, pl.ds(qi, 1), :] = (acc / den)[0:1]


def _sample_attn(phys, q4, k_new, v_new, cache_k, cache_v, n_sel, interpret=False):
    bsz, _, nq, _ = q4.shape

    def page_map(r):
        return lambda b, h, qi, ph: (ph[((b * N_HEADS + h) * nq + qi) * n_sel + r], h, 0, 0)

    small = pl.BlockSpec((1, 1, nq, HEAD_DIM), lambda b, h, qi, ph: (b, h, 0, 0))
    page_specs = [pl.BlockSpec((1, 1, PAGE_SIZE, HEAD_DIM), page_map(r)) for r in range(n_sel)]
    return pl.pallas_call(
        functools.partial(_sample_attn_kernel, n_sel=n_sel),
        out_shape=jax.ShapeDtypeStruct((bsz, N_HEADS, nq, HEAD_DIM), f32),
        grid_spec=pltpu.PrefetchScalarGridSpec(
            num_scalar_prefetch=1,
            grid=(bsz, N_HEADS, nq),
            in_specs=[small, small, small] + page_specs + page_specs,
            out_specs=small),
        compiler_params=_params(("parallel", "parallel", "arbitrary")),
        name="moba_sample_attn",
        interpret=interpret,
    )(phys, q4, k_new, v_new, *([cache_k] * n_sel), *([cache_v] * n_sel))


def _heads_first(t, bsz, seq_len):
    return t.reshape(bsz, seq_len, N_HEADS, HEAD_DIM).transpose(0, 2, 1, 3)


def _forward(x_prompt, x_sample, state_ssm, state_conv, cache_k, cache_v, page_table,
             ln_gain, ln_bias, ssm_w_in, ssm_w_conv, ssm_b_conv, ssm_dt_bias, ssm_a_log,
             ssm_d, ssm_norm_w, ssm_w_out, attn_w_q, attn_w_kv, attn_w_o,
             router_w, router_b, moe_w_gate, moe_w_up, moe_w_down, interpret=False):
    bp, lp, _ = x_prompt.shape
    bs, ls, _ = x_sample.shape
    n_p, n_s = bp * lp, bs * ls
    n = n_p + n_s
    n_pages = page_table.shape[1]
    past_len = n_pages * PAGE_SIZE
    assert n % TOKEN_TILE == 0 and lp % MOBA_BLOCK == 0 and lp % SSD_CHUNK == 0
    assert CONV_W - 1 <= ls <= SUBLANES and n_p % ls == 0
    assert past_len % MOBA_BLOCK == 0 and past_len // MOBA_BLOCK >= MOBA_TOPK
    assert ls <= MOBA_BLOCK - past_len % MOBA_BLOCK

    h = jnp.concatenate([x_prompt.reshape(n_p, D_MODEL), x_sample.reshape(n_s, D_MODEL)], axis=0)
    router_wt = router_w.T
    router_bc = router_b.reshape(N_EXPERTS, 1)

    def vec(v):
        return v.reshape(1, -1)

    l = 0
    w_in = ssm_w_in[l]
    wz = w_in[:, :D_INNER].astype(bf16)
    wx = w_in[:, D_INNER:D_INNER + CONV_DIM].astype(bf16)
    wdt = jnp.pad(w_in[:, D_INNER + CONV_DIM:], ((0, 0), (0, LANES - SSM_HEADS)))
    z, xbc, dt = _in_proj(h, wz, wx, wdt, interpret)

    pad_h = (0, LANES - SSM_HEADS)
    wconv = jnp.pad(ssm_w_conv[l], ((0, SUBLANES - CONV_W), (0, 0)))
    ssd_w = (wconv, vec(ssm_b_conv[l]), vec(jnp.pad(ssm_dt_bias[l], pad_h)),
             vec(jnp.pad(ssm_a_log[l], pad_h)), vec(jnp.repeat(ssm_d[l], SSM_HEAD_DIM)),
             vec(ssm_norm_w[l]))
    y, st_p = _ssd(z, xbc, dt, None, None, None, *ssd_w, n_seq=bp, seq_len=lp, row_off=0,
                   interpret=interpret)
    conv0 = jnp.pad(state_conv[l], ((0, 0), (SUBLANES - (CONV_W - 1), 0), (0, 0)))
    y, st_s = _ssd(z, xbc, dt, conv0, _state_to_pairs(state_ssm[l]), y, *ssd_w,
                   n_seq=bs, seq_len=ls, row_off=n_p, interpret=interpret)
    ssm_prompt = _pairs_to_state(st_p)[None]
    ssm_sample = _pairs_to_state(st_s)[None]
    conv_prompt = xbc[:n_p].reshape(bp, lp, CONV_DIM)[:, lp - (CONV_W - 1):][None]
    conv_sample = xbc[n_p:].reshape(bs, ls, CONV_DIM)[:, ls - (CONV_W - 1):][None]

    h = _proj_ln(y, ssm_w_out[l].astype(bf16), h, vec(ln_gain[l, 0]), vec(ln_bias[l, 0]), interpret)
    h = _moe(h, router_wt, router_bc, moe_w_gate[l].astype(bf16), moe_w_up[l].astype(bf16),
             moe_w_down[l].astype(bf16), vec(ln_gain[l, 1]), vec(ln_bias[l, 1]), interpret)

    l = 1
    pos = jnp.concatenate([jnp.tile(jnp.arange(lp), bp), jnp.tile(past_len + jnp.arange(ls), bs)])
    cos_t, sin_t = _rope_tables(pos)
    hd_all = N_HEADS * HEAD_DIM
    q, k, v = _qkv(h, attn_w_q[0].astype(bf16), attn_w_kv[:, :hd_all].astype(bf16),
                   attn_w_kv[:, hd_all:].astype(bf16), cos_t, sin_t, interpret)
    k_prompt = _heads_first(k[:n_p], bp, lp)
    v_prompt = _heads_first(v[:n_p], bp, lp)
    k_sample = _heads_first(k[n_p:], bs, ls)
    v_sample = _heads_first(v[n_p:], bs, ls)
    q_sample = _heads_first(q[n_p:], bs, ls)

    o_p = _moba_prompt(q, k, v, bp, lp, interpret)

    n_full = past_len // MOBA_BLOCK
    ppb = MOBA_BLOCK // PAGE_SIZE
    ksum = _page_sums(page_table, cache_k, n_full, interpret)
    kmean = ksum.transpose(0, 2, 1, 3) / MOBA_BLOCK
    idx = _sample_topk(q_sample, kmean, MOBA_TOPK, interpret)[..., :MOBA_TOPK]
    lpage = idx[..., None] * ppb + jnp.arange(ppb, dtype=jnp.int32)
    phys = page_table[jnp.arange(bs)[:, None, None, None, None], lpage]
    o_s4 = _sample_attn(phys.reshape(-1).astype(jnp.int32), q_sample, k_sample, v_sample,
                        cache_k, cache_v, MOBA_TOPK * ppb, interpret)
    o_s = o_s4.transpose(0, 2, 1, 3).reshape(n_s, D_MODEL)
    o = jnp.concatenate([o_p, o_s], axis=0)

    h = _proj_ln(o, attn_w_o[0].astype(bf16), h, vec(ln_gain[l, 0]), vec(ln_bias[l, 0]), interpret)
    h = _moe(h, router_wt, router_bc, moe_w_gate[l].astype(bf16), moe_w_up[l].astype(bf16),
             moe_w_down[l].astype(bf16), vec(ln_gain[l, 1]), vec(ln_bias[l, 1]), interpret)

    y_prompt = h[:n_p].reshape(bp, lp, D_MODEL)
    y_sample = h[n_p:].reshape(bs, ls, D_MODEL)
    return (y_prompt, y_sample, ssm_prompt, conv_prompt, k_prompt, v_prompt,
            ssm_sample, conv_sample, k_sample, v_sample)


def kernel(x_prompt, x_sample, state_ssm, state_conv, cache_k, cache_v, page_table, ln_gain, ln_bias, ssm_w_in, ssm_w_conv, ssm_b_conv, ssm_dt_bias, ssm_a_log, ssm_d, ssm_norm_w, ssm_w_out, attn_w_q, attn_w_kv, attn_w_o, router_w, router_b, moe_w_gate, moe_w_up, moe_w_down):
    return _forward(x_prompt, x_sample, state_ssm, state_conv, cache_k, cache_v, page_table,
                    ln_gain, ln_bias, ssm_w_in, ssm_w_conv, ssm_b_conv, ssm_dt_bias, ssm_a_log,
                    ssm_d, ssm_norm_w, ssm_w_out, attn_w_q, attn_w_kv, attn_w_o,
                    router_w, router_b, moe_w_gate, moe_w_up, moe_w_down)
```

```python
import functools
import math

import jax
import jax.numpy as jnp
from jax import lax
from jax.experimental import pallas as pl
from jax.experimental.pallas import tpu as pltpu

f32 = jnp.float32
bf16 = jnp.bfloat16
HIGHEST = lax.Precision.HIGHEST

D_MODEL = 1024
DEPTH = 2
PAGE_SIZE = 128
N_A_LAYERS = 1
SSM_HEAD_DIM = 64
D_INNER = 2048
SSM_HEADS = D_INNER // SSM_HEAD_DIM
SSM_GROUPS = 4
D_STATE = 128
CONV_W = 4
CONV_DIM = D_INNER + 2 * SSM_GROUPS * D_STATE
SSD_CHUNK = 128
N_HEADS = 16
HEAD_DIM = 64
MOBA_BLOCK = 256
MOBA_TOPK = 3
ROPE_THETA = 10000.0
N_EXPERTS = 16
N_EXPERT_GROUPS = 4
EXPERTS_PER_GROUP = 4
MOE_TOP_K = 2
D_EXPERT = 1024
DEEPNORM_ALPHA = (2.0 * DEPTH) ** 0.25
LN_EPS = 1e-5
RMS_EPS = 1e-6
MASK_BIAS = -1e30

LANES = 128
SUBLANES = 8
VMEM_LIMIT = 48 * 1024 * 1024

TOKEN_TILE = 256
EXPERT_TILE = 256
COL_CHUNK = 512
PAGES_PER_STEP = 8
HEAD_PAIRS = N_HEADS // 2
SSM_PAIRS = SSM_HEADS // 2


def _params(sem):
    return pltpu.CompilerParams(dimension_semantics=sem, vmem_limit_bytes=VMEM_LIMIT)


def _sigmoid(x):
    return 1.0 / (1.0 + jnp.exp(-x))


def _softplus(x):
    return jnp.maximum(x, 0.0) + jnp.log1p(jnp.exp(-jnp.abs(x)))


def _layer_norm_rows(v, g, b):
    mu = jnp.mean(v, axis=-1, keepdims=True)
    d = v - mu
    var = jnp.mean(d * d, axis=-1, keepdims=True)
    return d * lax.rsqrt(var + LN_EPS) * g + b


def _in_proj_kernel(x_ref, wz_ref, wx_ref, wdt_ref, z_ref, xbc_ref, dt_ref):
    x = x_ref[...]
    xb = x.astype(bf16)
    for c in range(0, D_INNER, COL_CHUNK):
        z_ref[:, c:c + COL_CHUNK] = jnp.dot(xb, wz_ref[:, c:c + COL_CHUNK],
                                            preferred_element_type=f32)
    for c in range(0, CONV_DIM, COL_CHUNK):
        xbc_ref[:, c:c + COL_CHUNK] = jnp.dot(xb, wx_ref[:, c:c + COL_CHUNK],
                                              preferred_element_type=f32)
    dt_ref[...] = jnp.dot(x, wdt_ref[...], preferred_element_type=f32, precision=HIGHEST)


def _in_proj(x, wz, wx, wdt, interpret=False):
    n = x.shape[0]
    tm = TOKEN_TILE
    return pl.pallas_call(
        _in_proj_kernel,
        out_shape=(jax.ShapeDtypeStruct((n, D_INNER), f32),
                   jax.ShapeDtypeStruct((n, CONV_DIM), f32),
                   jax.ShapeDtypeStruct((n, LANES), f32)),
        grid=(n // tm,),
        in_specs=[pl.BlockSpec((tm, D_MODEL), lambda i: (i, 0)),
                  pl.BlockSpec((D_MODEL, D_INNER), lambda i: (0, 0)),
                  pl.BlockSpec((D_MODEL, CONV_DIM), lambda i: (0, 0)),
                  pl.BlockSpec((D_MODEL, LANES), lambda i: (0, 0))],
        out_specs=(pl.BlockSpec((tm, D_INNER), lambda i: (i, 0)),
                   pl.BlockSpec((tm, CONV_DIM), lambda i: (i, 0)),
                   pl.BlockSpec((tm, LANES), lambda i: (i, 0))),
        compiler_params=_params(("parallel",)),
        name="ssm_in_proj",
        interpret=interpret,
    )(x, wz, wx, wdt)


def _ssd_kernel(*refs, c_in, nc, has_init):
    T = SSD_CHUNK
    if has_init:
        (z_ref, xbc_ref, dt_ref, conv0_ref, st0_ref, wconv_ref, bconv_ref, dtb_ref, alog_ref,
         d_ref, nw_ref, y_ref, st_out_ref, xbuf, xc, ybuf, st) = refs
    else:
        (z_ref, xbc_ref, dt_ref, wconv_ref, bconv_ref, dtb_ref, alog_ref,
         d_ref, nw_ref, y_ref, st_out_ref, xbuf, xc, ybuf, st) = refs
    c = pl.program_id(1)

    @pl.when(c == 0)
    def _init():
        if has_init:
            xbuf[0:SUBLANES, :] = conv0_ref[0]
            st[...] = st0_ref[0]
        else:
            xbuf[0:SUBLANES, :] = jnp.zeros((SUBLANES, CONV_DIM), f32)
            st[...] = jnp.zeros(st.shape, f32)

    xbuf[SUBLANES:SUBLANES + c_in, :] = xbc_ref[...]
    if c_in < T:
        xbuf[SUBLANES + c_in:SUBLANES + T, :] = jnp.zeros((T - c_in, CONV_DIM), f32)

    base = SUBLANES - (CONV_W - 1)
    for c0 in range(0, CONV_DIM, COL_CHUNK):
        cs = slice(c0, c0 + COL_CHUNK)
        acc = bconv_ref[:, cs] + xbuf[base:base + T, cs] * wconv_ref[0:1, cs]
        for k in range(1, CONV_W):
            acc = acc + xbuf[base + k:base + k + T, cs] * wconv_ref[k:k + 1, cs]
        xc[:, cs] = acc * _sigmoid(acc)
    if nc > 1:
        xbuf[0:SUBLANES, :] = xbuf[T:T + SUBLANES, :]

    def pad_rows(v):
        if c_in == T:
            return v
        return jnp.concatenate([v, jnp.zeros((T - c_in, v.shape[1]), v.dtype)], axis=0)

    dtv = pad_rows(_softplus(dt_ref[...] + dtb_ref[...]))
    a = -jnp.exp(alog_ref[...])
    da = dtv * a
    row = lax.broadcasted_iota(jnp.int32, (T, T), 0)
    col = lax.broadcasted_iota(jnp.int32, (T, T), 1)
    causal = col <= row
    tri = causal.astype(f32)
    acs = jnp.dot(tri, da, preferred_element_type=f32, precision=HIGHEST)
    acs_t = acs.T
    dt_t = dtv.T
    w_all = dt_t * jnp.exp(acs_t[:, T - 1:T] - acs_t)
    lo = lax.broadcasted_iota(jnp.int32, (1, LANES), 1) < SSM_HEAD_DIM

    for g in range(SSM_GROUPS):
        bm = xc[:, D_INNER + g * D_STATE:D_INNER + (g + 1) * D_STATE]
        cm = xc[:, D_INNER + (SSM_GROUPS + g) * D_STATE:D_INNER + (SSM_GROUPS + g + 1) * D_STATE]
        cb = lax.dot_general(cm.astype(bf16), bm.astype(bf16), (((1,), (1,)), ((), ())),
                             preferred_element_type=f32)
        bm_t = bm.T
        for e in range(SSM_PAIRS // SSM_GROUPS):
            pr = g * (SSM_PAIRS // SSM_GROUPS) + e
            ls = slice(pr * LANES, (pr + 1) * LANES)
            x_pair = xc[:, ls]
            xb = x_pair.astype(bf16)
            rhs = jnp.concatenate([xb, st[pr].astype(bf16)], axis=0)
            r, u, ea_last = [], [], []
            for hh in range(2):
                h = 2 * pr + hh
                a_b = jnp.broadcast_to(acs[:, h:h + 1], (T, LANES))
                seg = a_b - acs_t[h:h + 1, :]
                dec = jnp.exp(jnp.where(causal, seg, -jnp.inf))
                m_h = cb * dec * dt_t[h:h + 1, :]
                e_a = jnp.exp(a_b)
                lhs = jnp.concatenate([m_h.astype(bf16), (cm * e_a).astype(bf16)], axis=1)
                r.append(jnp.dot(lhs, rhs, preferred_element_type=f32))
                lhs_s = (bm_t * w_all[h:h + 1, :]).astype(bf16)
                u.append(jnp.dot(lhs_s, xb, preferred_element_type=f32))
                ea_last.append(e_a[T - 1:T, :])
            st[pr] = st[pr] * jnp.where(lo, ea_last[0], ea_last[1]) + jnp.where(lo, u[0], u[1])
            yv = jnp.where(lo, r[0], r[1]) + d_ref[:, ls] * x_pair
            zz = pad_rows(z_ref[:, ls])
            ybuf[:, ls] = yv * (zz * _sigmoid(zz))
        gw = D_INNER // SSM_GROUPS
        gs = slice(g * gw, (g + 1) * gw)
        yg = ybuf[:, gs]
        ms = jnp.mean(yg * yg, axis=-1, keepdims=True)
        y_ref[:, gs] = (yg * lax.rsqrt(ms + RMS_EPS) * nw_ref[:, gs])[0:c_in]

    @pl.when(c == nc - 1)
    def _fin():
        st_out_ref[0] = st[...]


def _ssd(z, xbc, dt, conv0, st0, wconv, bconv, dtb, alog, d_exp, nw, *,
         n_seq, seq_len, row_off, interpret=False):
    T = SSD_CHUNK
    has_init = conv0 is not None
    if seq_len >= T:
        c_in, nc = T, seq_len // T
    else:
        c_in, nc = seq_len, 1
    off = row_off // c_in

    def rows(bi, ci):
        return (off + bi * nc + ci, 0)

    const = lambda bi, ci: (0, 0)
    in_specs = [pl.BlockSpec((c_in, D_INNER), rows),
                pl.BlockSpec((c_in, CONV_DIM), rows),
                pl.BlockSpec((c_in, LANES), rows)]
    args = [z, xbc, dt]
    if has_init:
        in_specs += [pl.BlockSpec((1, SUBLANES, CONV_DIM), lambda bi, ci: (bi, 0, 0)),
                     pl.BlockSpec((1, SSM_PAIRS, D_STATE, LANES), lambda bi, ci: (bi, 0, 0, 0))]
        args += [conv0, st0]
    in_specs += [pl.BlockSpec((SUBLANES, CONV_DIM), const),
                 pl.BlockSpec((1, CONV_DIM), const),
                 pl.BlockSpec((1, LANES), const),
                 pl.BlockSpec((1, LANES), const),
                 pl.BlockSpec((1, D_INNER), const),
                 pl.BlockSpec((1, D_INNER), const)]
    args += [wconv, bconv, dtb, alog, d_exp, nw]
    return pl.pallas_call(
        functools.partial(_ssd_kernel, c_in=c_in, nc=nc, has_init=has_init),
        out_shape=(jax.ShapeDtypeStruct((n_seq * seq_len, D_INNER), f32),
                   jax.ShapeDtypeStruct((n_seq, SSM_PAIRS, D_STATE, LANES), f32)),
        grid=(n_seq, nc),
        in_specs=in_specs,
        out_specs=(pl.BlockSpec((c_in, D_INNER), lambda bi, ci: (bi * nc + ci, 0)),
                   pl.BlockSpec((1, SSM_PAIRS, D_STATE, LANES), lambda bi, ci: (bi, 0, 0, 0))),
        scratch_shapes=[pltpu.VMEM((T + 2 * SUBLANES, CONV_DIM), f32),
                        pltpu.VMEM((T, CONV_DIM), f32),
                        pltpu.VMEM((T, D_INNER), f32),
                        pltpu.VMEM((SSM_PAIRS, D_STATE, LANES), f32)],
        compiler_params=_params(("parallel", "arbitrary")),
        name="ssd_scan_init" if has_init else "ssd_scan",
        interpret=interpret,
    )(*args)


def _state_to_pairs(s):
    b = s.shape[0]
    s = s.reshape(b, SSM_PAIRS, 2, SSM_HEAD_DIM, D_STATE)
    return s.transpose(0, 1, 4, 2, 3).reshape(b, SSM_PAIRS, D_STATE, 2 * SSM_HEAD_DIM)


def _pairs_to_state(s):
    b = s.shape[0]
    s = s.reshape(b, SSM_PAIRS, D_STATE, 2, SSM_HEAD_DIM)
    return s.transpose(0, 1, 3, 4, 2).reshape(b, SSM_HEADS, SSM_HEAD_DIM, D_STATE)


def _proj_ln_kernel(yp_ref, ys_ref, w_ref, x_ref, g_ref, b_ref, o_ref, *, split_blk):
    y = jnp.where(pl.program_id(0) < split_blk, yp_ref[...], ys_ref[...])
    yb = y.astype(bf16)
    for c in range(0, D_MODEL, COL_CHUNK):
        cs = slice(c, c + COL_CHUNK)
        o_ref[:, cs] = DEEPNORM_ALPHA * x_ref[:, cs] + jnp.dot(yb, w_ref[:, cs],
                                                               preferred_element_type=f32)
    o_ref[...] = _layer_norm_rows(o_ref[...], g_ref[...], b_ref[...])


def _proj_ln(y_p, y_s, w, x, g, b, interpret=False):
    n = x.shape[0]
    k = y_p.shape[1]
    tm = TOKEN_TILE
    assert y_p.shape[0] % tm == 0 and y_s.shape[0] % tm == 0
    split_blk = y_p.shape[0] // tm
    return pl.pallas_call(
        functools.partial(_proj_ln_kernel, split_blk=split_blk),
        out_shape=jax.ShapeDtypeStruct((n, D_MODEL), f32),
        grid=(n // tm,),
        in_specs=[pl.BlockSpec((tm, k), lambda i: (jnp.minimum(i, split_blk - 1), 0)),
                  pl.BlockSpec((tm, k), lambda i: (jnp.maximum(i - split_blk, 0), 0)),
                  pl.BlockSpec((k, D_MODEL), lambda i: (0, 0)),
                  pl.BlockSpec((tm, D_MODEL), lambda i: (i, 0)),
                  pl.BlockSpec((1, D_MODEL), lambda i: (0, 0)),
                  pl.BlockSpec((1, D_MODEL), lambda i: (0, 0))],
        out_specs=pl.BlockSpec((tm, D_MODEL), lambda i: (i, 0)),
        compiler_params=_params(("parallel",)),
        name="proj_postnorm",
        interpret=interpret,
    )(y_p, y_s, w, x, g, b)


def _router_kernel(x_ref, wt_ref, b_ref, e_ref, w_ref):
    logits = lax.dot_general(wt_ref[...], x_ref[...], (((1,), (1,)), ((), ())),
                             preferred_element_type=f32, precision=HIGHEST)
    s = _sigmoid(logits)
    sb = s + b_ref[...]
    srow = [s[i:i + 1, :] for i in range(N_EXPERTS)]
    brow = [sb[i:i + 1, :] for i in range(N_EXPERTS)]
    gscore = []
    for g in range(N_EXPERT_GROUPS):
        v = brow[g * EXPERTS_PER_GROUP:(g + 1) * EXPERTS_PER_GROUP]
        best = None
        for i in range(EXPERTS_PER_GROUP):
            for j in range(i + 1, EXPERTS_PER_GROUP):
                hi = jnp.maximum(v[i], v[j])
                lo_ = jnp.minimum(v[i], v[j])
                p = hi + lo_
                best = p if best is None else jnp.maximum(best, p)
        gscore.append(best)
    gi = jnp.zeros_like(gscore[0], dtype=jnp.int32)
    gbest = gscore[0]
    for g in range(1, N_EXPERT_GROUPS):
        upd = gscore[g] > gbest
        gi = jnp.where(upd, g, gi)
        gbest = jnp.where(upd, gscore[g], gbest)
    vb, vs = [], []
    for k in range(EXPERTS_PER_GROUP):
        tb, ts = brow[k], srow[k]
        for g in range(1, N_EXPERT_GROUPS):
            tb = jnp.where(gi == g, brow[g * EXPERTS_PER_GROUP + k], tb)
            ts = jnp.where(gi == g, srow[g * EXPERTS_PER_GROUP + k], ts)
        vb.append(tb)
        vs.append(ts)
    i1 = jnp.zeros_like(gi)
    b1, s1 = vb[0], vs[0]
    for k in range(1, EXPERTS_PER_GROUP):
        upd = vb[k] > b1
        i1 = jnp.where(upd, k, i1)
        b1 = jnp.where(upd, vb[k], b1)
        s1 = jnp.where(upd, vs[k], s1)
    i2 = jnp.full_like(gi, -1)
    b2 = jnp.full_like(b1, -jnp.inf)
    s2 = jnp.zeros_like(s1)
    for k in range(EXPERTS_PER_GROUP):
        upd = (i1 != k) & ((vb[k] > b2) | (i2 < 0))
        i2 = jnp.where(upd, k, i2)
        b2 = jnp.where(upd, vb[k], b2)
        s2 = jnp.where(upd, vs[k], s2)
    den = s1 + s2
    tm = gi.shape[1]
    zi = jnp.zeros((SUBLANES - 2, tm), jnp.int32)
    zf = jnp.zeros((SUBLANES - 2, tm), f32)
    e_ref[...] = jnp.concatenate([gi * EXPERTS_PER_GROUP + i1, gi * EXPERTS_PER_GROUP + i2, zi], axis=0)
    w_ref[...] = jnp.concatenate([s1 / den, s2 / den, zf], axis=0)


def _router(x, wt, b, interpret=False):
    n = x.shape[0]
    tm = TOKEN_TILE
    return pl.pallas_call(
        _router_kernel,
        out_shape=(jax.ShapeDtypeStruct((SUBLANES, n), jnp.int32),
                   jax.ShapeDtypeStruct((SUBLANES, n), f32)),
        grid=(n // tm,),
        in_specs=[pl.BlockSpec((tm, D_MODEL), lambda i: (i, 0)),
                  pl.BlockSpec((N_EXPERTS, D_MODEL), lambda i: (0, 0)),
                  pl.BlockSpec((N_EXPERTS, 1), lambda i: (0, 0))],
        out_specs=(pl.BlockSpec((SUBLANES, tm), lambda i: (0, i)),
                   pl.BlockSpec((SUBLANES, tm), lambda i: (0, i))),
        compiler_params=_params(("parallel",)),
        name="moe_router",
        interpret=interpret,
    )(x, wt, b)


def _expert_kernel(blk_e_ref, n_used_ref, x_ref, wg_ref, wu_ref, wd_ref, o_ref, h_scr):
    i = pl.program_id(0)

    @pl.when(i < n_used_ref[0])
    def _():
        xb = x_ref[...].astype(bf16)
        for c in range(0, D_EXPERT, COL_CHUNK):
            cs = slice(c, c + COL_CHUNK)
            hg = jnp.dot(xb, wg_ref[0, :, cs], preferred_element_type=f32)
            hu = jnp.dot(xb, wu_ref[0, :, cs], preferred_element_type=f32)
            h_scr[:, cs] = (hg * _sigmoid(hg) * hu).astype(bf16)
        hb = h_scr[...]
        for c in range(0, D_MODEL, COL_CHUNK):
            cs = slice(c, c + COL_CHUNK)
            o_ref[:, cs] = jnp.dot(hb, wd_ref[0, :, cs], preferred_element_type=f32)

    @pl.when(i >= n_used_ref[0])
    def _():
        o_ref[...] = jnp.zeros(o_ref.shape, f32)


def _experts(blk_e, n_used, xs, wg, wu, wd, interpret=False):
    r = xs.shape[0]
    tm = EXPERT_TILE
    wspec = lambda i, be, nu: (be[i], 0, 0)
    return pl.pallas_call(
        _expert_kernel,
        out_shape=jax.ShapeDtypeStruct((r, D_MODEL), f32),
        grid_spec=pltpu.PrefetchScalarGridSpec(
            num_scalar_prefetch=2,
            grid=(r // tm,),
            in_specs=[pl.BlockSpec((tm, D_MODEL), lambda i, be, nu: (i, 0)),
                      pl.BlockSpec((1, D_MODEL, D_EXPERT), wspec),
                      pl.BlockSpec((1, D_MODEL, D_EXPERT), wspec),
                      pl.BlockSpec((1, D_EXPERT, D_MODEL), wspec)],
            out_specs=pl.BlockSpec((tm, D_MODEL), lambda i, be, nu: (i, 0)),
            scratch_shapes=[pltpu.VMEM((tm, D_EXPERT), bf16)]),
        compiler_params=_params(("arbitrary",)),
        name="moe_experts",
        interpret=interpret,
    )(blk_e, n_used, xs, wg, wu, wd)


def _combine_ln_kernel(x_ref, y0_ref, y1_ref, w_ref, g_ref, b_ref, *o_refs, split_blk):
    w = w_ref[...]
    v = DEEPNORM_ALPHA * x_ref[...] + (y0_ref[...] * w[:, 0:1] + y1_ref[...] * w[:, 1:2])
    res = _layer_norm_rows(v, g_ref[...], b_ref[...])
    if split_blk is None:
        o_refs[0][...] = res
    else:
        i = pl.program_id(0)

        @pl.when(i < split_blk)
        def _():
            o_refs[0][...] = res

        @pl.when(i >= split_blk)
        def _():
            o_refs[1][...] = res


def _combine_ln(x, y0, y1, w_col, g, b, split=None, interpret=False):
    n = x.shape[0]
    tm = TOKEN_TILE
    row = lambda i: (i, 0)
    const = lambda i: (0, 0)
    if split is None:
        split_blk = None
        out_shape = jax.ShapeDtypeStruct((n, D_MODEL), f32)
        out_specs = pl.BlockSpec((tm, D_MODEL), row)
    else:
        assert split % tm == 0 and (n - split) % tm == 0
        split_blk = split // tm
        out_shape = (jax.ShapeDtypeStruct((split, D_MODEL), f32),
                     jax.ShapeDtypeStruct((n - split, D_MODEL), f32))
        out_specs = (pl.BlockSpec((tm, D_MODEL), lambda i: (jnp.minimum(i, split_blk - 1), 0)),
                     pl.BlockSpec((tm, D_MODEL), lambda i: (jnp.maximum(i - split_blk, 0), 0)))
    return pl.pallas_call(
        functools.partial(_combine_ln_kernel, split_blk=split_blk),
        out_shape=out_shape,
        grid=(n // tm,),
        in_specs=[pl.BlockSpec((tm, D_MODEL), row),
                  pl.BlockSpec((tm, D_MODEL), row),
                  pl.BlockSpec((tm, D_MODEL), row),
                  pl.BlockSpec((tm, SUBLANES), row),
                  pl.BlockSpec((1, D_MODEL), const),
                  pl.BlockSpec((1, D_MODEL), const)],
        out_specs=out_specs,
        compiler_params=_params(("arbitrary",)),
        name="moe_combine_postnorm",
        interpret=interpret,
    )(x, y0, y1, w_col, g, b)


def _rows(x, idx):
    return x.at[idx].get(mode="promise_in_bounds")


def _moe(x, router_wt, router_b, wg, wu, wd, g, b, split=None, interpret=False):
    n = x.shape[0]
    e_idx, e_w = _router(x, router_wt, router_b, interpret)
    flat_e = e_idx[:MOE_TOP_K].reshape(-1)
    n_assign = n * MOE_TOP_K
    onehot = (flat_e[:, None] == jnp.arange(N_EXPERTS, dtype=jnp.int32)[None, :]).astype(jnp.int32)
    csum = jnp.cumsum(onehot, axis=0)
    rank = jnp.take_along_axis(csum, flat_e[:, None], axis=1)[:, 0] - 1
    counts = csum[-1]
    tm = EXPERT_TILE
    padded = (counts + tm - 1) // tm * tm
    pad_end = jnp.cumsum(padded)
    pad_start = pad_end - padded
    dest = pad_start[flat_e] + rank
    n_blk = (n_assign + N_EXPERTS * (tm - 1)) // tm + 1
    tok = jnp.tile(jnp.arange(n, dtype=jnp.int32), MOE_TOP_K)
    row_tok = jnp.zeros((n_blk * tm,), jnp.int32).at[dest].set(tok)
    blk_e = jnp.minimum(jnp.searchsorted(pad_end, jnp.arange(n_blk, dtype=jnp.int32) * tm, side='right'),
                        N_EXPERTS - 1).astype(jnp.int32)
    n_used = (pad_end[-1:] // tm).astype(jnp.int32)
    xs = _rows(x, row_tok)
    ys = _experts(blk_e, n_used, xs, wg, wu, wd, interpret)
    y0 = _rows(ys, dest[:n])
    y1 = _rows(ys, dest[n:])
    return _combine_ln(x, y0, y1, e_w.T, g, b, split, interpret)


def _rope_slab(t, cos, sin_signed, lo_half):
    swapped = jnp.where(lo_half, pltpu.roll(t, LANES - HEAD_DIM // 2, 1),
                        pltpu.roll(t, HEAD_DIM // 2, 1))
    return t * cos + swapped * sin_signed


def _qkv_kernel(x_ref, wq_ref, wk_ref, wv_ref, cos_ref, sin_ref, q_ref, k_ref, v_ref):
    xb = x_ref[...].astype(bf16)
    cos = cos_ref[...]
    sin = sin_ref[...]
    lane = lax.broadcasted_iota(jnp.int32, (1, LANES), 1)
    lo_half = (lane % HEAD_DIM) < (HEAD_DIM // 2)
    for w_ref, o_ref in ((wq_ref, q_ref), (wk_ref, k_ref)):
        for c in range(0, D_MODEL, COL_CHUNK):
            t = jnp.dot(xb, w_ref[:, c:c + COL_CHUNK], preferred_element_type=f32)
            for s in range(0, COL_CHUNK, LANES):
                o_ref[:, c + s:c + s + LANES] = _rope_slab(t[:, s:s + LANES], cos, sin, lo_half)
    for c in range(0, D_MODEL, COL_CHUNK):
        v_ref[:, c:c + COL_CHUNK] = jnp.dot(xb, wv_ref[:, c:c + COL_CHUNK],
                                            preferred_element_type=f32)


def _qkv(x, wq, wk, wv, cos, sin, interpret=False):
    n = x.shape[0]
    tm = TOKEN_TILE
    row = lambda i: (i, 0)
    const = lambda i: (0, 0)
    out = jax.ShapeDtypeStruct((n, D_MODEL), f32)
    return pl.pallas_call(
        _qkv_kernel,
        out_shape=(out, out, out),
        grid=(n // tm,),
        in_specs=[pl.BlockSpec((tm, D_MODEL), row),
                  pl.BlockSpec((D_MODEL, D_MODEL), const),
                  pl.BlockSpec((D_MODEL, D_MODEL), const),
                  pl.BlockSpec((D_MODEL, D_MODEL), const),
                  pl.BlockSpec((tm, LANES), row),
                  pl.BlockSpec((tm, LANES), row)],
        out_specs=(pl.BlockSpec((tm, D_MODEL), row),) * 3,
        compiler_params=_params(("parallel",)),
        name="qkv_rope",
        interpret=interpret,
    )(x, wq, wk, wv, cos, sin)


def _rope_tables(pos):
    half = HEAD_DIM // 2
    inv = ROPE_THETA ** (-jnp.arange(half, dtype=f32) / half)
    ang = pos.astype(f32)[:, None] * inv
    cos, sin = jnp.cos(ang), jnp.sin(ang)
    cos_t = jnp.tile(cos, (1, LANES // half))
    sin_t = jnp.tile(jnp.concatenate([-sin, sin], axis=1), (1, LANES // HEAD_DIM))
    return cos_t, sin_t


def _moba_prompt_kernel(q_ref, k_ref, v_ref, o_ref, kaug, vaug, kmrows, *, nb):
    BLK = MOBA_BLOCK
    i = pl.program_id(2)
    lane = lax.broadcasted_iota(jnp.int32, (1, LANES), 1)
    lo = lane < HEAD_DIM
    nt_dims = (((1,), (1,)), ((), ()))

    @pl.when(i == 0)
    def _prep():
        kmrows[...] = jnp.zeros(kmrows.shape, f32)
        for j in range(nb):
            rs = slice(j * BLK, (j + 1) * BLK)
            kj = k_ref[rs, :]
            vj = v_ref[rs, :]
            ind0 = jnp.where(lane == HEAD_DIM + j, 1.0, 0.0)
            ind1 = jnp.where(lane == j, 1.0, 0.0)
            kaug[0, rs, :] = jnp.where(lo, kj, ind0).astype(bf16)
            kaug[1, rs, :] = jnp.where(lo, ind1, kj).astype(bf16)
            vaug[0, rs, :] = jnp.where(lo, vj, 1.0).astype(bf16)
            vaug[1, rs, :] = jnp.where(lo, 1.0, vj).astype(bf16)
            kmean = jnp.mean(kj, axis=0, keepdims=True)
            kmrows[HEAD_DIM + j:HEAD_DIM + j + 1, :] = jnp.where(lo, kmean, 0.0)
            kmrows[j:j + 1, :] = jnp.where(lo, 0.0, kmean)

    row = lax.broadcasted_iota(jnp.int32, (BLK, BLK), 0)
    col = lax.broadcasted_iota(jnp.int32, (BLK, BLK), 1)
    diag_mask = col <= row

    def attend(iv):
        qp = q_ref[...]
        qs = qp * (HEAD_DIM ** -0.5)
        q_own = [jnp.where(lo, qs, 0.0), jnp.where(lo, 0.0, qs)]
        if iv > MOBA_TOPK:
            gate = lax.dot_general(qp, kmrows[...], nt_dims,
                                   preferred_element_type=f32, precision=HIGHEST)
            blkid = lane & (HEAD_DIM - 1)
            past = blkid < iv
            gate = jnp.where(past, gate, -jnp.inf)
            cnt = jnp.zeros((BLK, LANES), f32)
            for j in range(iv):
                c0 = jnp.broadcast_to(gate[:, HEAD_DIM + j:HEAD_DIM + j + 1], (BLK, LANES))
                c1 = jnp.broadcast_to(gate[:, j:j + 1], (BLK, LANES))
                cj = jnp.where(lo, c1, c0)
                ahead = (cj > gate) | ((cj == gate) & (blkid > j))
                cnt = cnt + jnp.where(ahead, 1.0, 0.0)
            bias = jnp.where((cnt < MOBA_TOPK) & past, 0.0, MASK_BIAS)
            q_aug = [jnp.where(lo, qs, bias), jnp.where(lo, bias, qs)]
        else:
            q_aug = q_own
        outs = []
        own = slice(iv * BLK, (iv + 1) * BLK)
        for hh in range(2):
            s_own = lax.dot_general(q_own[hh].astype(bf16), kaug[hh, own, :], nt_dims,
                                    preferred_element_type=f32)
            s_own = jnp.where(diag_mask, s_own, -jnp.inf)
            m = jnp.max(s_own, axis=-1, keepdims=True)
            if iv > 0:
                s_past = lax.dot_general(q_aug[hh].astype(bf16), kaug[hh, 0:iv * BLK, :], nt_dims,
                                         preferred_element_type=f32)
                m = jnp.maximum(m, jnp.max(s_past, axis=-1, keepdims=True))
                acc = jnp.dot(jnp.exp(s_past - m).astype(bf16), vaug[hh, 0:iv * BLK, :],
                              preferred_element_type=f32)
            p_own = jnp.exp(s_own - m).astype(bf16)
            acc_own = jnp.dot(p_own, vaug[hh, own, :], preferred_element_type=f32)
            acc = acc + acc_own if iv > 0 else acc_own
            outs.append(acc / pltpu.roll(acc, HEAD_DIM, 1))
        o_ref[...] = jnp.where(lo, outs[0], outs[1])

    for iv in range(nb):
        pl.when(i == iv)(functools.partial(attend, iv))


def _moba_prompt(q, k, v, n_seq, seq_len, interpret=False):
    BLK = MOBA_BLOCK
    nb = seq_len // BLK
    assert nb <= HEAD_DIM
    return pl.pallas_call(
        functools.partial(_moba_prompt_kernel, nb=nb),
        out_shape=jax.ShapeDtypeStruct((n_seq * seq_len, D_MODEL), f32),
        grid=(n_seq, HEAD_PAIRS, nb),
        in_specs=[pl.BlockSpec((BLK, LANES), lambda b, hp, i: (b * nb + i, hp)),
                  pl.BlockSpec((seq_len, LANES), lambda b, hp, i: (b, hp)),
                  pl.BlockSpec((seq_len, LANES), lambda b, hp, i: (b, hp))],
        out_specs=pl.BlockSpec((BLK, LANES), lambda b, hp, i: (b * nb + i, hp)),
        scratch_shapes=[pltpu.VMEM((2, seq_len, LANES), bf16),
                        pltpu.VMEM((2, seq_len, LANES), bf16),
                        pltpu.VMEM((LANES, LANES), f32)],
        compiler_params=_params(("parallel", "parallel", "arbitrary")),
        name="moba_prompt_attn",
        interpret=interpret,
    )(q, k, v)


def _page_sum_kernel(pt_ref, *refs):
    pages, o_ref = refs[:PAGES_PER_STEP], refs[PAGES_PER_STEP]
    ppb = MOBA_BLOCK // PAGE_SIZE
    bps = PAGES_PER_STEP // ppb
    s = pl.program_id(1)
    lane = lax.broadcasted_iota(jnp.int32, (1, LANES), 1)

    @pl.when(s == 0)
    def _():
        o_ref[...] = jnp.zeros(o_ref.shape, f32)

    for h in range(N_HEADS):
        acc = o_ref[0, h]
        for blk in range(bps):
            t = pages[blk * ppb][0, h]
            for r in range(1, ppb):
                t = t + pages[blk * ppb + r][0, h]
            col = jnp.sum(t, axis=-1, keepdims=True)
            acc = jnp.where(lane == s * bps + blk, col, acc)
        o_ref[0, h] = acc


def _page_sums(page_table, cache_kt, n_blocks, interpret=False):
    bsz, n_pages = page_table.shape
    ppb = MOBA_BLOCK // PAGE_SIZE
    assert n_blocks <= LANES and (n_blocks * ppb) % PAGES_PER_STEP == 0
    steps = n_blocks * ppb // PAGES_PER_STEP
    pt_flat = page_table.reshape(-1)

    def page_map(r):
        return lambda b, s, pt: (pt[b * n_pages + s * PAGES_PER_STEP + r], 0, 0, 0)

    return pl.pallas_call(
        _page_sum_kernel,
        out_shape=jax.ShapeDtypeStruct((bsz, N_HEADS, HEAD_DIM, LANES), f32),
        grid_spec=pltpu.PrefetchScalarGridSpec(
            num_scalar_prefetch=1,
            grid=(bsz, steps),
            in_specs=[pl.BlockSpec((1, N_HEADS, HEAD_DIM, PAGE_SIZE), page_map(r))
                      for r in range(PAGES_PER_STEP)],
            out_specs=pl.BlockSpec((1, N_HEADS, HEAD_DIM, LANES), lambda b, s, pt: (b, 0, 0, 0))),
        compiler_params=_params(("parallel", "arbitrary")),
        name="moba_page_sums",
        interpret=interpret,
    )(pt_flat, *([cache_kt] * PAGES_PER_STEP))


def _sample_topk_kernel(q_ref, ks_ref, idx_ref, *, n_blocks, topk):
    lane = lax.broadcasted_iota(jnp.int32, (1, LANES), 1)
    lane_f = lane.astype(f32)
    for h in range(N_HEADS):
        g = jnp.dot(q_ref[0, h], ks_ref[0, h], preferred_element_type=f32,
                    precision=HIGHEST) * (1.0 / MOBA_BLOCK)
        g = jnp.where(lane < n_blocks, g, -jnp.inf)
        out = jnp.zeros(g.shape, f32)
        for t in range(topk):
            m = jnp.max(g, axis=-1, keepdims=True)
            idx = jnp.min(jnp.where(g == m, lane_f, float(LANES)), axis=-1, keepdims=True)
            out = jnp.where(lane == t, idx, out)
            g = jnp.where(lane_f == idx, -jnp.inf, g)
        idx_ref[0, h] = out.astype(jnp.int32)


def _sample_topk(q4, ksum_t, n_blocks, topk, interpret=False):
    bsz, _, nq, _ = q4.shape
    return pl.pallas_call(
        functools.partial(_sample_topk_kernel, n_blocks=n_blocks, topk=topk),
        out_shape=jax.ShapeDtypeStruct((bsz, N_HEADS, nq, LANES), jnp.int32),
        grid=(bsz,),
        in_specs=[pl.BlockSpec((1, N_HEADS, nq, HEAD_DIM), lambda b: (b, 0, 0, 0)),
                  pl.BlockSpec((1, N_HEADS, HEAD_DIM, LANES), lambda b: (b, 0, 0, 0))],
        out_specs=pl.BlockSpec((1, N_HEADS, nq, LANES), lambda b: (b, 0, 0, 0)),
        compiler_params=_params(("parallel",)),
        name="moba_sample_topk",
        interpret=interpret,
    )(q4, ksum_t)


def _sample_attn_kernel(phys_ref, q_ref, kn_ref, vn_ref, ck_hbm, cv_hbm, o_ref,
                        kbuf, vbuf, sem, *, nq, n_sel):
    g = pl.program_id(0)
    ng = pl.num_programs(0)
    slot = g % 2
    n_pg = nq * n_sel

    def page_copies(gi, sl, r):
        page = phys_ref[gi * n_pg + r]
        head = gi % N_HEADS
        return (pltpu.make_async_copy(ck_hbm.at[page, head], kbuf.at[sl, r], sem.at[0, sl]),
                pltpu.make_async_copy(cv_hbm.at[page, head], vbuf.at[sl, r], sem.at[1, sl]))

    def fetch(gi, sl):
        for r in range(n_pg):
            ck, cv = page_copies(gi, sl, r)
            ck.start()
            cv.start()

    @pl.when(g == 0)
    def _():
        fetch(0, 0)

    @pl.when(g + 1 < ng)
    def _():
        fetch(g + 1, 1 - slot)

    for r in range(n_pg):
        ck, cv = page_copies(g, slot, r)
        ck.wait()
        cv.wait()

    lane = lax.broadcasted_iota(jnp.int32, (1, LANES), 1)
    q_t = q_ref[0, 0] * (HEAD_DIM ** -0.5)
    kn = kn_ref[0, 0]
    vn = vn_ref[0, 0]
    out = jnp.zeros((HEAD_DIM, LANES), f32)
    for qi in range(nq):
        qb = jnp.broadcast_to(q_t[:, qi:qi + 1], (HEAD_DIM, LANES))
        s_own = jnp.sum(kn * qb, axis=0, keepdims=True)
        s_own = jnp.where(lane <= qi, s_own, -jnp.inf)
        s_sel = [jnp.sum(kbuf[slot, qi * n_sel + r] * qb, axis=0, keepdims=True)
                 for r in range(n_sel)]
        m_lane = s_own
        for s in s_sel:
            m_lane = jnp.maximum(m_lane, s)
        m = jnp.max(m_lane, axis=-1, keepdims=True)
        p_own = jnp.exp(s_own - m)
        p_sum = p_own
        acc = vn * p_own
        for r, s in enumerate(s_sel):
            p = jnp.exp(s - m)
            p_sum = p_sum + p
            acc = acc + vbuf[slot, qi * n_sel + r] * p
        den = jnp.sum(p_sum, axis=-1, keepdims=True)
        o_col = jnp.sum(acc, axis=-1, keepdims=True) / den
        out = jnp.where(lane == qi, o_col, out)
    o_ref[0, 0] = out


def _sample_attn(phys, q_t, kn_t, vn_t, cache_kt, cache_vt, nq, n_sel, interpret=False):
    bsz = q_t.shape[0]
    small = pl.BlockSpec((1, 1, HEAD_DIM, LANES),
                         lambda g, ph: (g // N_HEADS, g % N_HEADS, 0, 0))
    hbm = pl.BlockSpec(memory_space=pl.ANY)
    return pl.pallas_call(
        functools.partial(_sample_attn_kernel, nq=nq, n_sel=n_sel),
        out_shape=jax.ShapeDtypeStruct((bsz, N_HEADS, HEAD_DIM, LANES), f32),
        grid_spec=pltpu.PrefetchScalarGridSpec(
            num_scalar_prefetch=1,
            grid=(bsz * N_HEADS,),
            in_specs=[small, small, small, hbm, hbm],
            out_specs=small,
            scratch_shapes=[pltpu.VMEM((2, nq * n_sel, HEAD_DIM, PAGE_SIZE), f32),
                            pltpu.VMEM((2, nq * n_sel, HEAD_DIM, PAGE_SIZE), f32),
                            pltpu.SemaphoreType.DMA((2, 2))]),
        compiler_params=_params(("arbitrary",)),
        name="moba_sample_attn",
        interpret=interpret,
    )(phys, q_t, kn_t, vn_t, cache_kt, cache_vt)


def _heads_first(t, bsz, seq_len):
    return t.reshape(bsz, seq_len, N_HEADS, HEAD_DIM).transpose(0, 2, 1, 3)


def _forward(x_prompt, x_sample, state_ssm, state_conv, cache_k, cache_v, page_table,
             ln_gain, ln_bias, ssm_w_in, ssm_w_conv, ssm_b_conv, ssm_dt_bias, ssm_a_log,
             ssm_d, ssm_norm_w, ssm_w_out, attn_w_q, attn_w_kv, attn_w_o,
             router_w, router_b, moe_w_gate, moe_w_up, moe_w_down, interpret=False):
    bp, lp, _ = x_prompt.shape
    bs, ls, _ = x_sample.shape
    n_p, n_s = bp * lp, bs * ls
    n = n_p + n_s
    n_pages = page_table.shape[1]
    past_len = n_pages * PAGE_SIZE
    assert n % TOKEN_TILE == 0 and lp % MOBA_BLOCK == 0 and lp % SSD_CHUNK == 0
    assert CONV_W - 1 <= ls <= SUBLANES and n_p % ls == 0
    assert past_len % MOBA_BLOCK == 0 and past_len // MOBA_BLOCK >= MOBA_TOPK
    assert ls <= MOBA_BLOCK - past_len % MOBA_BLOCK

    h = jnp.concatenate([x_prompt.reshape(n_p, D_MODEL), x_sample.reshape(n_s, D_MODEL)], axis=0)
    router_wt = router_w.T
    router_bc = router_b.reshape(N_EXPERTS, 1)

    def vec(v):
        return v.reshape(1, -1)

    l = 0
    w_in = ssm_w_in[l]
    wz = w_in[:, :D_INNER].astype(bf16)
    wx = w_in[:, D_INNER:D_INNER + CONV_DIM].astype(bf16)
    wdt = jnp.pad(w_in[:, D_INNER + CONV_DIM:], ((0, 0), (0, LANES - SSM_HEADS)))
    z, xbc, dt = _in_proj(h, wz, wx, wdt, interpret)

    pad_h = (0, LANES - SSM_HEADS)
    wconv = jnp.pad(ssm_w_conv[l], ((0, SUBLANES - CONV_W), (0, 0)))
    ssd_w = (wconv, vec(ssm_b_conv[l]), vec(jnp.pad(ssm_dt_bias[l], pad_h)),
             vec(jnp.pad(ssm_a_log[l], pad_h)), vec(jnp.repeat(ssm_d[l], SSM_HEAD_DIM)),
             vec(ssm_norm_w[l]))
    y_p, st_p = _ssd(z, xbc, dt, None, None, *ssd_w, n_seq=bp, seq_len=lp, row_off=0,
                     interpret=interpret)
    conv0 = jnp.pad(state_conv[l], ((0, 0), (SUBLANES - (CONV_W - 1), 0), (0, 0)))
    y_s, st_s = _ssd(z, xbc, dt, conv0, _state_to_pairs(state_ssm[l]), *ssd_w,
                     n_seq=bs, seq_len=ls, row_off=n_p, interpret=interpret)
    ssm_prompt = _pairs_to_state(st_p)[None]
    ssm_sample = _pairs_to_state(st_s)[None]
    tail = jnp.arange(-(CONV_W - 1), 0, dtype=jnp.int32)
    rows_p = ((jnp.arange(bp, dtype=jnp.int32) + 1) * lp)[:, None] + tail
    rows_s = (n_p + (jnp.arange(bs, dtype=jnp.int32) + 1) * ls)[:, None] + tail
    conv_prompt = _rows(xbc, rows_p.reshape(-1)).reshape(1, bp, CONV_W - 1, CONV_DIM)
    conv_sample = _rows(xbc, rows_s.reshape(-1)).reshape(1, bs, CONV_W - 1, CONV_DIM)

    h = _proj_ln(y_p, y_s, ssm_w_out[l].astype(bf16), h, vec(ln_gain[l, 0]), vec(ln_bias[l, 0]),
                 interpret)
    h = _moe(h, router_wt, router_bc, moe_w_gate[l].astype(bf16), moe_w_up[l].astype(bf16),
             moe_w_down[l].astype(bf16), vec(ln_gain[l, 1]), vec(ln_bias[l, 1]),
             interpret=interpret)

    l = 1
    pos = jnp.concatenate([jnp.tile(jnp.arange(lp), bp), jnp.tile(past_len + jnp.arange(ls), bs)])
    cos_t, sin_t = _rope_tables(pos)
    hd_all = N_HEADS * HEAD_DIM
    q, k, v = _qkv(h, attn_w_q[0].astype(bf16), attn_w_kv[:, :hd_all].astype(bf16),
                   attn_w_kv[:, hd_all:].astype(bf16), cos_t, sin_t, interpret)
    k_prompt = _heads_first(k[:n_p], bp, lp)
    v_prompt = _heads_first(v[:n_p], bp, lp)
    k_sample = _heads_first(k[n_p:], bs, ls)
    v_sample = _heads_first(v[n_p:], bs, ls)
    q_sample = _heads_first(q[n_p:], bs, ls)

    o_p = _moba_prompt(q, k, v, bp, lp, interpret)

    n_full = past_len // MOBA_BLOCK
    ppb = MOBA_BLOCK // PAGE_SIZE
    cache_kt = jnp.swapaxes(cache_k, 2, 3)
    cache_vt = jnp.swapaxes(cache_v, 2, 3)
    ksum_t = _page_sums(page_table, cache_kt, n_full, interpret)
    idx = _sample_topk(q_sample, ksum_t, n_full, MOBA_TOPK, interpret)[..., :MOBA_TOPK]
    lpage = idx[..., None] * ppb + jnp.arange(ppb, dtype=jnp.int32)
    phys = page_table[jnp.arange(bs)[:, None, None, None, None], lpage]

    def lanes_last(t):
        return jnp.pad(jnp.swapaxes(t, 2, 3), ((0, 0), (0, 0), (0, 0), (0, LANES - ls)))

    o_st = _sample_attn(phys.reshape(-1).astype(jnp.int32), lanes_last(q_sample),
                        lanes_last(k_sample), lanes_last(v_sample), cache_kt, cache_vt,
                        ls, MOBA_TOPK * ppb, interpret)
    o_s = o_st[..., :ls].transpose(0, 3, 1, 2).reshape(n_s, D_MODEL)

    h = _proj_ln(o_p, o_s, attn_w_o[0].astype(bf16), h, vec(ln_gain[l, 0]), vec(ln_bias[l, 0]),
                 interpret)
    h_p, h_s = _moe(h, router_wt, router_bc, moe_w_gate[l].astype(bf16), moe_w_up[l].astype(bf16),
                    moe_w_down[l].astype(bf16), vec(ln_gain[l, 1]), vec(ln_bias[l, 1]),
                    split=n_p, interpret=interpret)

    y_prompt = h_p.reshape(bp, lp, D_MODEL)
    y_sample = h_s.reshape(bs, ls, D_MODEL)
    return (y_prompt, y_sample, ssm_prompt, conv_prompt, k_prompt, v_prompt,
            ssm_sample, conv_sample, k_sample, v_sample)


def kernel(x_prompt, x_sample, state_ssm, state_conv, cache_k, cache_v, page_table, ln_gain, ln_bias, ssm_w_in, ssm_w_conv, ssm_b_conv, ssm_dt_bias, ssm_a_log, ssm_d, ssm_norm_w, ssm_w_out, attn_w_q, attn_w_kv, attn_w_o, router_w, router_b, moe_w_gate, moe_w_up, moe_w_down):
    return _forward(x_prompt, x_sample, state_ssm, state_conv, cache_k, cache_v, page_table,
                    ln_gain, ln_bias, ssm_w_in, ssm_w_conv, ssm_b_conv, ssm_dt_bias, ssm_a_log,
                    ssm_d, ssm_norm_w, ssm_w_out, attn_w_q, attn_w_kv, attn_w_o,
                    router_w, router_b, moe_w_gate, moe_w_up, moe_w_down)
```

```python
import functools
import math

import jax
import jax.numpy as jnp
from jax import lax
from jax.experimental import pallas as pl
from jax.experimental.pallas import tpu as pltpu

f32 = jnp.float32
bf16 = jnp.bfloat16
HIGHEST = lax.Precision.HIGHEST

D_MODEL = 1024
DEPTH = 2
PAGE_SIZE = 128
N_A_LAYERS = 1
SSM_HEAD_DIM = 64
D_INNER = 2048
SSM_HEADS = D_INNER // SSM_HEAD_DIM
SSM_GROUPS = 4
D_STATE = 128
CONV_W = 4
CONV_DIM = D_INNER + 2 * SSM_GROUPS * D_STATE
SSD_CHUNK = 128
N_HEADS = 16
HEAD_DIM = 64
MOBA_BLOCK = 256
MOBA_TOPK = 3
ROPE_THETA = 10000.0
N_EXPERTS = 16
N_EXPERT_GROUPS = 4
EXPERTS_PER_GROUP = 4
MOE_TOP_K = 2
D_EXPERT = 1024
DEEPNORM_ALPHA = (2.0 * DEPTH) ** 0.25
LN_EPS = 1e-5
RMS_EPS = 1e-6
MASK_BIAS = -1e30

LANES = 128
SUBLANES = 8
VMEM_LIMIT = 48 * 1024 * 1024

TOKEN_TILE = 256
EXPERT_TILE = 512
CAST_ROWS = 256
EXPERT_VMEM_LIMIT = 56 * 1024 * 1024
COL_CHUNK = 512
PAGES_PER_STEP = 8
HEAD_PAIRS = N_HEADS // 2
SSM_PAIRS = SSM_HEADS // 2


def _params(sem):
    return pltpu.CompilerParams(dimension_semantics=sem, vmem_limit_bytes=VMEM_LIMIT)


def _sigmoid(x):
    return 1.0 / (1.0 + jnp.exp(-x))


def _softplus(x):
    u = jnp.exp(-jnp.abs(x))
    w = 1.0 + u
    log1p_u = jnp.where(w == 1.0, u, jnp.log(w) * (u / (w - 1.0)))
    return jnp.maximum(x, 0.0) + log1p_u


def _layer_norm_rows(v, g, b):
    mu = jnp.mean(v, axis=-1, keepdims=True)
    d = v - mu
    var = jnp.mean(d * d, axis=-1, keepdims=True)
    return d * lax.rsqrt(var + LN_EPS) * g + b


def _in_proj_kernel(x_ref, wz_ref, wx_ref, wdt_ref, z_ref, xbc_ref, dt_ref):
    x = x_ref[...]
    xb = x.astype(bf16)
    for c in range(0, D_INNER, COL_CHUNK):
        z_ref[:, c:c + COL_CHUNK] = jnp.dot(xb, wz_ref[:, c:c + COL_CHUNK],
                                            preferred_element_type=f32)
    for c in range(0, CONV_DIM, COL_CHUNK):
        xbc_ref[:, c:c + COL_CHUNK] = jnp.dot(xb, wx_ref[:, c:c + COL_CHUNK],
                                              preferred_element_type=f32)
    dt_ref[...] = jnp.dot(x, wdt_ref[...], preferred_element_type=f32, precision=HIGHEST)


def _in_proj(x, wz, wx, wdt, interpret=False):
    n = x.shape[0]
    tm = TOKEN_TILE
    return pl.pallas_call(
        _in_proj_kernel,
        out_shape=(jax.ShapeDtypeStruct((n, D_INNER), f32),
                   jax.ShapeDtypeStruct((n, CONV_DIM), f32),
                   jax.ShapeDtypeStruct((n, LANES), f32)),
        grid=(n // tm,),
        in_specs=[pl.BlockSpec((tm, D_MODEL), lambda i: (i, 0)),
                  pl.BlockSpec((D_MODEL, D_INNER), lambda i: (0, 0)),
                  pl.BlockSpec((D_MODEL, CONV_DIM), lambda i: (0, 0)),
                  pl.BlockSpec((D_MODEL, LANES), lambda i: (0, 0))],
        out_specs=(pl.BlockSpec((tm, D_INNER), lambda i: (i, 0)),
                   pl.BlockSpec((tm, CONV_DIM), lambda i: (i, 0)),
                   pl.BlockSpec((tm, LANES), lambda i: (i, 0))),
        compiler_params=_params(("parallel",)),
        name="ssm_in_proj",
        interpret=interpret,
    )(x, wz, wx, wdt)


def _ssd_kernel(*refs, c_in, nc, has_init):
    T = SSD_CHUNK
    if has_init:
        (z_ref, xbc_ref, dt_ref, conv0_ref, st0_ref, wconv_ref, bconv_ref, dtb_ref, alog_ref,
         d_ref, nw_ref, y_ref, st_out_ref, xbuf, xc, ybuf, st) = refs
    else:
        (z_ref, xbc_ref, dt_ref, wconv_ref, bconv_ref, dtb_ref, alog_ref,
         d_ref, nw_ref, y_ref, st_out_ref, xbuf, xc, ybuf, st) = refs
    c = pl.program_id(1)

    @pl.when(c == 0)
    def _init():
        if has_init:
            xbuf[0:SUBLANES, :] = conv0_ref[0]
            st[...] = st0_ref[0]
        else:
            xbuf[0:SUBLANES, :] = jnp.zeros((SUBLANES, CONV_DIM), f32)
            st[...] = jnp.zeros(st.shape, f32)

    xbuf[SUBLANES:SUBLANES + c_in, :] = xbc_ref[...]
    if c_in < T:
        xbuf[SUBLANES + c_in:SUBLANES + T, :] = jnp.zeros((T - c_in, CONV_DIM), f32)

    for c0 in range(0, CONV_DIM, COL_CHUNK):
        cs = slice(c0, c0 + COL_CHUNK)
        xin = xbuf[0:T + SUBLANES, cs]
        acc = bconv_ref[:, cs]
        for k in range(CONV_W):
            shift = CONV_W - 1 - k
            tap = xin if shift == 0 else pltpu.roll(xin, shift, 0)
            acc = acc + tap[SUBLANES:, :] * wconv_ref[k:k + 1, cs]
        xc[:, cs] = acc * _sigmoid(acc)
    if nc > 1:
        xbuf[0:SUBLANES, :] = xbuf[T:T + SUBLANES, :]

    def pad_rows(v):
        if c_in == T:
            return v
        return jnp.concatenate([v, jnp.zeros((T - c_in, v.shape[1]), v.dtype)], axis=0)

    dtv = pad_rows(_softplus(dt_ref[...] + dtb_ref[...]))
    a = -jnp.exp(alog_ref[...])
    da = dtv * a
    row = lax.broadcasted_iota(jnp.int32, (T, T), 0)
    col = lax.broadcasted_iota(jnp.int32, (T, T), 1)
    causal = col <= row
    tri = causal.astype(f32)
    acs = jnp.dot(tri, da, preferred_element_type=f32, precision=HIGHEST)
    acs_t = acs.T
    dt_t = dtv.T
    w_all = dt_t * jnp.exp(acs_t[:, T - 1:T] - acs_t)
    lo = lax.broadcasted_iota(jnp.int32, (1, LANES), 1) < SSM_HEAD_DIM

    for g in range(SSM_GROUPS):
        bm = xc[:, D_INNER + g * D_STATE:D_INNER + (g + 1) * D_STATE]
        cm = xc[:, D_INNER + (SSM_GROUPS + g) * D_STATE:D_INNER + (SSM_GROUPS + g + 1) * D_STATE]
        cb = lax.dot_general(cm.astype(bf16), bm.astype(bf16), (((1,), (1,)), ((), ())),
                             preferred_element_type=f32)
        bm_t = bm.T
        for e in range(SSM_PAIRS // SSM_GROUPS):
            pr = g * (SSM_PAIRS // SSM_GROUPS) + e
            ls = slice(pr * LANES, (pr + 1) * LANES)
            x_pair = xc[:, ls]
            xb = x_pair.astype(bf16)
            rhs = jnp.concatenate([xb, st[pr].astype(bf16)], axis=0)
            r, u, ea_last = [], [], []
            for hh in range(2):
                h = 2 * pr + hh
                a_b = jnp.broadcast_to(acs[:, h:h + 1], (T, LANES))
                seg = a_b - acs_t[h:h + 1, :]
                dec = jnp.exp(jnp.where(causal, seg, -jnp.inf))
                m_h = cb * dec * dt_t[h:h + 1, :]
                e_a = jnp.exp(a_b)
                lhs = jnp.concatenate([m_h.astype(bf16), (cm * e_a).astype(bf16)], axis=1)
                r.append(jnp.dot(lhs, rhs, preferred_element_type=f32))
                lhs_s = (bm_t * w_all[h:h + 1, :]).astype(bf16)
                u.append(jnp.dot(lhs_s, xb, preferred_element_type=f32))
                ea_last.append(e_a[T - 1:T, :])
            st[pr] = st[pr] * jnp.where(lo, ea_last[0], ea_last[1]) + jnp.where(lo, u[0], u[1])
            yv = jnp.where(lo, r[0], r[1]) + d_ref[:, ls] * x_pair
            zz = pad_rows(z_ref[:, ls])
            ybuf[:, ls] = yv * (zz * _sigmoid(zz))
        gw = D_INNER // SSM_GROUPS
        gs = slice(g * gw, (g + 1) * gw)
        yg = ybuf[:, gs]
        ms = jnp.mean(yg * yg, axis=-1, keepdims=True)
        y_ref[:, gs] = (yg * lax.rsqrt(ms + RMS_EPS) * nw_ref[:, gs])[0:c_in]

    @pl.when(c == nc - 1)
    def _fin():
        st_out_ref[0] = st[...]


def _ssd(z, xbc, dt, conv0, st0, wconv, bconv, dtb, alog, d_exp, nw, *,
         n_seq, seq_len, row_off, interpret=False):
    T = SSD_CHUNK
    has_init = conv0 is not None
    if seq_len >= T:
        c_in, nc = T, seq_len // T
    else:
        c_in, nc = seq_len, 1
    off = row_off // c_in

    def rows(bi, ci):
        return (off + bi * nc + ci, 0)

    const = lambda bi, ci: (0, 0)
    in_specs = [pl.BlockSpec((c_in, D_INNER), rows),
                pl.BlockSpec((c_in, CONV_DIM), rows),
                pl.BlockSpec((c_in, LANES), rows)]
    args = [z, xbc, dt]
    if has_init:
        in_specs += [pl.BlockSpec((1, SUBLANES, CONV_DIM), lambda bi, ci: (bi, 0, 0)),
                     pl.BlockSpec((1, SSM_PAIRS, D_STATE, LANES), lambda bi, ci: (bi, 0, 0, 0))]
        args += [conv0, st0]
    in_specs += [pl.BlockSpec((SUBLANES, CONV_DIM), const),
                 pl.BlockSpec((1, CONV_DIM), const),
                 pl.BlockSpec((1, LANES), const),
                 pl.BlockSpec((1, LANES), const),
                 pl.BlockSpec((1, D_INNER), const),
                 pl.BlockSpec((1, D_INNER), const)]
    args += [wconv, bconv, dtb, alog, d_exp, nw]
    return pl.pallas_call(
        functools.partial(_ssd_kernel, c_in=c_in, nc=nc, has_init=has_init),
        out_shape=(jax.ShapeDtypeStruct((n_seq * seq_len, D_INNER), f32),
                   jax.ShapeDtypeStruct((n_seq, SSM_PAIRS, D_STATE, LANES), f32)),
        grid=(n_seq, nc),
        in_specs=in_specs,
        out_specs=(pl.BlockSpec((c_in, D_INNER), lambda bi, ci: (bi * nc + ci, 0)),
                   pl.BlockSpec((1, SSM_PAIRS, D_STATE, LANES), lambda bi, ci: (bi, 0, 0, 0))),
        scratch_shapes=[pltpu.VMEM((T + 2 * SUBLANES, CONV_DIM), f32),
                        pltpu.VMEM((T, CONV_DIM), f32),
                        pltpu.VMEM((T, D_INNER), f32),
                        pltpu.VMEM((SSM_PAIRS, D_STATE, LANES), f32)],
        compiler_params=_params(("parallel", "arbitrary")),
        name="ssd_scan_init" if has_init else "ssd_scan",
        interpret=interpret,
    )(*args)


def _state_to_pairs(s):
    b = s.shape[0]
    s = s.reshape(b, SSM_PAIRS, 2, SSM_HEAD_DIM, D_STATE)
    return s.transpose(0, 1, 4, 2, 3).reshape(b, SSM_PAIRS, D_STATE, 2 * SSM_HEAD_DIM)


def _pairs_to_state(s):
    b = s.shape[0]
    s = s.reshape(b, SSM_PAIRS, D_STATE, 2, SSM_HEAD_DIM)
    return s.transpose(0, 1, 3, 4, 2).reshape(b, SSM_HEADS, SSM_HEAD_DIM, D_STATE)


def _proj_ln_kernel(yp_ref, ys_ref, w_ref, x_ref, g_ref, b_ref, o_ref, *, split_blk, slabs):
    if slabs:
        y_p = jnp.concatenate([yp_ref[s] for s in range(yp_ref.shape[0])], axis=1)
    else:
        y_p = yp_ref[...]
    y = jnp.where(pl.program_id(0) < split_blk, y_p, ys_ref[...])
    yb = y.astype(bf16)
    for c in range(0, D_MODEL, COL_CHUNK):
        cs = slice(c, c + COL_CHUNK)
        o_ref[:, cs] = DEEPNORM_ALPHA * x_ref[:, cs] + jnp.dot(yb, w_ref[:, cs],
                                                               preferred_element_type=f32)
    o_ref[...] = _layer_norm_rows(o_ref[...], g_ref[...], b_ref[...])


def _proj_ln(y_p, y_s, w, x, g, b, interpret=False):
    n = x.shape[0]
    k = y_s.shape[1]
    tm = TOKEN_TILE
    slabs = y_p.ndim == 3
    rows_p = y_p.shape[1] if slabs else y_p.shape[0]
    assert rows_p % tm == 0 and y_s.shape[0] % tm == 0
    split_blk = rows_p // tm
    if slabs:
        p_spec = pl.BlockSpec((k // LANES, tm, LANES),
                              lambda i: (0, jnp.minimum(i, split_blk - 1), 0))
    else:
        p_spec = pl.BlockSpec((tm, k), lambda i: (jnp.minimum(i, split_blk - 1), 0))
    return pl.pallas_call(
        functools.partial(_proj_ln_kernel, split_blk=split_blk, slabs=slabs),
        out_shape=jax.ShapeDtypeStruct((n, D_MODEL), f32),
        grid=(n // tm,),
        in_specs=[p_spec,
                  pl.BlockSpec((tm, k), lambda i: (jnp.maximum(i - split_blk, 0), 0)),
                  pl.BlockSpec((k, D_MODEL), lambda i: (0, 0)),
                  pl.BlockSpec((tm, D_MODEL), lambda i: (i, 0)),
                  pl.BlockSpec((1, D_MODEL), lambda i: (0, 0)),
                  pl.BlockSpec((1, D_MODEL), lambda i: (0, 0))],
        out_specs=pl.BlockSpec((tm, D_MODEL), lambda i: (i, 0)),
        compiler_params=_params(("parallel",)),
        name="proj_postnorm",
        interpret=interpret,
    )(y_p, y_s, w, x, g, b)


def _router_kernel(x_ref, w_in_ref, b_ref, e_ref, w_ref, cnt_scr):
    @pl.when(pl.program_id(0) == 0)
    def _():
        cnt_scr[...] = jnp.zeros(cnt_scr.shape, f32)

    logits_tok = jnp.dot(x_ref[...], w_in_ref[...], preferred_element_type=f32,
                         precision=HIGHEST)
    logits = logits_tok.T[0:N_EXPERTS, :]
    s = _sigmoid(logits)
    sb = s + b_ref[...]
    srow = [s[i:i + 1, :] for i in range(N_EXPERTS)]
    brow = [sb[i:i + 1, :] for i in range(N_EXPERTS)]
    gscore = []
    for g in range(N_EXPERT_GROUPS):
        v = brow[g * EXPERTS_PER_GROUP:(g + 1) * EXPERTS_PER_GROUP]
        best = None
        for i in range(EXPERTS_PER_GROUP):
            for j in range(i + 1, EXPERTS_PER_GROUP):
                hi = jnp.maximum(v[i], v[j])
                lo_ = jnp.minimum(v[i], v[j])
                p = hi + lo_
                best = p if best is None else jnp.maximum(best, p)
        gscore.append(best)
    gi = jnp.zeros_like(gscore[0], dtype=jnp.int32)
    gbest = gscore[0]
    for g in range(1, N_EXPERT_GROUPS):
        upd = gscore[g] > gbest
        gi = jnp.where(upd, g, gi)
        gbest = jnp.where(upd, gscore[g], gbest)
    vb, vs = [], []
    for k in range(EXPERTS_PER_GROUP):
        tb, ts = brow[k], srow[k]
        for g in range(1, N_EXPERT_GROUPS):
            tb = jnp.where(gi == g, brow[g * EXPERTS_PER_GROUP + k], tb)
            ts = jnp.where(gi == g, srow[g * EXPERTS_PER_GROUP + k], ts)
        vb.append(tb)
        vs.append(ts)
    i1 = jnp.zeros_like(gi)
    b1, s1 = vb[0], vs[0]
    for k in range(1, EXPERTS_PER_GROUP):
        upd = vb[k] > b1
        i1 = jnp.where(upd, k, i1)
        b1 = jnp.where(upd, vb[k], b1)
        s1 = jnp.where(upd, vs[k], s1)
    i2 = jnp.full_like(gi, -1)
    b2 = jnp.full_like(b1, -jnp.inf)
    s2 = jnp.zeros_like(s1)
    for k in range(EXPERTS_PER_GROUP):
        upd = (i1 != k) & ((vb[k] > b2) | (i2 < 0))
        i2 = jnp.where(upd, k, i2)
        b2 = jnp.where(upd, vb[k], b2)
        s2 = jnp.where(upd, vs[k], s2)
    den = s1 + s2
    tm = gi.shape[1]
    e0 = gi * EXPERTS_PER_GROUP + i1
    e1 = gi * EXPERTS_PER_GROUP + i2
    eid = lax.broadcasted_iota(jnp.int32, (N_EXPERTS, tm), 0)
    hit = jnp.where((eid == e0) | (eid == e1), 1.0, 0.0)
    t_row = lax.broadcasted_iota(jnp.int32, (tm, tm), 0)
    t_col = lax.broadcasted_iota(jnp.int32, (tm, tm), 1)
    before = jnp.where(t_row < t_col, 1.0, 0.0).astype(bf16)
    rank_all = jnp.dot(hit.astype(bf16), before, preferred_element_type=f32) + cnt_scr[...]
    r0 = jnp.zeros_like(s1)
    r1 = jnp.zeros_like(s1)
    for k in range(N_EXPERTS):
        rk = rank_all[k:k + 1, :]
        r0 = jnp.where(e0 == k, rk, r0)
        r1 = jnp.where(e1 == k, rk, r1)
    cnt_scr[...] = cnt_scr[...] + jnp.sum(hit, axis=-1, keepdims=True)
    zi = jnp.zeros((SUBLANES - 4, tm), jnp.int32)
    zf = jnp.zeros((SUBLANES - 2, tm), f32)
    e_ref[...] = jnp.concatenate([e0, e1, r0.astype(jnp.int32), r1.astype(jnp.int32), zi], axis=0)
    w_ref[...] = jnp.concatenate([s1 / den, s2 / den, zf], axis=0)


def _router(x, w_pad, b, interpret=False):
    n = x.shape[0]
    tm = TOKEN_TILE
    return pl.pallas_call(
        _router_kernel,
        out_shape=(jax.ShapeDtypeStruct((SUBLANES, n), jnp.int32),
                   jax.ShapeDtypeStruct((SUBLANES, n), f32)),
        grid=(n // tm,),
        in_specs=[pl.BlockSpec((tm, D_MODEL), lambda i: (i, 0)),
                  pl.BlockSpec((D_MODEL, LANES), lambda i: (0, 0)),
                  pl.BlockSpec((N_EXPERTS, 1), lambda i: (0, 0))],
        out_specs=(pl.BlockSpec((SUBLANES, tm), lambda i: (0, i)),
                   pl.BlockSpec((SUBLANES, tm), lambda i: (0, i))),
        scratch_shapes=[pltpu.VMEM((N_EXPERTS, 1), f32)],
        compiler_params=_params(("arbitrary",)),
        name="moe_router",
        interpret=interpret,
    )(x, w_pad, b)


def _expert_kernel(blk_e_ref, n_used_ref, x_ref, wg_ref, wu_ref, wd_ref, o_ref,
                   wg_b, wu_b, wd_b, h_scr):
    i = pl.program_id(0)
    e = blk_e_ref[i]
    e_prev = blk_e_ref[jnp.maximum(i - 1, 0)]

    @pl.when((i == 0) | (e != e_prev))
    def _cast():
        for src, dst in ((wg_ref, wg_b), (wu_ref, wu_b), (wd_ref, wd_b)):
            for r0 in range(0, src.shape[1], CAST_ROWS):
                dst[r0:r0 + CAST_ROWS, :] = src[0, r0:r0 + CAST_ROWS, :].astype(bf16)

    @pl.when(i < n_used_ref[0])
    def _():
        xb = x_ref[...].astype(bf16)
        for c in range(0, D_EXPERT, COL_CHUNK):
            cs = slice(c, c + COL_CHUNK)
            hg = jnp.dot(xb, wg_b[:, cs], preferred_element_type=f32)
            hu = jnp.dot(xb, wu_b[:, cs], preferred_element_type=f32)
            h_scr[:, cs] = (hg * _sigmoid(hg) * hu).astype(bf16)
        hb = h_scr[...]
        for c in range(0, D_MODEL, COL_CHUNK):
            cs = slice(c, c + COL_CHUNK)
            o_ref[:, cs] = jnp.dot(hb, wd_b[:, cs], preferred_element_type=f32)

    @pl.when(i >= n_used_ref[0])
    def _():
        o_ref[...] = jnp.zeros(o_ref.shape, f32)


def _experts(blk_e, n_used, xs, wg, wu, wd, interpret=False):
    r = xs.shape[0]
    tm = EXPERT_TILE
    wspec = lambda i, be, nu: (be[i], 0, 0)
    return pl.pallas_call(
        _expert_kernel,
        out_shape=jax.ShapeDtypeStruct((r, D_MODEL), f32),
        grid_spec=pltpu.PrefetchScalarGridSpec(
            num_scalar_prefetch=2,
            grid=(r // tm,),
            in_specs=[pl.BlockSpec((tm, D_MODEL), lambda i, be, nu: (i, 0)),
                      pl.BlockSpec((1, D_MODEL, D_EXPERT), wspec),
                      pl.BlockSpec((1, D_MODEL, D_EXPERT), wspec),
                      pl.BlockSpec((1, D_EXPERT, D_MODEL), wspec)],
            out_specs=pl.BlockSpec((tm, D_MODEL), lambda i, be, nu: (i, 0)),
            scratch_shapes=[pltpu.VMEM((D_MODEL, D_EXPERT), bf16),
                            pltpu.VMEM((D_MODEL, D_EXPERT), bf16),
                            pltpu.VMEM((D_EXPERT, D_MODEL), bf16),
                            pltpu.VMEM((tm, D_EXPERT), bf16)]),
        compiler_params=pltpu.CompilerParams(dimension_semantics=("arbitrary",),
                                             vmem_limit_bytes=EXPERT_VMEM_LIMIT),
        name="moe_experts",
        interpret=interpret,
    )(blk_e, n_used, xs, wg, wu, wd)


def _combine_ln_kernel(x_ref, y0_ref, y1_ref, w_ref, g_ref, b_ref, *o_refs, split_blk):
    w = w_ref[...]
    v = DEEPNORM_ALPHA * x_ref[...] + (y0_ref[...] * w[:, 0:1] + y1_ref[...] * w[:, 1:2])
    res = _layer_norm_rows(v, g_ref[...], b_ref[...])
    if split_blk is None:
        o_refs[0][...] = res
    else:
        i = pl.program_id(0)

        @pl.when(i < split_blk)
        def _():
            o_refs[0][...] = res

        @pl.when(i >= split_blk)
        def _():
            o_refs[1][...] = res


def _combine_ln(x, y0, y1, w_col, g, b, split=None, interpret=False):
    n = x.shape[0]
    tm = TOKEN_TILE
    row = lambda i: (i, 0)
    const = lambda i: (0, 0)
    if split is None:
        split_blk = None
        out_shape = jax.ShapeDtypeStruct((n, D_MODEL), f32)
        out_specs = pl.BlockSpec((tm, D_MODEL), row)
    else:
        assert split % tm == 0 and (n - split) % tm == 0
        split_blk = split // tm
        out_shape = (jax.ShapeDtypeStruct((split, D_MODEL), f32),
                     jax.ShapeDtypeStruct((n - split, D_MODEL), f32))
        out_specs = (pl.BlockSpec((tm, D_MODEL), lambda i: (jnp.minimum(i, split_blk - 1), 0)),
                     pl.BlockSpec((tm, D_MODEL), lambda i: (jnp.maximum(i - split_blk, 0), 0)))
    return pl.pallas_call(
        functools.partial(_combine_ln_kernel, split_blk=split_blk),
        out_shape=out_shape,
        grid=(n // tm,),
        in_specs=[pl.BlockSpec((tm, D_MODEL), row),
                  pl.BlockSpec((tm, D_MODEL), row),
                  pl.BlockSpec((tm, D_MODEL), row),
                  pl.BlockSpec((tm, SUBLANES), row),
                  pl.BlockSpec((1, D_MODEL), const),
                  pl.BlockSpec((1, D_MODEL), const)],
        out_specs=out_specs,
        compiler_params=_params(("arbitrary",)),
        name="moe_combine_postnorm",
        interpret=interpret,
    )(x, y0, y1, w_col, g, b)


def _rows(x, idx):
    return x.at[idx].get(mode="promise_in_bounds")


def _moe(x, router_w_pad, router_b, wg, wu, wd, g, b, split=None, interpret=False):
    n = x.shape[0]
    route, e_w = _router(x, router_w_pad, router_b, interpret)
    e_idx = route[0:MOE_TOP_K]
    rank = route[MOE_TOP_K:2 * MOE_TOP_K]
    n_assign = n * MOE_TOP_K
    experts = jnp.arange(N_EXPERTS, dtype=jnp.int32)
    onehot = e_idx[:, :, None] == experts
    counts = jnp.sum(onehot, axis=(0, 1), dtype=jnp.int32)
    tm = EXPERT_TILE
    padded = (counts + tm - 1) // tm * tm
    pad_end = jnp.cumsum(padded)
    pad_start = pad_end - padded
    dest = jnp.sum(jnp.where(onehot, pad_start, 0), axis=-1) + rank
    n_blk = (n_assign + N_EXPERTS * (tm - 1)) // tm + 1
    tok = jnp.tile(jnp.arange(n, dtype=jnp.int32), MOE_TOP_K)
    row_tok = jnp.zeros((n_blk * tm,), jnp.int32).at[dest.reshape(-1)].set(
        tok, unique_indices=True, mode="promise_in_bounds")
    blk_start = jnp.arange(n_blk, dtype=jnp.int32) * tm
    blk_e = jnp.minimum(jnp.sum(pad_end[None, :] <= blk_start[:, None], axis=1, dtype=jnp.int32),
                        N_EXPERTS - 1)
    n_used = (pad_end[-1:] // tm).astype(jnp.int32)
    xs = _rows(x, row_tok)
    ys = _experts(blk_e, n_used, xs, wg, wu, wd, interpret)
    y0 = _rows(ys, dest[0])
    y1 = _rows(ys, dest[1])
    return _combine_ln(x, y0, y1, e_w.T, g, b, split, interpret)


def _rope_slab(t, cos, sin_signed, lo_half):
    swapped = jnp.where(lo_half, pltpu.roll(t, LANES - HEAD_DIM // 2, 1),
                        pltpu.roll(t, HEAD_DIM // 2, 1))
    return t * cos + swapped * sin_signed


def _qkv_kernel(x_ref, wq_ref, wkt_ref, wvt_ref, cos_ref, sin_ref, cost_ref, sint_ref,
                q_ref, kt_ref, vt_ref, ks_ref, vs_ref, *, prompt_blk):
    i = pl.program_id(0)
    tm = x_ref.shape[0]
    xb = x_ref[...].astype(bf16)
    cos = cos_ref[...]
    sin = sin_ref[...]
    lane = lax.broadcasted_iota(jnp.int32, (1, LANES), 1)
    lo_half = (lane % HEAD_DIM) < (HEAD_DIM // 2)
    for c in range(0, D_MODEL, COL_CHUNK):
        t = jnp.dot(xb, wq_ref[:, c:c + COL_CHUNK], preferred_element_type=f32)
        for s in range(0, COL_CHUNK, LANES):
            q_ref[(c + s) // LANES] = _rope_slab(t[:, s:s + LANES], cos, sin, lo_half)

    nt_dims = (((1,), (1,)), ((), ()))
    half = HEAD_DIM // 2
    hpc = COL_CHUNK // HEAD_DIM
    cos_t = cost_ref[...][None]
    sin_t = sint_ref[...][None]
    for r0 in range(0, D_MODEL, COL_CHUNK):
        rs = slice(r0, r0 + COL_CHUNK)
        hs = slice(r0 // HEAD_DIM, r0 // HEAD_DIM + hpc)
        kt = lax.dot_general(wkt_ref[rs, :], xb, nt_dims,
                             preferred_element_type=f32).reshape(hpc, HEAD_DIM, tm)
        sw = jnp.concatenate([kt[:, half:, :], kt[:, :half, :]], axis=1)
        kr = kt * cos_t + sw * sin_t
        vr = lax.dot_general(wvt_ref[rs, :], xb, nt_dims,
                             preferred_element_type=f32).reshape(hpc, HEAD_DIM, tm)

        @pl.when(i < prompt_blk)
        def _(kr=kr, vr=vr, hs=hs):
            kt_ref[0, hs] = kr
            vt_ref[0, hs] = vr

        @pl.when(i >= prompt_blk)
        def _(kr=kr, vr=vr, rs=rs):
            ks_ref[:, rs] = kr.reshape(COL_CHUNK, tm).T
            vs_ref[:, rs] = vr.reshape(COL_CHUNK, tm).T


def _qkv(x, wq, wkt, wvt, cos, sin, cos_t, sin_t, n_seq, seq_len, interpret=False):
    n = x.shape[0]
    tm = TOKEN_TILE
    n_p = n_seq * seq_len
    assert n_p % tm == 0 and seq_len % tm == 0 and (n - n_p) % tm == 0
    prompt_blk = n_p // tm
    tps = seq_len // tm
    row = lambda i: (i, 0)
    const = lambda i: (0, 0)

    def head_major(i):
        j = jnp.minimum(i, prompt_blk - 1)
        return (j // tps, 0, 0, j % tps)

    sample_rows = lambda i: (jnp.maximum(i - prompt_blk, 0), 0)
    kv_t = jax.ShapeDtypeStruct((n_seq, N_HEADS, HEAD_DIM, seq_len), f32)
    kv_s = jax.ShapeDtypeStruct((n - n_p, D_MODEL), f32)
    return pl.pallas_call(
        functools.partial(_qkv_kernel, prompt_blk=prompt_blk),
        out_shape=(jax.ShapeDtypeStruct((HEAD_PAIRS, n, LANES), f32), kv_t, kv_t, kv_s, kv_s),
        grid=(n // tm,),
        in_specs=[pl.BlockSpec((tm, D_MODEL), row),
                  pl.BlockSpec((D_MODEL, D_MODEL), const),
                  pl.BlockSpec((D_MODEL, D_MODEL), const),
                  pl.BlockSpec((D_MODEL, D_MODEL), const),
                  pl.BlockSpec((tm, LANES), row),
                  pl.BlockSpec((tm, LANES), row),
                  pl.BlockSpec((HEAD_DIM, tm), lambda i: (0, i)),
                  pl.BlockSpec((HEAD_DIM, tm), lambda i: (0, i))],
        out_specs=(pl.BlockSpec((HEAD_PAIRS, tm, LANES), lambda i: (0, i, 0)),
                   pl.BlockSpec((1, N_HEADS, HEAD_DIM, tm), head_major),
                   pl.BlockSpec((1, N_HEADS, HEAD_DIM, tm), head_major),
                   pl.BlockSpec((tm, D_MODEL), sample_rows),
                   pl.BlockSpec((tm, D_MODEL), sample_rows)),
        compiler_params=_params(("arbitrary",)),
        name="qkv_rope",
        interpret=interpret,
    )(x, wq, wkt, wvt, cos, sin, cos_t, sin_t)


def _rope_tables(pos):
    half = HEAD_DIM // 2
    inv = ROPE_THETA ** (-jnp.arange(half, dtype=f32) / half)
    ang = pos.astype(f32)[:, None] * inv
    cos, sin = jnp.cos(ang), jnp.sin(ang)
    cos_h = jnp.concatenate([cos, cos], axis=1)
    sin_h = jnp.concatenate([-sin, sin], axis=1)
    rep = (1, LANES // HEAD_DIM)
    return jnp.tile(cos_h, rep), jnp.tile(sin_h, rep), cos_h.T, sin_h.T


def _moba_prompt_kernel(q_ref, kt_ref, vt_ref, o_ref, kaug, vaug, kmcols, *, nb):
    BLK = MOBA_BLOCK
    L = kt_ref.shape[3]
    i = pl.program_id(2)
    lane = lax.broadcasted_iota(jnp.int32, (1, LANES), 1)
    lo = lane < HEAD_DIM
    nt_dims = (((1,), (1,)), ((), ()))

    @pl.when(i == 0)
    def _prep():
        zcol = jnp.zeros((HEAD_DIM, 1), f32)
        kmc = jnp.zeros((LANES, LANES), f32)
        for j in range(nb):
            cs = slice(j * BLK, (j + 1) * BLK)
            k0 = kt_ref[0, 0, :, cs]
            k1 = kt_ref[0, 1, :, cs]
            ind = jnp.where(lax.broadcasted_iota(jnp.int32, (HEAD_DIM, BLK), 0) == j,
                            1.0, 0.0).astype(bf16)
            ones = jnp.ones((HEAD_DIM, BLK), bf16)
            kaug[0, :, cs] = jnp.concatenate([k0.astype(bf16), ind], axis=0)
            kaug[1, :, cs] = jnp.concatenate([ind, k1.astype(bf16)], axis=0)
            vaug[0, :, cs] = jnp.concatenate([vt_ref[0, 0, :, cs].astype(bf16), ones], axis=0)
            vaug[1, :, cs] = jnp.concatenate([ones, vt_ref[0, 1, :, cs].astype(bf16)], axis=0)
            m0 = jnp.concatenate([jnp.mean(k0, axis=1, keepdims=True), zcol], axis=0)
            m1 = jnp.concatenate([zcol, jnp.mean(k1, axis=1, keepdims=True)], axis=0)
            kmc = jnp.where(lane == HEAD_DIM + j, m0, jnp.where(lane == j, m1, kmc))
        kmcols[...] = kmc

    row = lax.broadcasted_iota(jnp.int32, (BLK, BLK), 0)
    col = lax.broadcasted_iota(jnp.int32, (BLK, BLK), 1)
    diag_mask = col <= row

    def attend(iv):
        qp = q_ref[0]
        qs = qp * (HEAD_DIM ** -0.5)
        q_own = [jnp.where(lo, qs, 0.0), jnp.where(lo, 0.0, qs)]
        if iv > MOBA_TOPK:
            gate = jnp.dot(qp, kmcols[...], preferred_element_type=f32, precision=HIGHEST)
            blkid = lane & (HEAD_DIM - 1)
            past = blkid < iv
            gate = jnp.where(past, gate, -jnp.inf)
            cnt = jnp.zeros((BLK, LANES), f32)
            for j in range(iv):
                c0 = jnp.broadcast_to(gate[:, HEAD_DIM + j:HEAD_DIM + j + 1], (BLK, LANES))
                c1 = jnp.broadcast_to(gate[:, j:j + 1], (BLK, LANES))
                cj = jnp.where(lo, c1, c0)
                ahead = (cj > gate) | ((cj == gate) & (blkid > j))
                cnt = cnt + jnp.where(ahead, 1.0, 0.0)
            bias = jnp.where((cnt < MOBA_TOPK) & past, 0.0, MASK_BIAS)
            q_aug = [jnp.where(lo, qs, bias), jnp.where(lo, bias, qs)]
        else:
            q_aug = q_own
        outs = []
        own = slice(iv * BLK, (iv + 1) * BLK)
        for hh in range(2):
            s_own = jnp.dot(q_own[hh].astype(bf16), kaug[hh, :, own], preferred_element_type=f32)
            s_own = jnp.where(diag_mask, s_own, -jnp.inf)
            m = jnp.max(s_own, axis=-1, keepdims=True)
            if iv > 0:
                s_past = jnp.dot(q_aug[hh].astype(bf16), kaug[hh, :, 0:iv * BLK],
                                 preferred_element_type=f32)
                m = jnp.maximum(m, jnp.max(s_past, axis=-1, keepdims=True))
                acc = lax.dot_general(jnp.exp(s_past - m).astype(bf16), vaug[hh, :, 0:iv * BLK],
                                      nt_dims, preferred_element_type=f32)
            p_own = jnp.exp(s_own - m).astype(bf16)
            acc_own = lax.dot_general(p_own, vaug[hh, :, own], nt_dims,
                                      preferred_element_type=f32)
            acc = acc + acc_own if iv > 0 else acc_own
            outs.append(acc / pltpu.roll(acc, HEAD_DIM, 1))
        o_ref[0] = jnp.where(lo, outs[0], outs[1])

    for iv in range(nb):
        pl.when(i == iv)(functools.partial(attend, iv))


def _moba_prompt(q_hp, k_t, v_t, interpret=False):
    BLK = MOBA_BLOCK
    n_seq, _, _, seq_len = k_t.shape
    nb = seq_len // BLK
    assert nb <= HEAD_DIM and seq_len % BLK == 0
    qo_map = lambda b, hp, i: (hp, b * nb + i, 0)
    kv_map = lambda b, hp, i: (b, hp, 0, 0)
    return pl.pallas_call(
        functools.partial(_moba_prompt_kernel, nb=nb),
        out_shape=jax.ShapeDtypeStruct((HEAD_PAIRS, n_seq * seq_len, LANES), f32),
        grid=(n_seq, HEAD_PAIRS, nb),
        in_specs=[pl.BlockSpec((1, BLK, LANES), qo_map),
                  pl.BlockSpec((1, 2, HEAD_DIM, seq_len), kv_map),
                  pl.BlockSpec((1, 2, HEAD_DIM, seq_len), kv_map)],
        out_specs=pl.BlockSpec((1, BLK, LANES), qo_map),
        scratch_shapes=[pltpu.VMEM((2, LANES, seq_len), bf16),
                        pltpu.VMEM((2, LANES, seq_len), bf16),
                        pltpu.VMEM((LANES, LANES), f32)],
        compiler_params=_params(("parallel", "parallel", "arbitrary")),
        name="moba_prompt_attn",
        interpret=interpret,
    )(q_hp, k_t, v_t)


def _page_sum_kernel(pt_ref, *refs):
    pages, o_ref = refs[:PAGES_PER_STEP], refs[PAGES_PER_STEP]
    ppb = MOBA_BLOCK // PAGE_SIZE
    bps = PAGES_PER_STEP // ppb
    s = pl.program_id(1)
    lane = lax.broadcasted_iota(jnp.int32, (1, LANES), 1)

    @pl.when(s == 0)
    def _():
        o_ref[...] = jnp.zeros(o_ref.shape, f32)

    for h in range(N_HEADS):
        acc = o_ref[0, h]
        for blk in range(bps):
            t = pages[blk * ppb][0, h]
            for r in range(1, ppb):
                t = t + pages[blk * ppb + r][0, h]
            col = jnp.sum(t, axis=-1, keepdims=True)
            acc = jnp.where(lane == s * bps + blk, col, acc)
        o_ref[0, h] = acc


def _page_sums(page_table, cache_kt, n_blocks, interpret=False):
    bsz, n_pages = page_table.shape
    ppb = MOBA_BLOCK // PAGE_SIZE
    assert n_blocks <= LANES and (n_blocks * ppb) % PAGES_PER_STEP == 0
    steps = n_blocks * ppb // PAGES_PER_STEP
    pt_flat = page_table.reshape(-1)

    def page_map(r):
        return lambda b, s, pt: (pt[b * n_pages + s * PAGES_PER_STEP + r], 0, 0, 0)

    return pl.pallas_call(
        _page_sum_kernel,
        out_shape=jax.ShapeDtypeStruct((bsz, N_HEADS, HEAD_DIM, LANES), f32),
        grid_spec=pltpu.PrefetchScalarGridSpec(
            num_scalar_prefetch=1,
            grid=(bsz, steps),
            in_specs=[pl.BlockSpec((1, N_HEADS, HEAD_DIM, PAGE_SIZE), page_map(r))
                      for r in range(PAGES_PER_STEP)],
            out_specs=pl.BlockSpec((1, N_HEADS, HEAD_DIM, LANES), lambda b, s, pt: (b, 0, 0, 0))),
        compiler_params=_params(("parallel", "arbitrary")),
        name="moba_page_sums",
        interpret=interpret,
    )(pt_flat, *([cache_kt] * PAGES_PER_STEP))


def _sample_topk_kernel(q_ref, ks_ref, idx_ref, *, n_blocks, topk):
    lane = lax.broadcasted_iota(jnp.int32, (1, LANES), 1)
    lane_f = lane.astype(f32)
    for h in range(N_HEADS):
        g = jnp.dot(q_ref[0, h], ks_ref[0, h], preferred_element_type=f32,
                    precision=HIGHEST) * (1.0 / MOBA_BLOCK)
        g = jnp.where(lane < n_blocks, g, -jnp.inf)
        out = jnp.zeros(g.shape, f32)
        for t in range(topk):
            m = jnp.max(g, axis=-1, keepdims=True)
            idx = jnp.min(jnp.where(g == m, lane_f, float(LANES)), axis=-1, keepdims=True)
            out = jnp.where(lane == t, idx, out)
            g = jnp.where(lane_f == idx, -jnp.inf, g)
        idx_ref[0, h] = out.astype(jnp.int32)


def _sample_topk(q4, ksum_t, n_blocks, topk, interpret=False):
    bsz, _, nq, _ = q4.shape
    return pl.pallas_call(
        functools.partial(_sample_topk_kernel, n_blocks=n_blocks, topk=topk),
        out_shape=jax.ShapeDtypeStruct((bsz, N_HEADS, nq, LANES), jnp.int32),
        grid=(bsz,),
        in_specs=[pl.BlockSpec((1, N_HEADS, nq, HEAD_DIM), lambda b: (b, 0, 0, 0)),
                  pl.BlockSpec((1, N_HEADS, HEAD_DIM, LANES), lambda b: (b, 0, 0, 0))],
        out_specs=pl.BlockSpec((1, N_HEADS, nq, LANES), lambda b: (b, 0, 0, 0)),
        compiler_params=_params(("parallel",)),
        name="moba_sample_topk",
        interpret=interpret,
    )(q4, ksum_t)


def _sample_attn_kernel(phys_ref, q_ref, kn_ref, vn_ref, ck_hbm, cv_hbm, o_ref,
                        kbuf, vbuf, sem, *, nq, n_sel):
    g = pl.program_id(0)
    ng = pl.num_programs(0)
    slot = g % 2
    n_pg = nq * n_sel

    def page_copies(gi, sl, r):
        page = phys_ref[gi * n_pg + r]
        head = gi % N_HEADS
        return (pltpu.make_async_copy(ck_hbm.at[page, head], kbuf.at[sl, r], sem.at[0, sl]),
                pltpu.make_async_copy(cv_hbm.at[page, head], vbuf.at[sl, r], sem.at[1, sl]))

    def fetch(gi, sl):
        for r in range(n_pg):
            ck, cv = page_copies(gi, sl, r)
            ck.start()
            cv.start()

    @pl.when(g == 0)
    def _():
        fetch(0, 0)

    @pl.when(g + 1 < ng)
    def _():
        fetch(g + 1, 1 - slot)

    for r in range(n_pg):
        ck, cv = page_copies(g, slot, r)
        ck.wait()
        cv.wait()

    lane = lax.broadcasted_iota(jnp.int32, (1, LANES), 1)
    q_t = q_ref[0, 0] * (HEAD_DIM ** -0.5)
    kn = kn_ref[0, 0]
    vn = vn_ref[0, 0]
    out = jnp.zeros((HEAD_DIM, LANES), f32)
    for qi in range(nq):
        qb = jnp.broadcast_to(q_t[:, qi:qi + 1], (HEAD_DIM, LANES))
        s_own = jnp.sum(kn * qb, axis=0, keepdims=True)
        s_own = jnp.where(lane <= qi, s_own, -jnp.inf)
        s_sel = [jnp.sum(kbuf[slot, qi * n_sel + r] * qb, axis=0, keepdims=True)
                 for r in range(n_sel)]
        m_lane = s_own
        for s in s_sel:
            m_lane = jnp.maximum(m_lane, s)
        m = jnp.max(m_lane, axis=-1, keepdims=True)
        p_own = jnp.exp(s_own - m)
        p_sum = p_own
        acc = vn * p_own
        for r, s in enumerate(s_sel):
            p = jnp.exp(s - m)
            p_sum = p_sum + p
            acc = acc + vbuf[slot, qi * n_sel + r] * p
        den = jnp.sum(p_sum, axis=-1, keepdims=True)
        o_col = jnp.sum(acc, axis=-1, keepdims=True) / den
        out = jnp.where(lane == qi, o_col, out)
    o_ref[0, 0] = out


def _sample_attn(phys, q_t, kn_t, vn_t, cache_kt, cache_vt, nq, n_sel, interpret=False):
    bsz = q_t.shape[0]
    small = pl.BlockSpec((1, 1, HEAD_DIM, LANES),
                         lambda g, ph: (g // N_HEADS, g % N_HEADS, 0, 0))
    hbm = pl.BlockSpec(memory_space=pl.ANY)
    return pl.pallas_call(
        functools.partial(_sample_attn_kernel, nq=nq, n_sel=n_sel),
        out_shape=jax.ShapeDtypeStruct((bsz, N_HEADS, HEAD_DIM, LANES), f32),
        grid_spec=pltpu.PrefetchScalarGridSpec(
            num_scalar_prefetch=1,
            grid=(bsz * N_HEADS,),
            in_specs=[small, small, small, hbm, hbm],
            out_specs=small,
            scratch_shapes=[pltpu.VMEM((2, nq * n_sel, HEAD_DIM, PAGE_SIZE), f32),
                            pltpu.VMEM((2, nq * n_sel, HEAD_DIM, PAGE_SIZE), f32),
                            pltpu.SemaphoreType.DMA((2, 2))]),
        compiler_params=_params(("arbitrary",)),
        name="moba_sample_attn",
        interpret=interpret,
    )(phys, q_t, kn_t, vn_t, cache_kt, cache_vt)


def _heads_first(t, bsz, seq_len):
    return t.reshape(bsz, seq_len, N_HEADS, HEAD_DIM).transpose(0, 2, 1, 3)


def _forward(x_prompt, x_sample, state_ssm, state_conv, cache_k, cache_v, page_table,
             ln_gain, ln_bias, ssm_w_in, ssm_w_conv, ssm_b_conv, ssm_dt_bias, ssm_a_log,
             ssm_d, ssm_norm_w, ssm_w_out, attn_w_q, attn_w_kv, attn_w_o,
             router_w, router_b, moe_w_gate, moe_w_up, moe_w_down, interpret=False):
    bp, lp, _ = x_prompt.shape
    bs, ls, _ = x_sample.shape
    n_p, n_s = bp * lp, bs * ls
    n = n_p + n_s
    n_pages = page_table.shape[1]
    past_len = n_pages * PAGE_SIZE
    assert n % TOKEN_TILE == 0 and lp % MOBA_BLOCK == 0 and lp % SSD_CHUNK == 0
    assert CONV_W - 1 <= ls <= SUBLANES and n_p % ls == 0
    assert past_len % MOBA_BLOCK == 0 and past_len // MOBA_BLOCK >= MOBA_TOPK
    assert ls <= MOBA_BLOCK - past_len % MOBA_BLOCK

    h = jnp.concatenate([x_prompt.reshape(n_p, D_MODEL), x_sample.reshape(n_s, D_MODEL)], axis=0)
    router_wp = jnp.pad(router_w, ((0, 0), (0, LANES - N_EXPERTS)))
    router_bc = router_b.reshape(N_EXPERTS, 1)

    def vec(v):
        return v.reshape(1, -1)

    l = 0
    w_in = ssm_w_in[l]
    wz = w_in[:, :D_INNER].astype(bf16)
    wx = w_in[:, D_INNER:D_INNER + CONV_DIM].astype(bf16)
    wdt = jnp.pad(w_in[:, D_INNER + CONV_DIM:], ((0, 0), (0, LANES - SSM_HEADS)))
    z, xbc, dt = _in_proj(h, wz, wx, wdt, interpret)

    pad_h = (0, LANES - SSM_HEADS)
    wconv = jnp.pad(ssm_w_conv[l], ((0, SUBLANES - CONV_W), (0, 0)))
    ssd_w = (wconv, vec(ssm_b_conv[l]), vec(jnp.pad(ssm_dt_bias[l], pad_h)),
             vec(jnp.pad(ssm_a_log[l], pad_h)), vec(jnp.repeat(ssm_d[l], SSM_HEAD_DIM)),
             vec(ssm_norm_w[l]))
    y_p, st_p = _ssd(z, xbc, dt, None, None, *ssd_w, n_seq=bp, seq_len=lp, row_off=0,
                     interpret=interpret)
    conv0 = jnp.pad(state_conv[l], ((0, 0), (SUBLANES - (CONV_W - 1), 0), (0, 0)))
    y_s, st_s = _ssd(z, xbc, dt, conv0, _state_to_pairs(state_ssm[l]), *ssd_w,
                     n_seq=bs, seq_len=ls, row_off=n_p, interpret=interpret)
    ssm_prompt = _pairs_to_state(st_p)[None]
    ssm_sample = _pairs_to_state(st_s)[None]
    tail = jnp.arange(-(CONV_W - 1), 0, dtype=jnp.int32)
    rows_p = ((jnp.arange(bp, dtype=jnp.int32) + 1) * lp)[:, None] + tail
    rows_s = (n_p + (jnp.arange(bs, dtype=jnp.int32) + 1) * ls)[:, None] + tail
    conv_prompt = _rows(xbc, rows_p.reshape(-1)).reshape(1, bp, CONV_W - 1, CONV_DIM)
    conv_sample = _rows(xbc, rows_s.reshape(-1)).reshape(1, bs, CONV_W - 1, CONV_DIM)

    h = _proj_ln(y_p, y_s, ssm_w_out[l].astype(bf16), h, vec(ln_gain[l, 0]), vec(ln_bias[l, 0]),
                 interpret)
    h = _moe(h, router_wp, router_bc, moe_w_gate[l], moe_w_up[l], moe_w_down[l],
             vec(ln_gain[l, 1]), vec(ln_bias[l, 1]), interpret=interpret)

    l = 1
    pos = jnp.concatenate([jnp.tile(jnp.arange(lp), bp), jnp.tile(past_len + jnp.arange(ls), bs)])
    cos_tok, sin_tok, cos_dim, sin_dim = _rope_tables(pos)
    hd_all = N_HEADS * HEAD_DIM
    q_hp, k_t, v_t, k_s, v_s = _qkv(h, attn_w_q[0].astype(bf16),
                                    attn_w_kv[:, :hd_all].T.astype(bf16),
                                    attn_w_kv[:, hd_all:].T.astype(bf16),
                                    cos_tok, sin_tok, cos_dim, sin_dim, bp, lp, interpret)
    k_prompt = jnp.swapaxes(k_t, 2, 3)
    v_prompt = jnp.swapaxes(v_t, 2, 3)
    k_sample = _heads_first(k_s, bs, ls)
    v_sample = _heads_first(v_s, bs, ls)
    q_sample = (q_hp[:, n_p:, :].reshape(HEAD_PAIRS, bs, ls, 2, HEAD_DIM)
                .transpose(1, 0, 3, 2, 4).reshape(bs, N_HEADS, ls, HEAD_DIM))

    o_p = _moba_prompt(q_hp, k_t, v_t, interpret)

    n_full = past_len // MOBA_BLOCK
    ppb = MOBA_BLOCK // PAGE_SIZE
    cache_kt = jnp.swapaxes(cache_k, 2, 3)
    cache_vt = jnp.swapaxes(cache_v, 2, 3)
    ksum_t = _page_sums(page_table, cache_kt, n_full, interpret)
    idx = _sample_topk(q_sample, ksum_t, n_full, MOBA_TOPK, interpret)[..., :MOBA_TOPK]
    lpage = idx[..., None] * ppb + jnp.arange(ppb, dtype=jnp.int32)
    phys = page_table[jnp.arange(bs)[:, None, None, None, None], lpage]

    def lanes_last(t):
        return jnp.pad(jnp.swapaxes(t, 2, 3), ((0, 0), (0, 0), (0, 0), (0, LANES - ls)))

    o_st = _sample_attn(phys.reshape(-1).astype(jnp.int32), lanes_last(q_sample),
                        lanes_last(k_sample), lanes_last(v_sample), cache_kt, cache_vt,
                        ls, MOBA_TOPK * ppb, interpret)
    o_s = o_st[..., :ls].transpose(0, 3, 1, 2).reshape(n_s, D_MODEL)

    h = _proj_ln(o_p, o_s, attn_w_o[0].astype(bf16), h, vec(ln_gain[l, 0]), vec(ln_bias[l, 0]),
                 interpret)
    h_p, h_s = _moe(h, router_wp, router_bc, moe_w_gate[l], moe_w_up[l], moe_w_down[l],
                    vec(ln_gain[l, 1]), vec(ln_bias[l, 1]), split=n_p, interpret=interpret)

    y_prompt = h_p.reshape(bp, lp, D_MODEL)
    y_sample = h_s.reshape(bs, ls, D_MODEL)
    return (y_prompt, y_sample, ssm_prompt, conv_prompt, k_prompt, v_prompt,
            ssm_sample, conv_sample, k_sample, v_sample)


def kernel(x_prompt, x_sample, state_ssm, state_conv, cache_k, cache_v, page_table, ln_gain, ln_bias, ssm_w_in, ssm_w_conv, ssm_b_conv, ssm_dt_bias, ssm_a_log, ssm_d, ssm_norm_w, ssm_w_out, attn_w_q, attn_w_kv, attn_w_o, router_w, router_b, moe_w_gate, moe_w_up, moe_w_down):
    return _forward(x_prompt, x_sample, state_ssm, state_conv, cache_k, cache_v, page_table,
                    ln_gain, ln_bias, ssm_w_in, ssm_w_conv, ssm_b_conv, ssm_dt_bias, ssm_a_log,
                    ssm_d, ssm_norm_w, ssm_w_out, attn_w_q, attn_w_kv, attn_w_o,
                    router_w, router_b, moe_w_gate, moe_w_up, moe_w_down)
```

```python
import functools
import math

import jax
import jax.numpy as jnp
from jax import lax
from jax.experimental import pallas as pl
from jax.experimental.pallas import tpu as pltpu

f32 = jnp.float32
bf16 = jnp.bfloat16
HIGHEST = lax.Precision.HIGHEST

D_MODEL = 1024
DEPTH = 2
PAGE_SIZE = 128
N_A_LAYERS = 1
SSM_HEAD_DIM = 64
D_INNER = 2048
SSM_HEADS = D_INNER // SSM_HEAD_DIM
SSM_GROUPS = 4
D_STATE = 128
CONV_W = 4
CONV_DIM = D_INNER + 2 * SSM_GROUPS * D_STATE
SSD_CHUNK = 128
N_HEADS = 16
HEAD_DIM = 64
MOBA_BLOCK = 256
MOBA_TOPK = 3
ROPE_THETA = 10000.0
N_EXPERTS = 16
N_EXPERT_GROUPS = 4
EXPERTS_PER_GROUP = 4
MOE_TOP_K = 2
D_EXPERT = 1024
DEEPNORM_ALPHA = (2.0 * DEPTH) ** 0.25
LN_EPS = 1e-5
RMS_EPS = 1e-6
MASK_BIAS = -1e30

LANES = 128
SUBLANES = 8
VMEM_LIMIT = 48 * 1024 * 1024

TOKEN_TILE = 256
EXPERT_TILE = 256
CAST_ROWS = 256
EXPERT_VMEM_LIMIT = 56 * 1024 * 1024
COL_CHUNK = 512
PAGES_PER_STEP = 8
HEAD_PAIRS = N_HEADS // 2
SSM_PAIRS = SSM_HEADS // 2


def _params(sem):
    return pltpu.CompilerParams(dimension_semantics=sem, vmem_limit_bytes=VMEM_LIMIT)


def _sigmoid(x):
    return 1.0 / (1.0 + jnp.exp(-x))


def _softplus(x):
    u = jnp.exp(-jnp.abs(x))
    w = 1.0 + u
    log1p_u = jnp.where(w == 1.0, u, jnp.log(w) * (u / (w - 1.0)))
    return jnp.maximum(x, 0.0) + log1p_u


def _layer_norm_rows(v, g, b):
    mu = jnp.mean(v, axis=-1, keepdims=True)
    d = v - mu
    var = jnp.mean(d * d, axis=-1, keepdims=True)
    return d * lax.rsqrt(var + LN_EPS) * g + b


def _in_proj_kernel(x_ref, wz_ref, wx_ref, wdt_ref, z_ref, xbc_ref, dt_ref):
    x = x_ref[...]
    xb = x.astype(bf16)
    for c in range(0, D_INNER, COL_CHUNK):
        z_ref[:, c:c + COL_CHUNK] = jnp.dot(xb, wz_ref[:, c:c + COL_CHUNK],
                                            preferred_element_type=f32)
    for c in range(0, CONV_DIM, COL_CHUNK):
        xbc_ref[:, c:c + COL_CHUNK] = jnp.dot(xb, wx_ref[:, c:c + COL_CHUNK],
                                              preferred_element_type=f32)
    dt_ref[...] = jnp.dot(x, wdt_ref[...], preferred_element_type=f32, precision=HIGHEST)


def _in_proj(x, wz, wx, wdt, interpret=False):
    n = x.shape[0]
    tm = TOKEN_TILE
    return pl.pallas_call(
        _in_proj_kernel,
        out_shape=(jax.ShapeDtypeStruct((n, D_INNER), f32),
                   jax.ShapeDtypeStruct((n, CONV_DIM), f32),
                   jax.ShapeDtypeStruct((n, LANES), f32)),
        grid=(n // tm,),
        in_specs=[pl.BlockSpec((tm, D_MODEL), lambda i: (i, 0)),
                  pl.BlockSpec((D_MODEL, D_INNER), lambda i: (0, 0)),
                  pl.BlockSpec((D_MODEL, CONV_DIM), lambda i: (0, 0)),
                  pl.BlockSpec((D_MODEL, LANES), lambda i: (0, 0))],
        out_specs=(pl.BlockSpec((tm, D_INNER), lambda i: (i, 0)),
                   pl.BlockSpec((tm, CONV_DIM), lambda i: (i, 0)),
                   pl.BlockSpec((tm, LANES), lambda i: (i, 0))),
        compiler_params=_params(("parallel",)),
        name="ssm_in_proj",
        interpret=interpret,
    )(x, wz, wx, wdt)


def _ssd_kernel(*refs, c_in, nc, has_init):
    T = SSD_CHUNK
    if has_init:
        (z_ref, xbc_ref, dt_ref, conv0_ref, st0_ref, wconv_ref, bconv_ref, dtb_ref, alog_ref,
         d_ref, nw_ref, y_ref, st_out_ref, xbuf, xc, ybuf, st) = refs
    else:
        (z_ref, xbc_ref, dt_ref, wconv_ref, bconv_ref, dtb_ref, alog_ref,
         d_ref, nw_ref, y_ref, st_out_ref, xbuf, xc, ybuf, st) = refs
    c = pl.program_id(1)

    @pl.when(c == 0)
    def _init():
        if has_init:
            xbuf[0:SUBLANES, :] = conv0_ref[0]
            st[...] = st0_ref[0]
        else:
            xbuf[0:SUBLANES, :] = jnp.zeros((SUBLANES, CONV_DIM), f32)
            st[...] = jnp.zeros(st.shape, f32)

    xbuf[SUBLANES:SUBLANES + c_in, :] = xbc_ref[...]
    if c_in < T:
        xbuf[SUBLANES + c_in:SUBLANES + T, :] = jnp.zeros((T - c_in, CONV_DIM), f32)

    for c0 in range(0, CONV_DIM, COL_CHUNK):
        cs = slice(c0, c0 + COL_CHUNK)
        xin = xbuf[0:T + SUBLANES, cs]
        acc = bconv_ref[:, cs]
        for k in range(CONV_W):
            shift = CONV_W - 1 - k
            tap = xin if shift == 0 else pltpu.roll(xin, shift, 0)
            acc = acc + tap[SUBLANES:, :] * wconv_ref[k:k + 1, cs]
        xc[:, cs] = acc * _sigmoid(acc)
    if nc > 1:
        xbuf[0:SUBLANES, :] = xbuf[T:T + SUBLANES, :]

    def pad_rows(v):
        if c_in == T:
            return v
        return jnp.concatenate([v, jnp.zeros((T - c_in, v.shape[1]), v.dtype)], axis=0)

    dtv = pad_rows(_softplus(dt_ref[...] + dtb_ref[...]))
    a = -jnp.exp(alog_ref[...])
    da = dtv * a
    row = lax.broadcasted_iota(jnp.int32, (T, T), 0)
    col = lax.broadcasted_iota(jnp.int32, (T, T), 1)
    causal = col <= row
    tri = causal.astype(f32)
    acs = jnp.dot(tri, da, preferred_element_type=f32, precision=HIGHEST)
    acs_t = acs.T
    dt_t = dtv.T
    w_all = dt_t * jnp.exp(acs_t[:, T - 1:T] - acs_t)
    lo = lax.broadcasted_iota(jnp.int32, (1, LANES), 1) < SSM_HEAD_DIM

    for g in range(SSM_GROUPS):
        bm = xc[:, D_INNER + g * D_STATE:D_INNER + (g + 1) * D_STATE]
        cm = xc[:, D_INNER + (SSM_GROUPS + g) * D_STATE:D_INNER + (SSM_GROUPS + g + 1) * D_STATE]
        cb = lax.dot_general(cm.astype(bf16), bm.astype(bf16), (((1,), (1,)), ((), ())),
                             preferred_element_type=f32)
        bm_t = bm.T
        for e in range(SSM_PAIRS // SSM_GROUPS):
            pr = g * (SSM_PAIRS // SSM_GROUPS) + e
            ls = slice(pr * LANES, (pr + 1) * LANES)
            x_pair = xc[:, ls]
            xb = x_pair.astype(bf16)
            rhs = jnp.concatenate([xb, st[pr].astype(bf16)], axis=0)
            r, u, ea_last = [], [], []
            for hh in range(2):
                h = 2 * pr + hh
                a_b = jnp.broadcast_to(acs[:, h:h + 1], (T, LANES))
                seg = a_b - acs_t[h:h + 1, :]
                dec = jnp.exp(jnp.where(causal, seg, -jnp.inf))
                m_h = cb * dec * dt_t[h:h + 1, :]
                e_a = jnp.exp(a_b)
                lhs = jnp.concatenate([m_h.astype(bf16), (cm * e_a).astype(bf16)], axis=1)
                r.append(jnp.dot(lhs, rhs, preferred_element_type=f32))
                lhs_s = (bm_t * w_all[h:h + 1, :]).astype(bf16)
                u.append(jnp.dot(lhs_s, xb, preferred_element_type=f32))
                ea_last.append(e_a[T - 1:T, :])
            st[pr] = st[pr] * jnp.where(lo, ea_last[0], ea_last[1]) + jnp.where(lo, u[0], u[1])
            yv = jnp.where(lo, r[0], r[1]) + d_ref[:, ls] * x_pair
            zz = pad_rows(z_ref[:, ls])
            ybuf[:, ls] = yv * (zz * _sigmoid(zz))
        gw = D_INNER // SSM_GROUPS
        gs = slice(g * gw, (g + 1) * gw)
        yg = ybuf[:, gs]
        ms = jnp.mean(yg * yg, axis=-1, keepdims=True)
        y_ref[:, gs] = (yg * lax.rsqrt(ms + RMS_EPS) * nw_ref[:, gs])[0:c_in]

    @pl.when(c == nc - 1)
    def _fin():
        st_out_ref[0] = st[...]


def _ssd(z, xbc, dt, conv0, st0, wconv, bconv, dtb, alog, d_exp, nw, *,
         n_seq, seq_len, row_off, interpret=False):
    T = SSD_CHUNK
    has_init = conv0 is not None
    if seq_len >= T:
        c_in, nc = T, seq_len // T
    else:
        c_in, nc = seq_len, 1
    off = row_off // c_in

    def rows(bi, ci):
        return (off + bi * nc + ci, 0)

    const = lambda bi, ci: (0, 0)
    in_specs = [pl.BlockSpec((c_in, D_INNER), rows),
                pl.BlockSpec((c_in, CONV_DIM), rows),
                pl.BlockSpec((c_in, LANES), rows)]
    args = [z, xbc, dt]
    if has_init:
        in_specs += [pl.BlockSpec((1, SUBLANES, CONV_DIM), lambda bi, ci: (bi, 0, 0)),
                     pl.BlockSpec((1, SSM_PAIRS, D_STATE, LANES), lambda bi, ci: (bi, 0, 0, 0))]
        args += [conv0, st0]
    in_specs += [pl.BlockSpec((SUBLANES, CONV_DIM), const),
                 pl.BlockSpec((1, CONV_DIM), const),
                 pl.BlockSpec((1, LANES), const),
                 pl.BlockSpec((1, LANES), const),
                 pl.BlockSpec((1, D_INNER), const),
                 pl.BlockSpec((1, D_INNER), const)]
    args += [wconv, bconv, dtb, alog, d_exp, nw]
    return pl.pallas_call(
        functools.partial(_ssd_kernel, c_in=c_in, nc=nc, has_init=has_init),
        out_shape=(jax.ShapeDtypeStruct((n_seq * seq_len, D_INNER), f32),
                   jax.ShapeDtypeStruct((n_seq, SSM_PAIRS, D_STATE, LANES), f32)),
        grid=(n_seq, nc),
        in_specs=in_specs,
        out_specs=(pl.BlockSpec((c_in, D_INNER), lambda bi, ci: (bi * nc + ci, 0)),
                   pl.BlockSpec((1, SSM_PAIRS, D_STATE, LANES), lambda bi, ci: (bi, 0, 0, 0))),
        scratch_shapes=[pltpu.VMEM((T + 2 * SUBLANES, CONV_DIM), f32),
                        pltpu.VMEM((T, CONV_DIM), f32),
                        pltpu.VMEM((T, D_INNER), f32),
                        pltpu.VMEM((SSM_PAIRS, D_STATE, LANES), f32)],
        compiler_params=_params(("parallel", "arbitrary")),
        name="ssd_scan_init" if has_init else "ssd_scan",
        interpret=interpret,
    )(*args)


def _state_to_pairs(s):
    b = s.shape[0]
    s = s.reshape(b, SSM_PAIRS, 2, SSM_HEAD_DIM, D_STATE)
    return s.transpose(0, 1, 4, 2, 3).reshape(b, SSM_PAIRS, D_STATE, 2 * SSM_HEAD_DIM)


def _pairs_to_state(s):
    b = s.shape[0]
    s = s.reshape(b, SSM_PAIRS, D_STATE, 2, SSM_HEAD_DIM)
    return s.transpose(0, 1, 3, 4, 2).reshape(b, SSM_HEADS, SSM_HEAD_DIM, D_STATE)


def _proj_ln_kernel(yp_ref, ys_ref, w_ref, x_ref, g_ref, b_ref, o_ref, *, split_blk, slabs):
    if slabs:
        y_p = jnp.concatenate([yp_ref[s] for s in range(yp_ref.shape[0])], axis=1)
    else:
        y_p = yp_ref[...]
    y = jnp.where(pl.program_id(0) < split_blk, y_p, ys_ref[...])
    yb = y.astype(bf16)
    for c in range(0, D_MODEL, COL_CHUNK):
        cs = slice(c, c + COL_CHUNK)
        o_ref[:, cs] = DEEPNORM_ALPHA * x_ref[:, cs] + jnp.dot(yb, w_ref[:, cs],
                                                               preferred_element_type=f32)
    o_ref[...] = _layer_norm_rows(o_ref[...], g_ref[...], b_ref[...])


def _proj_ln(y_p, y_s, w, x, g, b, interpret=False):
    n = x.shape[0]
    k = y_s.shape[1]
    tm = TOKEN_TILE
    slabs = y_p.ndim == 3
    rows_p = y_p.shape[1] if slabs else y_p.shape[0]
    assert rows_p % tm == 0 and y_s.shape[0] % tm == 0
    split_blk = rows_p // tm
    if slabs:
        p_spec = pl.BlockSpec((k // LANES, tm, LANES),
                              lambda i: (0, jnp.minimum(i, split_blk - 1), 0))
    else:
        p_spec = pl.BlockSpec((tm, k), lambda i: (jnp.minimum(i, split_blk - 1), 0))
    return pl.pallas_call(
        functools.partial(_proj_ln_kernel, split_blk=split_blk, slabs=slabs),
        out_shape=jax.ShapeDtypeStruct((n, D_MODEL), f32),
        grid=(n // tm,),
        in_specs=[p_spec,
                  pl.BlockSpec((tm, k), lambda i: (jnp.maximum(i - split_blk, 0), 0)),
                  pl.BlockSpec((k, D_MODEL), lambda i: (0, 0)),
                  pl.BlockSpec((tm, D_MODEL), lambda i: (i, 0)),
                  pl.BlockSpec((1, D_MODEL), lambda i: (0, 0)),
                  pl.BlockSpec((1, D_MODEL), lambda i: (0, 0))],
        out_specs=pl.BlockSpec((tm, D_MODEL), lambda i: (i, 0)),
        compiler_params=_params(("parallel",)),
        name="proj_postnorm",
        interpret=interpret,
    )(y_p, y_s, w, x, g, b)


def _router_kernel(x_ref, w_in_ref, b_ref, e_ref, w_ref, cnt_scr):
    @pl.when(pl.program_id(0) == 0)
    def _():
        cnt_scr[...] = jnp.zeros(cnt_scr.shape, f32)

    logits_tok = jnp.dot(x_ref[...], w_in_ref[...], preferred_element_type=f32,
                         precision=HIGHEST)
    logits = logits_tok.T[0:N_EXPERTS, :]
    s = _sigmoid(logits)
    sb = s + b_ref[...]
    srow = [s[i:i + 1, :] for i in range(N_EXPERTS)]
    brow = [sb[i:i + 1, :] for i in range(N_EXPERTS)]
    gscore = []
    for g in range(N_EXPERT_GROUPS):
        v = brow[g * EXPERTS_PER_GROUP:(g + 1) * EXPERTS_PER_GROUP]
        best = None
        for i in range(EXPERTS_PER_GROUP):
            for j in range(i + 1, EXPERTS_PER_GROUP):
                hi = jnp.maximum(v[i], v[j])
                lo_ = jnp.minimum(v[i], v[j])
                p = hi + lo_
                best = p if best is None else jnp.maximum(best, p)
        gscore.append(best)
    gi = jnp.zeros_like(gscore[0], dtype=jnp.int32)
    gbest = gscore[0]
    for g in range(1, N_EXPERT_GROUPS):
        upd = gscore[g] > gbest
        gi = jnp.where(upd, g, gi)
        gbest = jnp.where(upd, gscore[g], gbest)
    vb, vs = [], []
    for k in range(EXPERTS_PER_GROUP):
        tb, ts = brow[k], srow[k]
        for g in range(1, N_EXPERT_GROUPS):
            tb = jnp.where(gi == g, brow[g * EXPERTS_PER_GROUP + k], tb)
            ts = jnp.where(gi == g, srow[g * EXPERTS_PER_GROUP + k], ts)
        vb.append(tb)
        vs.append(ts)
    i1 = jnp.zeros_like(gi)
    b1, s1 = vb[0], vs[0]
    for k in range(1, EXPERTS_PER_GROUP):
        upd = vb[k] > b1
        i1 = jnp.where(upd, k, i1)
        b1 = jnp.where(upd, vb[k], b1)
        s1 = jnp.where(upd, vs[k], s1)
    i2 = jnp.full_like(gi, -1)
    b2 = jnp.full_like(b1, -jnp.inf)
    s2 = jnp.zeros_like(s1)
    for k in range(EXPERTS_PER_GROUP):
        upd = (i1 != k) & ((vb[k] > b2) | (i2 < 0))
        i2 = jnp.where(upd, k, i2)
        b2 = jnp.where(upd, vb[k], b2)
        s2 = jnp.where(upd, vs[k], s2)
    den = s1 + s2
    tm = gi.shape[1]
    e0 = gi * EXPERTS_PER_GROUP + i1
    e1 = gi * EXPERTS_PER_GROUP + i2
    eid = lax.broadcasted_iota(jnp.int32, (N_EXPERTS, tm), 0)
    hit = jnp.where((eid == e0) | (eid == e1), 1.0, 0.0)
    t_row = lax.broadcasted_iota(jnp.int32, (tm, tm), 0)
    t_col = lax.broadcasted_iota(jnp.int32, (tm, tm), 1)
    before = jnp.where(t_row < t_col, 1.0, 0.0).astype(bf16)
    rank_all = jnp.dot(hit.astype(bf16), before, preferred_element_type=f32) + cnt_scr[...]
    r0 = jnp.zeros_like(s1)
    r1 = jnp.zeros_like(s1)
    for k in range(N_EXPERTS):
        rk = rank_all[k:k + 1, :]
        r0 = jnp.where(e0 == k, rk, r0)
        r1 = jnp.where(e1 == k, rk, r1)
    cnt_scr[...] = cnt_scr[...] + jnp.sum(hit, axis=-1, keepdims=True)
    zi = jnp.zeros((SUBLANES - 4, tm), jnp.int32)
    zf = jnp.zeros((SUBLANES - 2, tm), f32)
    e_ref[...] = jnp.concatenate([e0, e1, r0.astype(jnp.int32), r1.astype(jnp.int32), zi], axis=0)
    w_ref[...] = jnp.concatenate([s1 / den, s2 / den, zf], axis=0)


def _router(x, w_pad, b, interpret=False):
    n = x.shape[0]
    tm = TOKEN_TILE
    return pl.pallas_call(
        _router_kernel,
        out_shape=(jax.ShapeDtypeStruct((SUBLANES, n), jnp.int32),
                   jax.ShapeDtypeStruct((SUBLANES, n), f32)),
        grid=(n // tm,),
        in_specs=[pl.BlockSpec((tm, D_MODEL), lambda i: (i, 0)),
                  pl.BlockSpec((D_MODEL, LANES), lambda i: (0, 0)),
                  pl.BlockSpec((N_EXPERTS, 1), lambda i: (0, 0))],
        out_specs=(pl.BlockSpec((SUBLANES, tm), lambda i: (0, i)),
                   pl.BlockSpec((SUBLANES, tm), lambda i: (0, i))),
        scratch_shapes=[pltpu.VMEM((N_EXPERTS, 1), f32)],
        compiler_params=_params(("arbitrary",)),
        name="moe_router",
        interpret=interpret,
    )(x, w_pad, b)


def _expert_kernel(blk_e_ref, n_used_ref, src_ref, dst_ref, x_hbm, wg_ref, wu_ref, wd_ref, y_hbm,
                   wg_b, wu_b, wd_b, h_scr, xbuf, obuf, gsem, ssem, *, trash_row0, n_trash_blk):
    i = pl.program_id(0)
    tm = EXPERT_TILE
    n_used = n_used_ref[0]
    slot = i % 2
    e = blk_e_ref[i]
    e_prev = blk_e_ref[jnp.maximum(i - 1, 0)]

    def start_gather(blk, sl):
        def body(r, c):
            pltpu.make_async_copy(x_hbm.at[pl.ds(src_ref[blk * tm + r], 1)],
                                  xbuf.at[sl, pl.ds(r, 1)], gsem.at[sl]).start()
            return c
        lax.fori_loop(0, tm, body, 0, unroll=8)

    def wait_gather(sl):
        for r in range(tm):
            pltpu.make_async_copy(x_hbm.at[pl.ds(0, 1)], xbuf.at[sl, pl.ds(r, 1)],
                                  gsem.at[sl]).wait()

    def start_scatter(blk, sl):
        def body(r, c):
            pltpu.make_async_copy(obuf.at[sl, pl.ds(r, 1)],
                                  y_hbm.at[pl.ds(dst_ref[blk * tm + r], 1)], ssem.at[sl]).start()
            return c
        lax.fori_loop(0, tm, body, 0, unroll=8)

    def wait_scatter(sl):
        for r in range(tm):
            pltpu.make_async_copy(obuf.at[sl, pl.ds(r, 1)], y_hbm.at[pl.ds(0, 1)],
                                  ssem.at[sl]).wait()

    @pl.when(i == 0)
    def _first():
        start_gather(0, 0)
        obuf[1] = jnp.zeros((tm, D_MODEL), f32)
        fills = [pltpu.make_async_copy(obuf.at[1], y_hbm.at[pl.ds(trash_row0 + t * tm, tm)],
                                       ssem.at[1]) for t in range(n_trash_blk)]
        for f in fills:
            f.start()
        for f in fills:
            f.wait()

    @pl.when((i == 0) | (e != e_prev))
    def _cast():
        for src, dst in ((wg_ref, wg_b), (wu_ref, wu_b), (wd_ref, wd_b)):
            for r0 in range(0, src.shape[2], CAST_ROWS):
                dst[r0:r0 + CAST_ROWS, :] = src[0, 0, r0:r0 + CAST_ROWS, :].astype(bf16)

    @pl.when(i < n_used)
    def _block():
        @pl.when(i + 1 < n_used)
        def _():
            start_gather(i + 1, 1 - slot)

        wait_gather(slot)

        @pl.when(i >= 2)
        def _():
            wait_scatter(slot)

        xb = xbuf[slot].astype(bf16)
        for c in range(0, D_EXPERT, COL_CHUNK):
            cs = slice(c, c + COL_CHUNK)
            hg = jnp.dot(xb, wg_b[:, cs], preferred_element_type=f32)
            hu = jnp.dot(xb, wu_b[:, cs], preferred_element_type=f32)
            h_scr[:, cs] = (hg * _sigmoid(hg) * hu).astype(bf16)
        hb = h_scr[...]
        for c in range(0, D_MODEL, COL_CHUNK):
            cs = slice(c, c + COL_CHUNK)
            obuf[slot, :, cs] = jnp.dot(hb, wd_b[:, cs], preferred_element_type=f32)
        start_scatter(i, slot)

        @pl.when(i == n_used - 1)
        def _drain():
            wait_scatter(slot)

            @pl.when(i >= 1)
            def _():
                wait_scatter(1 - slot)


def _experts(blk_e, n_used, src_row, dst_row, x, wg, wu, wd, layer, n_out_rows, interpret=False):
    tm = EXPERT_TILE
    n_blk = blk_e.shape[0]
    wspec = lambda i, be, nu, sr, ds: (layer, be[i], 0, 0)
    hbm = pl.BlockSpec(memory_space=pl.ANY)
    return pl.pallas_call(
        functools.partial(_expert_kernel, trash_row0=n_out_rows, n_trash_blk=N_EXPERTS),
        out_shape=jax.ShapeDtypeStruct((n_out_rows + N_EXPERTS * tm, D_MODEL), f32),
        grid_spec=pltpu.PrefetchScalarGridSpec(
            num_scalar_prefetch=4,
            grid=(n_blk,),
            in_specs=[hbm,
                      pl.BlockSpec((1, 1, D_MODEL, D_EXPERT), wspec),
                      pl.BlockSpec((1, 1, D_MODEL, D_EXPERT), wspec),
                      pl.BlockSpec((1, 1, D_EXPERT, D_MODEL), wspec)],
            out_specs=hbm,
            scratch_shapes=[pltpu.VMEM((D_MODEL, D_EXPERT), bf16),
                            pltpu.VMEM((D_MODEL, D_EXPERT), bf16),
                            pltpu.VMEM((D_EXPERT, D_MODEL), bf16),
                            pltpu.VMEM((tm, D_EXPERT), bf16),
                            pltpu.VMEM((2, tm, D_MODEL), f32),
                            pltpu.VMEM((2, tm, D_MODEL), f32),
                            pltpu.SemaphoreType.DMA((2,)),
                            pltpu.SemaphoreType.DMA((2,))]),
        compiler_params=pltpu.CompilerParams(dimension_semantics=("arbitrary",),
                                             vmem_limit_bytes=EXPERT_VMEM_LIMIT),
        name="moe_experts",
        interpret=interpret,
    )(blk_e, n_used, src_row, dst_row, x, wg, wu, wd)


def _combine_ln_kernel(x_ref, y0_ref, y1_ref, w_ref, g_ref, b_ref, *o_refs, split_blk):
    w = w_ref[...]
    v = DEEPNORM_ALPHA * x_ref[...] + (y0_ref[...] * w[:, 0:1] + y1_ref[...] * w[:, 1:2])
    res = _layer_norm_rows(v, g_ref[...], b_ref[...])
    if split_blk is None:
        o_refs[0][...] = res
    else:
        i = pl.program_id(0)

        @pl.when(i < split_blk)
        def _():
            o_refs[0][...] = res

        @pl.when(i >= split_blk)
        def _():
            o_refs[1][...] = res


def _combine_ln(x, y, w_col, g, b, split=None, interpret=False):
    n = x.shape[0]
    tm = TOKEN_TILE
    n_tiles = n // tm
    row = lambda i: (i, 0)
    const = lambda i: (0, 0)
    if split is None:
        split_blk = None
        out_shape = jax.ShapeDtypeStruct((n, D_MODEL), f32)
        out_specs = pl.BlockSpec((tm, D_MODEL), row)
    else:
        assert split % tm == 0 and (n - split) % tm == 0
        split_blk = split // tm
        out_shape = (jax.ShapeDtypeStruct((split, D_MODEL), f32),
                     jax.ShapeDtypeStruct((n - split, D_MODEL), f32))
        out_specs = (pl.BlockSpec((tm, D_MODEL), lambda i: (jnp.minimum(i, split_blk - 1), 0)),
                     pl.BlockSpec((tm, D_MODEL), lambda i: (jnp.maximum(i - split_blk, 0), 0)))
    return pl.pallas_call(
        functools.partial(_combine_ln_kernel, split_blk=split_blk),
        out_shape=out_shape,
        grid=(n // tm,),
        in_specs=[pl.BlockSpec((tm, D_MODEL), row),
                  pl.BlockSpec((tm, D_MODEL), row),
                  pl.BlockSpec((tm, D_MODEL), lambda i: (i + n_tiles, 0)),
                  pl.BlockSpec((tm, SUBLANES), row),
                  pl.BlockSpec((1, D_MODEL), const),
                  pl.BlockSpec((1, D_MODEL), const)],
        out_specs=out_specs,
        compiler_params=_params(("arbitrary",)),
        name="moe_combine_postnorm",
        interpret=interpret,
    )(x, y, y, w_col, g, b)


def _rows(x, idx):
    return x.at[idx].get(mode="promise_in_bounds")


def _moe(x, router_w_pad, router_b, wg, wu, wd, layer, g, b, split=None, interpret=False):
    n = x.shape[0]
    route, e_w = _router(x, router_w_pad, router_b, interpret)
    e_idx = route[0:MOE_TOP_K]
    rank = route[MOE_TOP_K:2 * MOE_TOP_K]
    n_assign = n * MOE_TOP_K
    experts = jnp.arange(N_EXPERTS, dtype=jnp.int32)
    onehot = e_idx[:, :, None] == experts
    counts = jnp.sum(onehot, axis=(0, 1), dtype=jnp.int32)
    tm = EXPERT_TILE
    padded = (counts + tm - 1) // tm * tm
    pad_end = jnp.cumsum(padded)
    pad_start = pad_end - padded
    dest = jnp.sum(jnp.where(onehot, pad_start, 0), axis=-1) + rank
    n_blk = (n_assign + N_EXPERTS * (tm - 1)) // tm + 1
    slot_id = jnp.arange(n_assign, dtype=jnp.int32)
    held = jnp.full((n_blk * tm,), -1, jnp.int32).at[dest.reshape(-1)].set(
        slot_id, unique_indices=True, mode="promise_in_bounds")
    blk_start = jnp.arange(n_blk, dtype=jnp.int32) * tm
    blk_e = jnp.minimum(jnp.sum(pad_end[None, :] <= blk_start[:, None], axis=1, dtype=jnp.int32),
                        N_EXPERTS - 1)
    n_used = (pad_end[-1:] // tm).astype(jnp.int32)
    row = jnp.arange(n_blk * tm, dtype=jnp.int32)
    row_e = jnp.repeat(blk_e, tm)
    trash = n_assign + row_e * tm + jnp.clip(row - (pad_start + counts)[row_e], 0, tm - 1)
    src_row = jnp.where(held < 0, 0, jnp.where(held >= n, held - n, held))
    dst_row = jnp.where(held < 0, trash, held)
    y = _experts(blk_e, n_used, src_row, dst_row, x, wg, wu, wd, layer, n_assign, interpret)
    return _combine_ln(x, y, e_w.T, g, b, split, interpret)


def _rope_slab(t, cos, sin_signed, lo_half):
    swapped = jnp.where(lo_half, pltpu.roll(t, LANES - HEAD_DIM // 2, 1),
                        pltpu.roll(t, HEAD_DIM // 2, 1))
    return t * cos + swapped * sin_signed


def _qkv_kernel(x_ref, wq_ref, wkt_ref, wvt_ref, cos_ref, sin_ref, cost_ref, sint_ref,
                q_ref, kt_ref, vt_ref, ks_ref, vs_ref, *, prompt_blk):
    i = pl.program_id(0)
    tm = x_ref.shape[0]
    xb = x_ref[...].astype(bf16)
    cos = cos_ref[...]
    sin = sin_ref[...]
    lane = lax.broadcasted_iota(jnp.int32, (1, LANES), 1)
    lo_half = (lane % HEAD_DIM) < (HEAD_DIM // 2)
    for c in range(0, D_MODEL, COL_CHUNK):
        t = jnp.dot(xb, wq_ref[:, c:c + COL_CHUNK], preferred_element_type=f32)
        for s in range(0, COL_CHUNK, LANES):
            q_ref[(c + s) // LANES] = _rope_slab(t[:, s:s + LANES], cos, sin, lo_half)

    nt_dims = (((1,), (1,)), ((), ()))
    half = HEAD_DIM // 2
    hpc = COL_CHUNK // HEAD_DIM
    cos_t = cost_ref[...][None]
    sin_t = sint_ref[...][None]
    for r0 in range(0, D_MODEL, COL_CHUNK):
        rs = slice(r0, r0 + COL_CHUNK)
        hs = slice(r0 // HEAD_DIM, r0 // HEAD_DIM + hpc)
        kt = lax.dot_general(wkt_ref[rs, :], xb, nt_dims,
                             preferred_element_type=f32).reshape(hpc, HEAD_DIM, tm)
        sw = jnp.concatenate([kt[:, half:, :], kt[:, :half, :]], axis=1)
        kr = kt * cos_t + sw * sin_t
        vr = lax.dot_general(wvt_ref[rs, :], xb, nt_dims,
                             preferred_element_type=f32).reshape(hpc, HEAD_DIM, tm)

        @pl.when(i < prompt_blk)
        def _(kr=kr, vr=vr, hs=hs):
            kt_ref[0, hs] = kr
            vt_ref[0, hs] = vr

        @pl.when(i >= prompt_blk)
        def _(kr=kr, vr=vr, rs=rs):
            ks_ref[:, rs] = kr.reshape(COL_CHUNK, tm).T
            vs_ref[:, rs] = vr.reshape(COL_CHUNK, tm).T


def _qkv(x, wq, wkt, wvt, cos, sin, cos_t, sin_t, n_seq, seq_len, interpret=False):
    n = x.shape[0]
    tm = TOKEN_TILE
    n_p = n_seq * seq_len
    assert n_p % tm == 0 and seq_len % tm == 0 and (n - n_p) % tm == 0
    prompt_blk = n_p // tm
    tps = seq_len // tm
    row = lambda i: (i, 0)
    const = lambda i: (0, 0)

    def head_major(i):
        j = jnp.minimum(i, prompt_blk - 1)
        return (j // tps, 0, 0, j % tps)

    sample_rows = lambda i: (jnp.maximum(i - prompt_blk, 0), 0)
    kv_t = jax.ShapeDtypeStruct((n_seq, N_HEADS, HEAD_DIM, seq_len), f32)
    kv_s = jax.ShapeDtypeStruct((n - n_p, D_MODEL), f32)
    return pl.pallas_call(
        functools.partial(_qkv_kernel, prompt_blk=prompt_blk),
        out_shape=(jax.ShapeDtypeStruct((HEAD_PAIRS, n, LANES), f32), kv_t, kv_t, kv_s, kv_s),
        grid=(n // tm,),
        in_specs=[pl.BlockSpec((tm, D_MODEL), row),
                  pl.BlockSpec((D_MODEL, D_MODEL), const),
                  pl.BlockSpec((D_MODEL, D_MODEL), const),
                  pl.BlockSpec((D_MODEL, D_MODEL), const),
                  pl.BlockSpec((tm, LANES), row),
                  pl.BlockSpec((tm, LANES), row),
                  pl.BlockSpec((HEAD_DIM, tm), lambda i: (0, i)),
                  pl.BlockSpec((HEAD_DIM, tm), lambda i: (0, i))],
        out_specs=(pl.BlockSpec((HEAD_PAIRS, tm, LANES), lambda i: (0, i, 0)),
                   pl.BlockSpec((1, N_HEADS, HEAD_DIM, tm), head_major),
                   pl.BlockSpec((1, N_HEADS, HEAD_DIM, tm), head_major),
                   pl.BlockSpec((tm, D_MODEL), sample_rows),
                   pl.BlockSpec((tm, D_MODEL), sample_rows)),
        compiler_params=_params(("arbitrary",)),
        name="qkv_rope",
        interpret=interpret,
    )(x, wq, wkt, wvt, cos, sin, cos_t, sin_t)


def _rope_tables(pos):
    half = HEAD_DIM // 2
    inv = ROPE_THETA ** (-jnp.arange(half, dtype=f32) / half)
    ang = pos.astype(f32)[:, None] * inv
    cos, sin = jnp.cos(ang), jnp.sin(ang)
    cos_h = jnp.concatenate([cos, cos], axis=1)
    sin_h = jnp.concatenate([-sin, sin], axis=1)
    rep = (1, LANES // HEAD_DIM)
    return jnp.tile(cos_h, rep), jnp.tile(sin_h, rep), cos_h.T, sin_h.T


def _moba_prompt_kernel(q_ref, kt_ref, vt_ref, o_ref, kaug, vaug, kmcols, *, nb):
    BLK = MOBA_BLOCK
    lane = lax.broadcasted_iota(jnp.int32, (1, LANES), 1)
    lo = lane < HEAD_DIM
    nt_dims = (((1,), (1,)), ((), ()))

    zcol = jnp.zeros((HEAD_DIM, 1), f32)
    kmc = jnp.zeros((LANES, LANES), f32)
    for j in range(nb):
        cs = slice(j * BLK, (j + 1) * BLK)
        k0 = kt_ref[0, 0, :, cs]
        k1 = kt_ref[0, 1, :, cs]
        ind = jnp.where(lax.broadcasted_iota(jnp.int32, (HEAD_DIM, BLK), 0) == j,
                        1.0, 0.0).astype(bf16)
        ones = jnp.ones((HEAD_DIM, BLK), bf16)
        kaug[0, :, cs] = jnp.concatenate([k0.astype(bf16), ind], axis=0)
        kaug[1, :, cs] = jnp.concatenate([ind, k1.astype(bf16)], axis=0)
        vaug[0, :, cs] = jnp.concatenate([vt_ref[0, 0, :, cs].astype(bf16), ones], axis=0)
        vaug[1, :, cs] = jnp.concatenate([ones, vt_ref[0, 1, :, cs].astype(bf16)], axis=0)
        m0 = jnp.concatenate([jnp.mean(k0, axis=1, keepdims=True), zcol], axis=0)
        m1 = jnp.concatenate([zcol, jnp.mean(k1, axis=1, keepdims=True)], axis=0)
        kmc = jnp.where(lane == HEAD_DIM + j, m0, jnp.where(lane == j, m1, kmc))
    kmcols[...] = kmc

    row = lax.broadcasted_iota(jnp.int32, (BLK, BLK), 0)
    col = lax.broadcasted_iota(jnp.int32, (BLK, BLK), 1)
    diag_mask = col <= row

    def attend(iv):
        qp = q_ref[0, iv * BLK:(iv + 1) * BLK, :]
        qs = qp * (HEAD_DIM ** -0.5)
        q_own = [jnp.where(lo, qs, 0.0), jnp.where(lo, 0.0, qs)]
        if iv > MOBA_TOPK:
            gate = jnp.dot(qp, kmcols[...], preferred_element_type=f32, precision=HIGHEST)
            blkid = lane & (HEAD_DIM - 1)
            past = blkid < iv
            gate = jnp.where(past, gate, -jnp.inf)
            cnt = jnp.zeros((BLK, LANES), f32)
            for j in range(iv):
                c0 = jnp.broadcast_to(gate[:, HEAD_DIM + j:HEAD_DIM + j + 1], (BLK, LANES))
                c1 = jnp.broadcast_to(gate[:, j:j + 1], (BLK, LANES))
                cj = jnp.where(lo, c1, c0)
                ahead = (cj > gate) | ((cj == gate) & (blkid > j))
                cnt = cnt + jnp.where(ahead, 1.0, 0.0)
            bias = jnp.where((cnt < MOBA_TOPK) & past, 0.0, MASK_BIAS)
            q_aug = [jnp.where(lo, qs, bias), jnp.where(lo, bias, qs)]
        else:
            q_aug = q_own
        outs = []
        own = slice(iv * BLK, (iv + 1) * BLK)
        for hh in range(2):
            s_own = jnp.dot(q_own[hh].astype(bf16), kaug[hh, :, own], preferred_element_type=f32)
            s_own = jnp.where(diag_mask, s_own, -jnp.inf)
            m = jnp.max(s_own, axis=-1, keepdims=True)
            if iv > 0:
                s_past = jnp.dot(q_aug[hh].astype(bf16), kaug[hh, :, 0:iv * BLK],
                                 preferred_element_type=f32)
                m = jnp.maximum(m, jnp.max(s_past, axis=-1, keepdims=True))
                acc = lax.dot_general(jnp.exp(s_past - m).astype(bf16), vaug[hh, :, 0:iv * BLK],
                                      nt_dims, preferred_element_type=f32)
            p_own = jnp.exp(s_own - m).astype(bf16)
            acc_own = lax.dot_general(p_own, vaug[hh, :, own], nt_dims,
                                      preferred_element_type=f32)
            acc = acc + acc_own if iv > 0 else acc_own
            outs.append(acc / pltpu.roll(acc, HEAD_DIM, 1))
        o_ref[0, iv * BLK:(iv + 1) * BLK, :] = jnp.where(lo, outs[0], outs[1])

    for iv in range(nb):
        attend(iv)


def _moba_prompt(q_hp, k_t, v_t, interpret=False):
    BLK = MOBA_BLOCK
    n_seq, _, _, seq_len = k_t.shape
    nb = seq_len // BLK
    assert nb <= HEAD_DIM and seq_len % BLK == 0
    qo_map = lambda b, hp: (hp, b, 0)
    kv_map = lambda b, hp: (b, hp, 0, 0)
    return pl.pallas_call(
        functools.partial(_moba_prompt_kernel, nb=nb),
        out_shape=jax.ShapeDtypeStruct((HEAD_PAIRS, n_seq * seq_len, LANES), f32),
        grid=(n_seq, HEAD_PAIRS),
        in_specs=[pl.BlockSpec((1, seq_len, LANES), qo_map),
                  pl.BlockSpec((1, 2, HEAD_DIM, seq_len), kv_map),
                  pl.BlockSpec((1, 2, HEAD_DIM, seq_len), kv_map)],
        out_specs=pl.BlockSpec((1, seq_len, LANES), qo_map),
        scratch_shapes=[pltpu.VMEM((2, LANES, seq_len), bf16),
                        pltpu.VMEM((2, LANES, seq_len), bf16),
                        pltpu.VMEM((LANES, LANES), f32)],
        compiler_params=_params(("parallel", "parallel")),
        name="moba_prompt_attn",
        interpret=interpret,
    )(q_hp, k_t, v_t)


def _page_sum_kernel(pt_ref, *refs):
    pages, o_ref = refs[:PAGES_PER_STEP], refs[PAGES_PER_STEP]
    ppb = MOBA_BLOCK // PAGE_SIZE
    bps = PAGES_PER_STEP // ppb
    s = pl.program_id(1)
    lane = lax.broadcasted_iota(jnp.int32, (1, LANES), 1)

    @pl.when(s == 0)
    def _():
        o_ref[...] = jnp.zeros(o_ref.shape, f32)

    for h in range(N_HEADS):
        acc = o_ref[0, h]
        for blk in range(bps):
            t = pages[blk * ppb][0, h]
            for r in range(1, ppb):
                t = t + pages[blk * ppb + r][0, h]
            col = jnp.sum(t, axis=-1, keepdims=True)
            acc = jnp.where(lane == s * bps + blk, col, acc)
        o_ref[0, h] = acc


def _page_sums(page_table, cache_kt, n_blocks, interpret=False):
    bsz, n_pages = page_table.shape
    ppb = MOBA_BLOCK // PAGE_SIZE
    assert n_blocks <= LANES and (n_blocks * ppb) % PAGES_PER_STEP == 0
    steps = n_blocks * ppb // PAGES_PER_STEP
    pt_flat = page_table.reshape(-1)

    def page_map(r):
        return lambda b, s, pt: (pt[b * n_pages + s * PAGES_PER_STEP + r], 0, 0, 0)

    return pl.pallas_call(
        _page_sum_kernel,
        out_shape=jax.ShapeDtypeStruct((bsz, N_HEADS, HEAD_DIM, LANES), f32),
        grid_spec=pltpu.PrefetchScalarGridSpec(
            num_scalar_prefetch=1,
            grid=(bsz, steps),
            in_specs=[pl.BlockSpec((1, N_HEADS, HEAD_DIM, PAGE_SIZE), page_map(r))
                      for r in range(PAGES_PER_STEP)],
            out_specs=pl.BlockSpec((1, N_HEADS, HEAD_DIM, LANES), lambda b, s, pt: (b, 0, 0, 0))),
        compiler_params=_params(("parallel", "arbitrary")),
        name="moba_page_sums",
        interpret=interpret,
    )(pt_flat, *([cache_kt] * PAGES_PER_STEP))


def _sample_topk_kernel(q_ref, ks_ref, idx_ref, *, n_blocks, topk):
    lane = lax.broadcasted_iota(jnp.int32, (1, LANES), 1)
    lane_f = lane.astype(f32)
    for h in range(N_HEADS):
        g = jnp.dot(q_ref[0, h], ks_ref[0, h], preferred_element_type=f32,
                    precision=HIGHEST) * (1.0 / MOBA_BLOCK)
        g = jnp.where(lane < n_blocks, g, -jnp.inf)
        out = jnp.zeros(g.shape, f32)
        for t in range(topk):
            m = jnp.max(g, axis=-1, keepdims=True)
            idx = jnp.min(jnp.where(g == m, lane_f, float(LANES)), axis=-1, keepdims=True)
            out = jnp.where(lane == t, idx, out)
            g = jnp.where(lane_f == idx, -jnp.inf, g)
        idx_ref[0, h] = out.astype(jnp.int32)


def _sample_topk(q4, ksum_t, n_blocks, topk, interpret=False):
    bsz, _, nq, _ = q4.shape
    return pl.pallas_call(
        functools.partial(_sample_topk_kernel, n_blocks=n_blocks, topk=topk),
        out_shape=jax.ShapeDtypeStruct((bsz, N_HEADS, nq, LANES), jnp.int32),
        grid=(bsz,),
        in_specs=[pl.BlockSpec((1, N_HEADS, nq, HEAD_DIM), lambda b: (b, 0, 0, 0)),
                  pl.BlockSpec((1, N_HEADS, HEAD_DIM, LANES), lambda b: (b, 0, 0, 0))],
        out_specs=pl.BlockSpec((1, N_HEADS, nq, LANES), lambda b: (b, 0, 0, 0)),
        compiler_params=_params(("parallel",)),
        name="moba_sample_topk",
        interpret=interpret,
    )(q4, ksum_t)


def _sample_attn_kernel(phys_ref, q_ref, kn_ref, vn_ref, ck_hbm, cv_hbm, o_ref,
                        kbuf, vbuf, sem, *, nq, n_sel):
    g = pl.program_id(0)
    ng = pl.num_programs(0)
    slot = g % 2
    n_pg = nq * n_sel

    def page_copies(gi, sl, r):
        page = phys_ref[gi * n_pg + r]
        head = gi % N_HEADS
        return (pltpu.make_async_copy(ck_hbm.at[page, head], kbuf.at[sl, r], sem.at[0, sl]),
                pltpu.make_async_copy(cv_hbm.at[page, head], vbuf.at[sl, r], sem.at[1, sl]))

    def fetch(gi, sl):
        for r in range(n_pg):
            ck, cv = page_copies(gi, sl, r)
            ck.start()
            cv.start()

    @pl.when(g == 0)
    def _():
        fetch(0, 0)

    @pl.when(g + 1 < ng)
    def _():
        fetch(g + 1, 1 - slot)

    for r in range(n_pg):
        ck, cv = page_copies(g, slot, r)
        ck.wait()
        cv.wait()

    lane = lax.broadcasted_iota(jnp.int32, (1, LANES), 1)
    q_t = q_ref[0, 0] * (HEAD_DIM ** -0.5)
    kn = kn_ref[0, 0]
    vn = vn_ref[0, 0]
    out = jnp.zeros((HEAD_DIM, LANES), f32)
    for qi in range(nq):
        qb = jnp.broadcast_to(q_t[:, qi:qi + 1], (HEAD_DIM, LANES))
        s_own = jnp.sum(kn * qb, axis=0, keepdims=True)
        s_own = jnp.where(lane <= qi, s_own, -jnp.inf)
        s_sel = [jnp.sum(kbuf[slot, qi * n_sel + r] * qb, axis=0, keepdims=True)
                 for r in range(n_sel)]
        m_lane = s_own
        for s in s_sel:
            m_lane = jnp.maximum(m_lane, s)
        m = jnp.max(m_lane, axis=-1, keepdims=True)
        p_own = jnp.exp(s_own - m)
        p_sum = p_own
        acc = vn * p_own
        for r, s in enumerate(s_sel):
            p = jnp.exp(s - m)
            p_sum = p_sum + p
            acc = acc + vbuf[slot, qi * n_sel + r] * p
        den = jnp.sum(p_sum, axis=-1, keepdims=True)
        o_col = jnp.sum(acc, axis=-1, keepdims=True) / den
        out = jnp.where(lane == qi, o_col, out)
    o_ref[0, 0] = out


def _sample_attn(phys, q_t, kn_t, vn_t, cache_kt, cache_vt, nq, n_sel, interpret=False):
    bsz = q_t.shape[0]
    small = pl.BlockSpec((1, 1, HEAD_DIM, LANES),
                         lambda g, ph: (g // N_HEADS, g % N_HEADS, 0, 0))
    hbm = pl.BlockSpec(memory_space=pl.ANY)
    return pl.pallas_call(
        functools.partial(_sample_attn_kernel, nq=nq, n_sel=n_sel),
        out_shape=jax.ShapeDtypeStruct((bsz, N_HEADS, HEAD_DIM, LANES), f32),
        grid_spec=pltpu.PrefetchScalarGridSpec(
            num_scalar_prefetch=1,
            grid=(bsz * N_HEADS,),
            in_specs=[small, small, small, hbm, hbm],
            out_specs=small,
            scratch_shapes=[pltpu.VMEM((2, nq * n_sel, HEAD_DIM, PAGE_SIZE), f32),
                            pltpu.VMEM((2, nq * n_sel, HEAD_DIM, PAGE_SIZE), f32),
                            pltpu.SemaphoreType.DMA((2, 2))]),
        compiler_params=_params(("arbitrary",)),
        name="moba_sample_attn",
        interpret=interpret,
    )(phys, q_t, kn_t, vn_t, cache_kt, cache_vt)


def _heads_first(t, bsz, seq_len):
    return t.reshape(bsz, seq_len, N_HEADS, HEAD_DIM).transpose(0, 2, 1, 3)


def _forward(x_prompt, x_sample, state_ssm, state_conv, cache_k, cache_v, page_table,
             ln_gain, ln_bias, ssm_w_in, ssm_w_conv, ssm_b_conv, ssm_dt_bias, ssm_a_log,
             ssm_d, ssm_norm_w, ssm_w_out, attn_w_q, attn_w_kv, attn_w_o,
             router_w, router_b, moe_w_gate, moe_w_up, moe_w_down, interpret=False):
    bp, lp, _ = x_prompt.shape
    bs, ls, _ = x_sample.shape
    n_p, n_s = bp * lp, bs * ls
    n = n_p + n_s
    n_pages = page_table.shape[1]
    past_len = n_pages * PAGE_SIZE
    assert n % TOKEN_TILE == 0 and lp % MOBA_BLOCK == 0 and lp % SSD_CHUNK == 0
    assert CONV_W - 1 <= ls <= SUBLANES and n_p % ls == 0
    assert past_len % MOBA_BLOCK == 0 and past_len // MOBA_BLOCK >= MOBA_TOPK
    assert ls <= MOBA_BLOCK - past_len % MOBA_BLOCK

    h = jnp.concatenate([x_prompt.reshape(n_p, D_MODEL), x_sample.reshape(n_s, D_MODEL)], axis=0)
    router_wp = jnp.pad(router_w, ((0, 0), (0, LANES - N_EXPERTS)))
    router_bc = router_b.reshape(N_EXPERTS, 1)

    def vec(v):
        return v.reshape(1, -1)

    l = 0
    w_in = ssm_w_in[l]
    wz = w_in[:, :D_INNER].astype(bf16)
    wx = w_in[:, D_INNER:D_INNER + CONV_DIM].astype(bf16)
    wdt = jnp.pad(w_in[:, D_INNER + CONV_DIM:], ((0, 0), (0, LANES - SSM_HEADS)))
    z, xbc, dt = _in_proj(h, wz, wx, wdt, interpret)

    pad_h = (0, LANES - SSM_HEADS)
    wconv = jnp.pad(ssm_w_conv[l], ((0, SUBLANES - CONV_W), (0, 0)))
    ssd_w = (wconv, vec(ssm_b_conv[l]), vec(jnp.pad(ssm_dt_bias[l], pad_h)),
             vec(jnp.pad(ssm_a_log[l], pad_h)), vec(jnp.repeat(ssm_d[l], SSM_HEAD_DIM)),
             vec(ssm_norm_w[l]))
    y_p, st_p = _ssd(z, xbc, dt, None, None, *ssd_w, n_seq=bp, seq_len=lp, row_off=0,
                     interpret=interpret)
    conv0 = jnp.pad(state_conv[l], ((0, 0), (SUBLANES - (CONV_W - 1), 0), (0, 0)))
    y_s, st_s = _ssd(z, xbc, dt, conv0, _state_to_pairs(state_ssm[l]), *ssd_w,
                     n_seq=bs, seq_len=ls, row_off=n_p, interpret=interpret)
    ssm_prompt = _pairs_to_state(st_p)[None]
    ssm_sample = _pairs_to_state(st_s)[None]
    tail = jnp.arange(-(CONV_W - 1), 0, dtype=jnp.int32)
    rows_p = ((jnp.arange(bp, dtype=jnp.int32) + 1) * lp)[:, None] + tail
    rows_s = (n_p + (jnp.arange(bs, dtype=jnp.int32) + 1) * ls)[:, None] + tail
    conv_prompt = _rows(xbc, rows_p.reshape(-1)).reshape(1, bp, CONV_W - 1, CONV_DIM)
    conv_sample = _rows(xbc, rows_s.reshape(-1)).reshape(1, bs, CONV_W - 1, CONV_DIM)

    h = _proj_ln(y_p, y_s, ssm_w_out[l].astype(bf16), h, vec(ln_gain[l, 0]), vec(ln_bias[l, 0]),
                 interpret)
    h = _moe(h, router_wp, router_bc, moe_w_gate, moe_w_up, moe_w_down, l,
             vec(ln_gain[l, 1]), vec(ln_bias[l, 1]), interpret=interpret)

    l = 1
    pos = jnp.concatenate([jnp.tile(jnp.arange(lp), bp), jnp.tile(past_len + jnp.arange(ls), bs)])
    cos_tok, sin_tok, cos_dim, sin_dim = _rope_tables(pos)
    hd_all = N_HEADS * HEAD_DIM
    q_hp, k_t, v_t, k_s, v_s = _qkv(h, attn_w_q[0].astype(bf16),
                                    attn_w_kv[:, :hd_all].T.astype(bf16),
                                    attn_w_kv[:, hd_all:].T.astype(bf16),
                                    cos_tok, sin_tok, cos_dim, sin_dim, bp, lp, interpret)
    k_prompt = jnp.swapaxes(k_t, 2, 3)
    v_prompt = jnp.swapaxes(v_t, 2, 3)
    k_sample = _heads_first(k_s, bs, ls)
    v_sample = _heads_first(v_s, bs, ls)
    q_sample = (q_hp[:, n_p:, :].reshape(HEAD_PAIRS, bs, ls, 2, HEAD_DIM)
                .transpose(1, 0, 3, 2, 4).reshape(bs, N_HEADS, ls, HEAD_DIM))

    o_p = _moba_prompt(q_hp, k_t, v_t, interpret)

    n_full = past_len // MOBA_BLOCK
    ppb = MOBA_BLOCK // PAGE_SIZE
    cache_kt = jnp.swapaxes(cache_k, 2, 3)
    cache_vt = jnp.swapaxes(cache_v, 2, 3)
    ksum_t = _page_sums(page_table, cache_kt, n_full, interpret)
    idx = _sample_topk(q_sample, ksum_t, n_full, MOBA_TOPK, interpret)[..., :MOBA_TOPK]
    lpage = idx[..., None] * ppb + jnp.arange(ppb, dtype=jnp.int32)
    phys = page_table[jnp.arange(bs)[:, None, None, None, None], lpage]

    def lanes_last(t):
        return jnp.pad(jnp.swapaxes(t, 2, 3), ((0, 0), (0, 0), (0, 0), (0, LANES - ls)))

    o_st = _sample_attn(phys.reshape(-1).astype(jnp.int32), lanes_last(q_sample),
                        lanes_last(k_sample), lanes_last(v_sample), cache_kt, cache_vt,
                        ls, MOBA_TOPK * ppb, interpret)
    o_s = o_st[..., :ls].transpose(0, 3, 1, 2).reshape(n_s, D_MODEL)

    h = _proj_ln(o_p, o_s, attn_w_o[0].astype(bf16), h, vec(ln_gain[l, 0]), vec(ln_bias[l, 0]),
                 interpret)
    h_p, h_s = _moe(h, router_wp, router_bc, moe_w_gate, moe_w_up, moe_w_down, l,
                    vec(ln_gain[l, 1]), vec(ln_bias[l, 1]), split=n_p, interpret=interpret)

    y_prompt = h_p.reshape(bp, lp, D_MODEL)
    y_sample = h_s.reshape(bs, ls, D_MODEL)
    return (y_prompt, y_sample, ssm_prompt, conv_prompt, k_prompt, v_prompt,
            ssm_sample, conv_sample, k_sample, v_sample)


def kernel(x_prompt, x_sample, state_ssm, state_conv, cache_k, cache_v, page_table, ln_gain, ln_bias, ssm_w_in, ssm_w_conv, ssm_b_conv, ssm_dt_bias, ssm_a_log, ssm_d, ssm_norm_w, ssm_w_out, attn_w_q, attn_w_kv, attn_w_o, router_w, router_b, moe_w_gate, moe_w_up, moe_w_down):
    return _forward(x_prompt, x_sample, state_ssm, state_conv, cache_k, cache_v, page_table,
                    ln_gain, ln_bias, ssm_w_in, ssm_w_conv, ssm_b_conv, ssm_dt_bias, ssm_a_log,
                    ssm_d, ssm_norm_w, ssm_w_out, attn_w_q, attn_w_kv, attn_w_o,
                    router_w, router_b, moe_w_gate, moe_w_up, moe_w_down)
```

```python
import functools
import math

import jax
import jax.numpy as jnp
from jax import lax
from jax.experimental import pallas as pl
from jax.experimental.pallas import tpu as pltpu

f32 = jnp.float32
bf16 = jnp.bfloat16
HIGHEST = lax.Precision.HIGHEST

D_MODEL = 1024
DEPTH = 2
PAGE_SIZE = 128
N_A_LAYERS = 1
SSM_HEAD_DIM = 64
D_INNER = 2048
SSM_HEADS = D_INNER // SSM_HEAD_DIM
SSM_GROUPS = 4
D_STATE = 128
CONV_W = 4
CONV_DIM = D_INNER + 2 * SSM_GROUPS * D_STATE
SSD_CHUNK = 128
N_HEADS = 16
HEAD_DIM = 64
MOBA_BLOCK = 256
MOBA_TOPK = 3
ROPE_THETA = 10000.0
N_EXPERTS = 16
N_EXPERT_GROUPS = 4
EXPERTS_PER_GROUP = 4
MOE_TOP_K = 2
D_EXPERT = 1024
DEEPNORM_ALPHA = (2.0 * DEPTH) ** 0.25
LN_EPS = 1e-5
RMS_EPS = 1e-6
MASK_BIAS = -1e30

LANES = 128
SUBLANES = 8
VMEM_LIMIT = 48 * 1024 * 1024

TOKEN_TILE = 256
EXPERT_TILE = 256
CAST_ROWS = 256
EXPERT_VMEM_LIMIT = 56 * 1024 * 1024
COL_CHUNK = 512
PAGES_PER_STEP = 16
HEAD_PAIRS = N_HEADS // 2
SSM_PAIRS = SSM_HEADS // 2


def _params(sem):
    return pltpu.CompilerParams(dimension_semantics=sem, vmem_limit_bytes=VMEM_LIMIT)


def _sigmoid(x):
    return 1.0 / (1.0 + jnp.exp(-x))


def _softplus(x):
    u = jnp.exp(-jnp.abs(x))
    w = 1.0 + u
    log1p_u = jnp.where(w == 1.0, u, jnp.log(w) * (u / (w - 1.0)))
    return jnp.maximum(x, 0.0) + log1p_u


def _layer_norm_rows(v, g, b):
    mu = jnp.mean(v, axis=-1, keepdims=True)
    d = v - mu
    var = jnp.mean(d * d, axis=-1, keepdims=True)
    return d * lax.rsqrt(var + LN_EPS) * g + b


def _in_proj_kernel(x_ref, wz_ref, wx_ref, wdt_ref, z_ref, xbc_ref, dt_ref):
    x = x_ref[...]
    xb = x.astype(bf16)
    for c in range(0, D_INNER, COL_CHUNK):
        z_ref[:, c:c + COL_CHUNK] = jnp.dot(xb, wz_ref[:, c:c + COL_CHUNK],
                                            preferred_element_type=f32)
    for c in range(0, CONV_DIM, COL_CHUNK):
        xbc_ref[:, c:c + COL_CHUNK] = jnp.dot(xb, wx_ref[:, c:c + COL_CHUNK],
                                              preferred_element_type=f32)
    dt_ref[...] = jnp.dot(x, wdt_ref[...], preferred_element_type=f32, precision=HIGHEST)


def _in_proj(x, wz, wx, wdt, interpret=False):
    n = x.shape[0]
    tm = TOKEN_TILE
    return pl.pallas_call(
        _in_proj_kernel,
        out_shape=(jax.ShapeDtypeStruct((n, D_INNER), f32),
                   jax.ShapeDtypeStruct((n, CONV_DIM), f32),
                   jax.ShapeDtypeStruct((n, LANES), f32)),
        grid=(n // tm,),
        in_specs=[pl.BlockSpec((tm, D_MODEL), lambda i: (i, 0)),
                  pl.BlockSpec((D_MODEL, D_INNER), lambda i: (0, 0)),
                  pl.BlockSpec((D_MODEL, CONV_DIM), lambda i: (0, 0)),
                  pl.BlockSpec((D_MODEL, LANES), lambda i: (0, 0))],
        out_specs=(pl.BlockSpec((tm, D_INNER), lambda i: (i, 0)),
                   pl.BlockSpec((tm, CONV_DIM), lambda i: (i, 0)),
                   pl.BlockSpec((tm, LANES), lambda i: (i, 0))),
        compiler_params=_params(("parallel",)),
        name="ssm_in_proj",
        interpret=interpret,
    )(x, wz, wx, wdt)


def _ssd_kernel(*refs, c_in, nc, has_init):
    T = SSD_CHUNK
    if has_init:
        (z_ref, xbc_ref, dt_ref, conv0_ref, st0_ref, wconv_ref, bconv_ref, dtb_ref, alog_ref,
         d_ref, nw_ref, y_ref, st_out_ref, xbuf, xc, ybuf, st) = refs
    else:
        (z_ref, xbc_ref, dt_ref, wconv_ref, bconv_ref, dtb_ref, alog_ref,
         d_ref, nw_ref, y_ref, st_out_ref, xbuf, xc, ybuf, st) = refs
    c = pl.program_id(1)

    @pl.when(c == 0)
    def _init():
        if has_init:
            xbuf[0:SUBLANES, :] = conv0_ref[0]
            st[...] = st0_ref[0]
        else:
            xbuf[0:SUBLANES, :] = jnp.zeros((SUBLANES, CONV_DIM), f32)
            st[...] = jnp.zeros(st.shape, f32)

    xbuf[SUBLANES:SUBLANES + c_in, :] = xbc_ref[...]
    if c_in < T:
        xbuf[SUBLANES + c_in:SUBLANES + T, :] = jnp.zeros((T - c_in, CONV_DIM), f32)

    for c0 in range(0, CONV_DIM, COL_CHUNK):
        cs = slice(c0, c0 + COL_CHUNK)
        xin = xbuf[0:T + SUBLANES, cs]
        acc = bconv_ref[:, cs]
        for k in range(CONV_W):
            shift = CONV_W - 1 - k
            tap = xin if shift == 0 else pltpu.roll(xin, shift, 0)
            acc = acc + tap[SUBLANES:, :] * wconv_ref[k:k + 1, cs]
        xc[:, cs] = acc * _sigmoid(acc)
    if nc > 1:
        xbuf[0:SUBLANES, :] = xbuf[T:T + SUBLANES, :]

    def pad_rows(v):
        if c_in == T:
            return v
        return jnp.concatenate([v, jnp.zeros((T - c_in, v.shape[1]), v.dtype)], axis=0)

    dtv = pad_rows(_softplus(dt_ref[...] + dtb_ref[...]))
    a = -jnp.exp(alog_ref[...])
    da = dtv * a
    row = lax.broadcasted_iota(jnp.int32, (T, T), 0)
    col = lax.broadcasted_iota(jnp.int32, (T, T), 1)
    causal = col <= row
    tri = causal.astype(f32)
    acs = jnp.dot(tri, da, preferred_element_type=f32, precision=HIGHEST)
    acs_t = acs.T
    dt_t = dtv.T
    w_all = dt_t * jnp.exp(acs_t[:, T - 1:T] - acs_t)
    lo = lax.broadcasted_iota(jnp.int32, (1, LANES), 1) < SSM_HEAD_DIM

    for g in range(SSM_GROUPS):
        bm = xc[:, D_INNER + g * D_STATE:D_INNER + (g + 1) * D_STATE]
        cm = xc[:, D_INNER + (SSM_GROUPS + g) * D_STATE:D_INNER + (SSM_GROUPS + g + 1) * D_STATE]
        cb = lax.dot_general(cm.astype(bf16), bm.astype(bf16), (((1,), (1,)), ((), ())),
                             preferred_element_type=f32)
        bm_t = bm.T
        for e in range(SSM_PAIRS // SSM_GROUPS):
            pr = g * (SSM_PAIRS // SSM_GROUPS) + e
            ls = slice(pr * LANES, (pr + 1) * LANES)
            x_pair = xc[:, ls]
            xb = x_pair.astype(bf16)
            rhs = jnp.concatenate([xb, st[pr].astype(bf16)], axis=0)
            r, u, ea_last = [], [], []
            for hh in range(2):
                h = 2 * pr + hh
                a_b = jnp.broadcast_to(acs[:, h:h + 1], (T, LANES))
                seg = a_b - acs_t[h:h + 1, :]
                dec = jnp.exp(jnp.where(causal, seg, -jnp.inf))
                m_h = cb * dec * dt_t[h:h + 1, :]
                e_a = jnp.exp(a_b)
                lhs = jnp.concatenate([m_h.astype(bf16), (cm * e_a).astype(bf16)], axis=1)
                r.append(jnp.dot(lhs, rhs, preferred_element_type=f32))
                lhs_s = (bm_t * w_all[h:h + 1, :]).astype(bf16)
                u.append(jnp.dot(lhs_s, xb, preferred_element_type=f32))
                ea_last.append(e_a[T - 1:T, :])
            st[pr] = st[pr] * jnp.where(lo, ea_last[0], ea_last[1]) + jnp.where(lo, u[0], u[1])
            yv = jnp.where(lo, r[0], r[1]) + d_ref[:, ls] * x_pair
            zz = pad_rows(z_ref[:, ls])
            ybuf[:, ls] = yv * (zz * _sigmoid(zz))
        gw = D_INNER // SSM_GROUPS
        gs = slice(g * gw, (g + 1) * gw)
        yg = ybuf[:, gs]
        ms = jnp.mean(yg * yg, axis=-1, keepdims=True)
        y_ref[:, gs] = (yg * lax.rsqrt(ms + RMS_EPS) * nw_ref[:, gs])[0:c_in]

    @pl.when(c == nc - 1)
    def _fin():
        st_out_ref[0] = st[...]


def _ssd(z, xbc, dt, conv0, st0, wconv, bconv, dtb, alog, d_exp, nw, *,
         n_seq, seq_len, row_off, interpret=False):
    T = SSD_CHUNK
    has_init = conv0 is not None
    if seq_len >= T:
        c_in, nc = T, seq_len // T
    else:
        c_in, nc = seq_len, 1
    off = row_off // c_in

    def rows(bi, ci):
        return (off + bi * nc + ci, 0)

    const = lambda bi, ci: (0, 0)
    in_specs = [pl.BlockSpec((c_in, D_INNER), rows),
                pl.BlockSpec((c_in, CONV_DIM), rows),
                pl.BlockSpec((c_in, LANES), rows)]
    args = [z, xbc, dt]
    if has_init:
        in_specs += [pl.BlockSpec((1, SUBLANES, CONV_DIM), lambda bi, ci: (bi, 0, 0)),
                     pl.BlockSpec((1, SSM_PAIRS, D_STATE, LANES), lambda bi, ci: (bi, 0, 0, 0))]
        args += [conv0, st0]
    in_specs += [pl.BlockSpec((SUBLANES, CONV_DIM), const),
                 pl.BlockSpec((1, CONV_DIM), const),
                 pl.BlockSpec((1, LANES), const),
                 pl.BlockSpec((1, LANES), const),
                 pl.BlockSpec((1, D_INNER), const),
                 pl.BlockSpec((1, D_INNER), const)]
    args += [wconv, bconv, dtb, alog, d_exp, nw]
    return pl.pallas_call(
        functools.partial(_ssd_kernel, c_in=c_in, nc=nc, has_init=has_init),
        out_shape=(jax.ShapeDtypeStruct((n_seq * seq_len, D_INNER), f32),
                   jax.ShapeDtypeStruct((n_seq, SSM_PAIRS, D_STATE, LANES), f32)),
        grid=(n_seq, nc),
        in_specs=in_specs,
        out_specs=(pl.BlockSpec((c_in, D_INNER), lambda bi, ci: (bi * nc + ci, 0)),
                   pl.BlockSpec((1, SSM_PAIRS, D_STATE, LANES), lambda bi, ci: (bi, 0, 0, 0))),
        scratch_shapes=[pltpu.VMEM((T + 2 * SUBLANES, CONV_DIM), f32),
                        pltpu.VMEM((T, CONV_DIM), f32),
                        pltpu.VMEM((T, D_INNER), f32),
                        pltpu.VMEM((SSM_PAIRS, D_STATE, LANES), f32)],
        compiler_params=_params(("parallel", "arbitrary")),
        name="ssd_scan_init" if has_init else "ssd_scan",
        interpret=interpret,
    )(*args)


def _state_to_pairs(s):
    b = s.shape[0]
    s = s.reshape(b, SSM_PAIRS, 2, SSM_HEAD_DIM, D_STATE)
    return s.transpose(0, 1, 4, 2, 3).reshape(b, SSM_PAIRS, D_STATE, 2 * SSM_HEAD_DIM)


def _pairs_to_state(s):
    b = s.shape[0]
    s = s.reshape(b, SSM_PAIRS, D_STATE, 2, SSM_HEAD_DIM)
    return s.transpose(0, 1, 3, 4, 2).reshape(b, SSM_HEADS, SSM_HEAD_DIM, D_STATE)


def _proj_ln_kernel(yp_ref, ys_ref, w_ref, x_ref, g_ref, b_ref, o_ref, *, split_blk, slabs):
    if slabs:
        y_p = jnp.concatenate([yp_ref[s] for s in range(yp_ref.shape[0])], axis=1)
    else:
        y_p = yp_ref[...]
    y = jnp.where(pl.program_id(0) < split_blk, y_p, ys_ref[...])
    yb = y.astype(bf16)
    for c in range(0, D_MODEL, COL_CHUNK):
        cs = slice(c, c + COL_CHUNK)
        o_ref[:, cs] = DEEPNORM_ALPHA * x_ref[:, cs] + jnp.dot(yb, w_ref[:, cs],
                                                               preferred_element_type=f32)
    o_ref[...] = _layer_norm_rows(o_ref[...], g_ref[...], b_ref[...])


def _proj_ln(y_p, y_s, w, x, g, b, interpret=False):
    n = x.shape[0]
    k = y_s.shape[1]
    tm = TOKEN_TILE
    slabs = y_p.ndim == 3
    rows_p = y_p.shape[1] if slabs else y_p.shape[0]
    assert rows_p % tm == 0 and y_s.shape[0] % tm == 0
    split_blk = rows_p // tm
    if slabs:
        p_spec = pl.BlockSpec((k // LANES, tm, LANES),
                              lambda i: (0, jnp.minimum(i, split_blk - 1), 0))
    else:
        p_spec = pl.BlockSpec((tm, k), lambda i: (jnp.minimum(i, split_blk - 1), 0))
    return pl.pallas_call(
        functools.partial(_proj_ln_kernel, split_blk=split_blk, slabs=slabs),
        out_shape=jax.ShapeDtypeStruct((n, D_MODEL), f32),
        grid=(n // tm,),
        in_specs=[p_spec,
                  pl.BlockSpec((tm, k), lambda i: (jnp.maximum(i - split_blk, 0), 0)),
                  pl.BlockSpec((k, D_MODEL), lambda i: (0, 0)),
                  pl.BlockSpec((tm, D_MODEL), lambda i: (i, 0)),
                  pl.BlockSpec((1, D_MODEL), lambda i: (0, 0)),
                  pl.BlockSpec((1, D_MODEL), lambda i: (0, 0))],
        out_specs=pl.BlockSpec((tm, D_MODEL), lambda i: (i, 0)),
        compiler_params=_params(("parallel",)),
        name="proj_postnorm",
        interpret=interpret,
    )(y_p, y_s, w, x, g, b)


def _router_kernel(x_ref, w_in_ref, b_ref, e_ref, w_ref, cnt_scr):
    @pl.when(pl.program_id(0) == 0)
    def _():
        cnt_scr[...] = jnp.zeros(cnt_scr.shape, f32)

    logits_tok = jnp.dot(x_ref[...], w_in_ref[...], preferred_element_type=f32,
                         precision=HIGHEST)
    logits = logits_tok.T[0:N_EXPERTS, :]
    s = _sigmoid(logits)
    sb = s + b_ref[...]
    srow = [s[i:i + 1, :] for i in range(N_EXPERTS)]
    brow = [sb[i:i + 1, :] for i in range(N_EXPERTS)]
    gscore = []
    for g in range(N_EXPERT_GROUPS):
        v = brow[g * EXPERTS_PER_GROUP:(g + 1) * EXPERTS_PER_GROUP]
        best = None
        for i in range(EXPERTS_PER_GROUP):
            for j in range(i + 1, EXPERTS_PER_GROUP):
                hi = jnp.maximum(v[i], v[j])
                lo_ = jnp.minimum(v[i], v[j])
                p = hi + lo_
                best = p if best is None else jnp.maximum(best, p)
        gscore.append(best)
    gi = jnp.zeros_like(gscore[0], dtype=jnp.int32)
    gbest = gscore[0]
    for g in range(1, N_EXPERT_GROUPS):
        upd = gscore[g] > gbest
        gi = jnp.where(upd, g, gi)
        gbest = jnp.where(upd, gscore[g], gbest)
    vb, vs = [], []
    for k in range(EXPERTS_PER_GROUP):
        tb, ts = brow[k], srow[k]
        for g in range(1, N_EXPERT_GROUPS):
            tb = jnp.where(gi == g, brow[g * EXPERTS_PER_GROUP + k], tb)
            ts = jnp.where(gi == g, srow[g * EXPERTS_PER_GROUP + k], ts)
        vb.append(tb)
        vs.append(ts)
    i1 = jnp.zeros_like(gi)
    b1, s1 = vb[0], vs[0]
    for k in range(1, EXPERTS_PER_GROUP):
        upd = vb[k] > b1
        i1 = jnp.where(upd, k, i1)
        b1 = jnp.where(upd, vb[k], b1)
        s1 = jnp.where(upd, vs[k], s1)
    i2 = jnp.full_like(gi, -1)
    b2 = jnp.full_like(b1, -jnp.inf)
    s2 = jnp.zeros_like(s1)
    for k in range(EXPERTS_PER_GROUP):
        upd = (i1 != k) & ((vb[k] > b2) | (i2 < 0))
        i2 = jnp.where(upd, k, i2)
        b2 = jnp.where(upd, vb[k], b2)
        s2 = jnp.where(upd, vs[k], s2)
    den = s1 + s2
    tm = gi.shape[1]
    e0 = gi * EXPERTS_PER_GROUP + i1
    e1 = gi * EXPERTS_PER_GROUP + i2
    eid = lax.broadcasted_iota(jnp.int32, (N_EXPERTS, tm), 0)
    hit = jnp.where((eid == e0) | (eid == e1), 1.0, 0.0)
    t_row = lax.broadcasted_iota(jnp.int32, (tm, tm), 0)
    t_col = lax.broadcasted_iota(jnp.int32, (tm, tm), 1)
    before = jnp.where(t_row < t_col, 1.0, 0.0).astype(bf16)
    rank_all = jnp.dot(hit.astype(bf16), before, preferred_element_type=f32) + cnt_scr[...]
    r0 = jnp.zeros_like(s1)
    r1 = jnp.zeros_like(s1)
    for k in range(N_EXPERTS):
        rk = rank_all[k:k + 1, :]
        r0 = jnp.where(e0 == k, rk, r0)
        r1 = jnp.where(e1 == k, rk, r1)
    cnt_scr[...] = cnt_scr[...] + jnp.sum(hit, axis=-1, keepdims=True)
    zi = jnp.zeros((SUBLANES - 4, tm), jnp.int32)
    zf = jnp.zeros((SUBLANES - 2, tm), f32)
    e_ref[...] = jnp.concatenate([e0, e1, r0.astype(jnp.int32), r1.astype(jnp.int32), zi], axis=0)
    w_ref[...] = jnp.concatenate([s1 / den, s2 / den, zf], axis=0)


def _router(x, w_pad, b, interpret=False):
    n = x.shape[0]
    tm = TOKEN_TILE
    return pl.pallas_call(
        _router_kernel,
        out_shape=(jax.ShapeDtypeStruct((SUBLANES, n), jnp.int32),
                   jax.ShapeDtypeStruct((SUBLANES, n), f32)),
        grid=(n // tm,),
        in_specs=[pl.BlockSpec((tm, D_MODEL), lambda i: (i, 0)),
                  pl.BlockSpec((D_MODEL, LANES), lambda i: (0, 0)),
                  pl.BlockSpec((N_EXPERTS, 1), lambda i: (0, 0))],
        out_specs=(pl.BlockSpec((SUBLANES, tm), lambda i: (0, i)),
                   pl.BlockSpec((SUBLANES, tm), lambda i: (0, i))),
        scratch_shapes=[pltpu.VMEM((N_EXPERTS, 1), f32)],
        compiler_params=_params(("arbitrary",)),
        name="moe_router",
        interpret=interpret,
    )(x, w_pad, b)


def _expert_kernel(blk_e_ref, n_used_ref, src_ref, dst_ref, x_hbm, wg_ref, wu_ref, wd_ref, y_hbm,
                   wg_b, wu_b, wd_b, h_scr, xbuf, obuf, gsem, ssem, *, trash_row0, n_trash_blk):
    i = pl.program_id(0)
    n_blk = pl.num_programs(0)
    tm = EXPERT_TILE
    n_used = n_used_ref[0]
    slot = i % 2
    other = 1 - slot
    e = blk_e_ref[i]
    e_prev = blk_e_ref[jnp.maximum(i - 1, 0)]

    def gather_copy(base, sl, r):
        return pltpu.make_async_copy(x_hbm.at[pl.ds(src_ref[base + r], 1)],
                                     xbuf.at[sl, pl.ds(r, 1)], gsem.at[sl])

    def scatter_copy(base, sl, r):
        return pltpu.make_async_copy(obuf.at[sl, pl.ds(r, 1)],
                                     y_hbm.at[pl.ds(dst_ref[base + tm + r], 1)], ssem.at[sl])

    def wait_gather(sl):
        for r in range(tm):
            gather_copy(0, sl, r).wait()

    def wait_scatter(sl):
        for r in range(tm):
            scatter_copy(0, sl, r).wait()

    @pl.when(i == 0)
    def _first():
        def body(r, c):
            gather_copy(0, 0, r).start()
            return c
        lax.fori_loop(0, tm, body, 0, unroll=8)
        obuf[1] = jnp.zeros((tm, D_MODEL), f32)
        fills = [pltpu.make_async_copy(obuf.at[1], y_hbm.at[pl.ds(trash_row0 + t * tm, tm)],
                                       ssem.at[1]) for t in range(n_trash_blk)]
        for f in fills:
            f.start()
        for f in fills:
            f.wait()

    @pl.when((i == 0) | (e != e_prev))
    def _cast():
        for src, dst in ((wg_ref, wg_b), (wu_ref, wu_b), (wd_ref, wd_b)):
            for r0 in range(0, src.shape[2], CAST_ROWS):
                dst[r0:r0 + CAST_ROWS, :] = src[0, 0, r0:r0 + CAST_ROWS, :].astype(bf16)

    @pl.when(i < n_used)
    def _block():
        wait_gather(slot)

        @pl.when(i >= 1)
        def _():
            wait_scatter(slot)

        nxt = jnp.minimum(i + 1, n_blk - 1) * tm
        prev = (i - 1) * tm
        for r in range(tm):
            gather_copy(nxt, other, r).start()
            scatter_copy(prev, other, r).start()
        xb = xbuf[slot].astype(bf16)
        for c in range(0, D_EXPERT, COL_CHUNK):
            cs = slice(c, c + COL_CHUNK)
            hg = jnp.dot(xb, wg_b[:, cs], preferred_element_type=f32)
            hu = jnp.dot(xb, wu_b[:, cs], preferred_element_type=f32)
            h_scr[:, cs] = (hg * _sigmoid(hg) * hu).astype(bf16)
        hb = h_scr[...]
        for c in range(0, D_MODEL, COL_CHUNK):
            cs = slice(c, c + COL_CHUNK)
            obuf[slot, :, cs] = jnp.dot(hb, wd_b[:, cs], preferred_element_type=f32)

        @pl.when(i == n_used - 1)
        def _drain():
            def body(r, c):
                scatter_copy(i * tm, slot, r).start()
                return c
            lax.fori_loop(0, tm, body, 0, unroll=8)
            wait_gather(other)
            wait_scatter(other)
            wait_scatter(slot)


def _experts(blk_e, n_used, src_row, dst_row, x, wg, wu, wd, layer, n_out_rows, interpret=False):
    tm = EXPERT_TILE
    n_blk = blk_e.shape[0]
    dummy_dst = n_out_rows + N_EXPERTS * tm + jnp.arange(tm, dtype=jnp.int32)
    dst_row = jnp.concatenate([dummy_dst, dst_row])
    wspec = lambda i, be, nu, sr, ds: (layer, be[i], 0, 0)
    hbm = pl.BlockSpec(memory_space=pl.ANY)
    return pl.pallas_call(
        functools.partial(_expert_kernel, trash_row0=n_out_rows, n_trash_blk=N_EXPERTS),
        out_shape=jax.ShapeDtypeStruct((n_out_rows + (N_EXPERTS + 1) * tm, D_MODEL), f32),
        grid_spec=pltpu.PrefetchScalarGridSpec(
            num_scalar_prefetch=4,
            grid=(n_blk,),
            in_specs=[hbm,
                      pl.BlockSpec((1, 1, D_MODEL, D_EXPERT), wspec),
                      pl.BlockSpec((1, 1, D_MODEL, D_EXPERT), wspec),
                      pl.BlockSpec((1, 1, D_EXPERT, D_MODEL), wspec)],
            out_specs=hbm,
            scratch_shapes=[pltpu.VMEM((D_MODEL, D_EXPERT), bf16),
                            pltpu.VMEM((D_MODEL, D_EXPERT), bf16),
                            pltpu.VMEM((D_EXPERT, D_MODEL), bf16),
                            pltpu.VMEM((tm, D_EXPERT), bf16),
                            pltpu.VMEM((2, tm, D_MODEL), f32),
                            pltpu.VMEM((2, tm, D_MODEL), f32),
                            pltpu.SemaphoreType.DMA((2,)),
                            pltpu.SemaphoreType.DMA((2,))]),
        compiler_params=pltpu.CompilerParams(dimension_semantics=("arbitrary",),
                                             vmem_limit_bytes=EXPERT_VMEM_LIMIT),
        name="moe_experts",
        interpret=interpret,
    )(blk_e, n_used, src_row, dst_row, x, wg, wu, wd)


def _combine_ln_kernel(x_ref, y0_ref, y1_ref, w_ref, g_ref, b_ref, *o_refs, split_blk):
    w = w_ref[...]
    v = DEEPNORM_ALPHA * x_ref[...] + (y0_ref[...] * w[:, 0:1] + y1_ref[...] * w[:, 1:2])
    res = _layer_norm_rows(v, g_ref[...], b_ref[...])
    if split_blk is None:
        o_refs[0][...] = res
    else:
        i = pl.program_id(0)

        @pl.when(i < split_blk)
        def _():
            o_refs[0][...] = res

        @pl.when(i >= split_blk)
        def _():
            o_refs[1][...] = res


def _combine_ln(x, y, w_col, g, b, split=None, interpret=False):
    n = x.shape[0]
    tm = TOKEN_TILE
    n_tiles = n // tm
    row = lambda i: (i, 0)
    const = lambda i: (0, 0)
    if split is None:
        split_blk = None
        out_shape = jax.ShapeDtypeStruct((n, D_MODEL), f32)
        out_specs = pl.BlockSpec((tm, D_MODEL), row)
    else:
        assert split % tm == 0 and (n - split) % tm == 0
        split_blk = split // tm
        out_shape = (jax.ShapeDtypeStruct((split, D_MODEL), f32),
                     jax.ShapeDtypeStruct((n - split, D_MODEL), f32))
        out_specs = (pl.BlockSpec((tm, D_MODEL), lambda i: (jnp.minimum(i, split_blk - 1), 0)),
                     pl.BlockSpec((tm, D_MODEL), lambda i: (jnp.maximum(i - split_blk, 0), 0)))
    return pl.pallas_call(
        functools.partial(_combine_ln_kernel, split_blk=split_blk),
        out_shape=out_shape,
        grid=(n // tm,),
        in_specs=[pl.BlockSpec((tm, D_MODEL), row),
                  pl.BlockSpec((tm, D_MODEL), row),
                  pl.BlockSpec((tm, D_MODEL), lambda i: (i + n_tiles, 0)),
                  pl.BlockSpec((tm, SUBLANES), row),
                  pl.BlockSpec((1, D_MODEL), const),
                  pl.BlockSpec((1, D_MODEL), const)],
        out_specs=out_specs,
        compiler_params=_params(("arbitrary",)),
        name="moe_combine_postnorm",
        interpret=interpret,
    )(x, y, y, w_col, g, b)


def _rows(x, idx):
    return x.at[idx].get(mode="promise_in_bounds")


def _moe(x, router_w_pad, router_b, wg, wu, wd, layer, g, b, split=None, interpret=False):
    n = x.shape[0]
    route, e_w = _router(x, router_w_pad, router_b, interpret)
    e_idx = route[0:MOE_TOP_K]
    rank = route[MOE_TOP_K:2 * MOE_TOP_K]
    n_assign = n * MOE_TOP_K
    experts = jnp.arange(N_EXPERTS, dtype=jnp.int32)
    onehot = e_idx[:, :, None] == experts
    counts = jnp.sum(onehot, axis=(0, 1), dtype=jnp.int32)
    tm = EXPERT_TILE
    padded = (counts + tm - 1) // tm * tm
    pad_end = jnp.cumsum(padded)
    pad_start = pad_end - padded
    dest = jnp.sum(jnp.where(onehot, pad_start, 0), axis=-1) + rank
    n_blk = (n_assign + N_EXPERTS * (tm - 1)) // tm + 1
    slot_id = jnp.arange(n_assign, dtype=jnp.int32)
    held = jnp.full((n_blk * tm,), -1, jnp.int32).at[dest.reshape(-1)].set(
        slot_id, unique_indices=True, mode="promise_in_bounds")
    blk_start = jnp.arange(n_blk, dtype=jnp.int32) * tm
    blk_e = jnp.minimum(jnp.sum(pad_end[None, :] <= blk_start[:, None], axis=1, dtype=jnp.int32),
                        N_EXPERTS - 1)
    n_used = (pad_end[-1:] // tm).astype(jnp.int32)
    row = jnp.arange(n_blk * tm, dtype=jnp.int32)
    row_e = jnp.repeat(blk_e, tm)
    trash = n_assign + row_e * tm + jnp.clip(row - (pad_start + counts)[row_e], 0, tm - 1)
    src_row = jnp.where(held < 0, 0, jnp.where(held >= n, held - n, held))
    dst_row = jnp.where(held < 0, trash, held)
    y = _experts(blk_e, n_used, src_row, dst_row, x, wg, wu, wd, layer, n_assign, interpret)
    return _combine_ln(x, y, e_w.T, g, b, split, interpret)


def _rope_slab(t, cos, sin_signed, lo_half):
    swapped = jnp.where(lo_half, pltpu.roll(t, LANES - HEAD_DIM // 2, 1),
                        pltpu.roll(t, HEAD_DIM // 2, 1))
    return t * cos + swapped * sin_signed


def _qkv_kernel(x_ref, wq_ref, wkt_ref, wvt_ref, cos_ref, sin_ref, cost_ref, sint_ref,
                q_ref, kt_ref, vt_ref, ks_ref, vs_ref, *, prompt_blk):
    i = pl.program_id(0)
    tm = x_ref.shape[0]
    xb = x_ref[...].astype(bf16)
    cos = cos_ref[...]
    sin = sin_ref[...]
    lane = lax.broadcasted_iota(jnp.int32, (1, LANES), 1)
    lo_half = (lane % HEAD_DIM) < (HEAD_DIM // 2)
    for c in range(0, D_MODEL, COL_CHUNK):
        t = jnp.dot(xb, wq_ref[:, c:c + COL_CHUNK], preferred_element_type=f32)
        for s in range(0, COL_CHUNK, LANES):
            q_ref[(c + s) // LANES] = _rope_slab(t[:, s:s + LANES], cos, sin, lo_half)

    nt_dims = (((1,), (1,)), ((), ()))
    half = HEAD_DIM // 2
    hpc = COL_CHUNK // HEAD_DIM
    cos_t = cost_ref[...][None]
    sin_t = sint_ref[...][None]
    for r0 in range(0, D_MODEL, COL_CHUNK):
        rs = slice(r0, r0 + COL_CHUNK)
        hs = slice(r0 // HEAD_DIM, r0 // HEAD_DIM + hpc)
        kt = lax.dot_general(wkt_ref[rs, :], xb, nt_dims,
                             preferred_element_type=f32).reshape(hpc, HEAD_DIM, tm)
        sw = jnp.concatenate([kt[:, half:, :], kt[:, :half, :]], axis=1)
        kr = kt * cos_t + sw * sin_t
        vr = lax.dot_general(wvt_ref[rs, :], xb, nt_dims,
                             preferred_element_type=f32).reshape(hpc, HEAD_DIM, tm)

        @pl.when(i < prompt_blk)
        def _(kr=kr, vr=vr, hs=hs):
            kt_ref[0, hs] = kr
            vt_ref[0, hs] = vr

        @pl.when(i >= prompt_blk)
        def _(kr=kr, vr=vr, rs=rs):
            ks_ref[:, rs] = kr.reshape(COL_CHUNK, tm).T
            vs_ref[:, rs] = vr.reshape(COL_CHUNK, tm).T


def _qkv(x, wq, wkt, wvt, cos, sin, cos_t, sin_t, n_seq, seq_len, interpret=False):
    n = x.shape[0]
    tm = TOKEN_TILE
    n_p = n_seq * seq_len
    assert n_p % tm == 0 and seq_len % tm == 0 and (n - n_p) % tm == 0
    prompt_blk = n_p // tm
    tps = seq_len // tm
    row = lambda i: (i, 0)
    const = lambda i: (0, 0)

    def head_major(i):
        j = jnp.minimum(i, prompt_blk - 1)
        return (j // tps, 0, 0, j % tps)

    sample_rows = lambda i: (jnp.maximum(i - prompt_blk, 0), 0)
    kv_t = jax.ShapeDtypeStruct((n_seq, N_HEADS, HEAD_DIM, seq_len), f32)
    kv_s = jax.ShapeDtypeStruct((n - n_p, D_MODEL), f32)
    return pl.pallas_call(
        functools.partial(_qkv_kernel, prompt_blk=prompt_blk),
        out_shape=(jax.ShapeDtypeStruct((HEAD_PAIRS, n, LANES), f32), kv_t, kv_t, kv_s, kv_s),
        grid=(n // tm,),
        in_specs=[pl.BlockSpec((tm, D_MODEL), row),
                  pl.BlockSpec((D_MODEL, D_MODEL), const),
                  pl.BlockSpec((D_MODEL, D_MODEL), const),
                  pl.BlockSpec((D_MODEL, D_MODEL), const),
                  pl.BlockSpec((tm, LANES), row),
                  pl.BlockSpec((tm, LANES), row),
                  pl.BlockSpec((HEAD_DIM, tm), lambda i: (0, i)),
                  pl.BlockSpec((HEAD_DIM, tm), lambda i: (0, i))],
        out_specs=(pl.BlockSpec((HEAD_PAIRS, tm, LANES), lambda i: (0, i, 0)),
                   pl.BlockSpec((1, N_HEADS, HEAD_DIM, tm), head_major),
                   pl.BlockSpec((1, N_HEADS, HEAD_DIM, tm), head_major),
                   pl.BlockSpec((tm, D_MODEL), sample_rows),
                   pl.BlockSpec((tm, D_MODEL), sample_rows)),
        compiler_params=_params(("arbitrary",)),
        name="qkv_rope",
        interpret=interpret,
    )(x, wq, wkt, wvt, cos, sin, cos_t, sin_t)


def _rope_tables(pos):
    half = HEAD_DIM // 2
    inv = ROPE_THETA ** (-jnp.arange(half, dtype=f32) / half)
    ang = pos.astype(f32)[:, None] * inv
    cos, sin = jnp.cos(ang), jnp.sin(ang)
    cos_h = jnp.concatenate([cos, cos], axis=1)
    sin_h = jnp.concatenate([-sin, sin], axis=1)
    rep = (1, LANES // HEAD_DIM)
    return jnp.tile(cos_h, rep), jnp.tile(sin_h, rep), cos_h.T, sin_h.T


def _moba_prompt_kernel(q_ref, kt_ref, vt_ref, o_ref, kaug, vaug, kmcols, *, nb):
    BLK = MOBA_BLOCK
    lane = lax.broadcasted_iota(jnp.int32, (1, LANES), 1)
    lo = lane < HEAD_DIM
    nt_dims = (((1,), (1,)), ((), ()))

    zcol = jnp.zeros((HEAD_DIM, 1), f32)
    kmc = jnp.zeros((LANES, LANES), f32)
    for j in range(nb):
        cs = slice(j * BLK, (j + 1) * BLK)
        k0 = kt_ref[0, 0, :, cs]
        k1 = kt_ref[0, 1, :, cs]
        ind = jnp.where(lax.broadcasted_iota(jnp.int32, (HEAD_DIM, BLK), 0) == j,
                        1.0, 0.0).astype(bf16)
        ones = jnp.ones((HEAD_DIM, BLK), bf16)
        kaug[0, :, cs] = jnp.concatenate([k0.astype(bf16), ind], axis=0)
        kaug[1, :, cs] = jnp.concatenate([ind, k1.astype(bf16)], axis=0)
        vaug[0, :, cs] = jnp.concatenate([vt_ref[0, 0, :, cs].astype(bf16), ones], axis=0)
        vaug[1, :, cs] = jnp.concatenate([ones, vt_ref[0, 1, :, cs].astype(bf16)], axis=0)
        m0 = jnp.concatenate([jnp.mean(k0, axis=1, keepdims=True), zcol], axis=0)
        m1 = jnp.concatenate([zcol, jnp.mean(k1, axis=1, keepdims=True)], axis=0)
        kmc = jnp.where(lane == HEAD_DIM + j, m0, jnp.where(lane == j, m1, kmc))
    kmcols[...] = kmc

    row = lax.broadcasted_iota(jnp.int32, (BLK, BLK), 0)
    col = lax.broadcasted_iota(jnp.int32, (BLK, BLK), 1)
    diag_mask = col <= row

    def attend(iv):
        qp = q_ref[0, iv * BLK:(iv + 1) * BLK, :]
        qs = qp * (HEAD_DIM ** -0.5)
        q_own = [jnp.where(lo, qs, 0.0), jnp.where(lo, 0.0, qs)]
        if iv > MOBA_TOPK:
            gate = jnp.dot(qp, kmcols[...], preferred_element_type=f32, precision=HIGHEST)
            blkid = lane & (HEAD_DIM - 1)
            past = blkid < iv
            gate = jnp.where(past, gate, -jnp.inf)
            cnt = jnp.zeros((BLK, LANES), f32)
            for j in range(iv):
                c0 = jnp.broadcast_to(gate[:, HEAD_DIM + j:HEAD_DIM + j + 1], (BLK, LANES))
                c1 = jnp.broadcast_to(gate[:, j:j + 1], (BLK, LANES))
                cj = jnp.where(lo, c1, c0)
                ahead = (cj > gate) | ((cj == gate) & (blkid > j))
                cnt = cnt + jnp.where(ahead, 1.0, 0.0)
            bias = jnp.where((cnt < MOBA_TOPK) & past, 0.0, MASK_BIAS)
            q_aug = [jnp.where(lo, qs, bias), jnp.where(lo, bias, qs)]
        else:
            q_aug = q_own
        outs = []
        own = slice(iv * BLK, (iv + 1) * BLK)
        for hh in range(2):
            s_own = jnp.dot(q_own[hh].astype(bf16), kaug[hh, :, own], preferred_element_type=f32)
            s_own = jnp.where(diag_mask, s_own, -jnp.inf)
            m = jnp.max(s_own, axis=-1, keepdims=True)
            if iv > 0:
                s_past = jnp.dot(q_aug[hh].astype(bf16), kaug[hh, :, 0:iv * BLK],
                                 preferred_element_type=f32)
                m = jnp.maximum(m, jnp.max(s_past, axis=-1, keepdims=True))
                acc = lax.dot_general(jnp.exp(s_past - m).astype(bf16), vaug[hh, :, 0:iv * BLK],
                                      nt_dims, preferred_element_type=f32)
            p_own = jnp.exp(s_own - m).astype(bf16)
            acc_own = lax.dot_general(p_own, vaug[hh, :, own], nt_dims,
                                      preferred_element_type=f32)
            acc = acc + acc_own if iv > 0 else acc_own
            outs.append(acc / pltpu.roll(acc, HEAD_DIM, 1))
        o_ref[0, iv * BLK:(iv + 1) * BLK, :] = jnp.where(lo, outs[0], outs[1])

    for iv in range(nb):
        attend(iv)


def _moba_prompt(q_hp, k_t, v_t, interpret=False):
    BLK = MOBA_BLOCK
    n_seq, _, _, seq_len = k_t.shape
    nb = seq_len // BLK
    assert nb <= HEAD_DIM and seq_len % BLK == 0
    qo_map = lambda b, hp: (hp, b, 0)
    kv_map = lambda b, hp: (b, hp, 0, 0)
    return pl.pallas_call(
        functools.partial(_moba_prompt_kernel, nb=nb),
        out_shape=jax.ShapeDtypeStruct((HEAD_PAIRS, n_seq * seq_len, LANES), f32),
        grid=(n_seq, HEAD_PAIRS),
        in_specs=[pl.BlockSpec((1, seq_len, LANES), qo_map),
                  pl.BlockSpec((1, 2, HEAD_DIM, seq_len), kv_map),
                  pl.BlockSpec((1, 2, HEAD_DIM, seq_len), kv_map)],
        out_specs=pl.BlockSpec((1, seq_len, LANES), qo_map),
        scratch_shapes=[pltpu.VMEM((2, LANES, seq_len), bf16),
                        pltpu.VMEM((2, LANES, seq_len), bf16),
                        pltpu.VMEM((LANES, LANES), f32)],
        compiler_params=_params(("parallel", "parallel")),
        name="moba_prompt_attn",
        interpret=interpret,
    )(q_hp, k_t, v_t)


def _page_sum_kernel(pt_ref, *refs):
    pages, o_ref = refs[:PAGES_PER_STEP], refs[PAGES_PER_STEP]
    ppb = MOBA_BLOCK // PAGE_SIZE
    bps = PAGES_PER_STEP // ppb
    s = pl.program_id(1)
    lane = lax.broadcasted_iota(jnp.int32, (1, LANES), 1)

    @pl.when(s == 0)
    def _():
        o_ref[...] = jnp.zeros(o_ref.shape, f32)

    for h in range(N_HEADS):
        acc = o_ref[0, h]
        for blk in range(bps):
            t = pages[blk * ppb][0, h]
            for r in range(1, ppb):
                t = t + pages[blk * ppb + r][0, h]
            col = jnp.sum(t, axis=-1, keepdims=True)
            acc = jnp.where(lane == s * bps + blk, col, acc)
        o_ref[0, h] = acc


def _page_sums(page_table, cache_kt, n_blocks, interpret=False):
    bsz, n_pages = page_table.shape
    ppb = MOBA_BLOCK // PAGE_SIZE
    assert n_blocks <= LANES and (n_blocks * ppb) % PAGES_PER_STEP == 0
    steps = n_blocks * ppb // PAGES_PER_STEP
    pt_flat = page_table.reshape(-1)

    def page_map(r):
        return lambda b, s, pt: (pt[b * n_pages + s * PAGES_PER_STEP + r], 0, 0, 0)

    return pl.pallas_call(
        _page_sum_kernel,
        out_shape=jax.ShapeDtypeStruct((bsz, N_HEADS, HEAD_DIM, LANES), f32),
        grid_spec=pltpu.PrefetchScalarGridSpec(
            num_scalar_prefetch=1,
            grid=(bsz, steps),
            in_specs=[pl.BlockSpec((1, N_HEADS, HEAD_DIM, PAGE_SIZE), page_map(r))
                      for r in range(PAGES_PER_STEP)],
            out_specs=pl.BlockSpec((1, N_HEADS, HEAD_DIM, LANES), lambda b, s, pt: (b, 0, 0, 0))),
        compiler_params=_params(("parallel", "arbitrary")),
        name="moba_page_sums",
        interpret=interpret,
    )(pt_flat, *([cache_kt] * PAGES_PER_STEP))


def _sample_topk_kernel(q_ref, ks_ref, idx_ref, *, n_blocks, topk):
    lane = lax.broadcasted_iota(jnp.int32, (1, LANES), 1)
    lane_f = lane.astype(f32)
    for h in range(N_HEADS):
        g = jnp.dot(q_ref[0, h], ks_ref[0, h], preferred_element_type=f32,
                    precision=HIGHEST) * (1.0 / MOBA_BLOCK)
        g = jnp.where(lane < n_blocks, g, -jnp.inf)
        out = jnp.zeros(g.shape, f32)
        for t in range(topk):
            m = jnp.max(g, axis=-1, keepdims=True)
            idx = jnp.min(jnp.where(g == m, lane_f, float(LANES)), axis=-1, keepdims=True)
            out = jnp.where(lane == t, idx, out)
            g = jnp.where(lane_f == idx, -jnp.inf, g)
        idx_ref[0, h] = out.astype(jnp.int32)


def _sample_topk(q4, ksum_t, n_blocks, topk, interpret=False):
    bsz, _, nq, _ = q4.shape
    return pl.pallas_call(
        functools.partial(_sample_topk_kernel, n_blocks=n_blocks, topk=topk),
        out_shape=jax.ShapeDtypeStruct((bsz, N_HEADS, nq, LANES), jnp.int32),
        grid=(bsz,),
        in_specs=[pl.BlockSpec((1, N_HEADS, nq, HEAD_DIM), lambda b: (b, 0, 0, 0)),
                  pl.BlockSpec((1, N_HEADS, HEAD_DIM, LANES), lambda b: (b, 0, 0, 0))],
        out_specs=pl.BlockSpec((1, N_HEADS, nq, LANES), lambda b: (b, 0, 0, 0)),
        compiler_params=_params(("parallel",)),
        name="moba_sample_topk",
        interpret=interpret,
    )(q4, ksum_t)


def _sample_attn_kernel(phys_ref, q_ref, kn_ref, vn_ref, ck_hbm, cv_hbm, o_ref,
                        kbuf, vbuf, sem, *, nq, n_sel):
    g = pl.program_id(0)
    ng = pl.num_programs(0)
    slot = g % 2
    n_pg = nq * n_sel

    def page_copies(gi, sl, r):
        page = phys_ref[gi * n_pg + r]
        head = gi % N_HEADS
        return (pltpu.make_async_copy(ck_hbm.at[page, head], kbuf.at[sl, r], sem.at[0, sl]),
                pltpu.make_async_copy(cv_hbm.at[page, head], vbuf.at[sl, r], sem.at[1, sl]))

    def fetch(gi, sl):
        for r in range(n_pg):
            ck, cv = page_copies(gi, sl, r)
            ck.start()
            cv.start()

    @pl.when(g == 0)
    def _():
        fetch(0, 0)

    @pl.when(g + 1 < ng)
    def _():
        fetch(g + 1, 1 - slot)

    for r in range(n_pg):
        ck, cv = page_copies(g, slot, r)
        ck.wait()
        cv.wait()

    lane = lax.broadcasted_iota(jnp.int32, (1, LANES), 1)
    q_t = q_ref[0, 0] * (HEAD_DIM ** -0.5)
    kn = kn_ref[0, 0]
    vn = vn_ref[0, 0]
    out = jnp.zeros((HEAD_DIM, LANES), f32)
    for qi in range(nq):
        qb = jnp.broadcast_to(q_t[:, qi:qi + 1], (HEAD_DIM, LANES))
        s_own = jnp.sum(kn * qb, axis=0, keepdims=True)
        s_own = jnp.where(lane <= qi, s_own, -jnp.inf)
        s_sel = [jnp.sum(kbuf[slot, qi * n_sel + r] * qb, axis=0, keepdims=True)
                 for r in range(n_sel)]
        m_lane = s_own
        for s in s_sel:
            m_lane = jnp.maximum(m_lane, s)
        m = jnp.max(m_lane, axis=-1, keepdims=True)
        p_own = jnp.exp(s_own - m)
        p_sum = p_own
        acc = vn * p_own
        for r, s in enumerate(s_sel):
            p = jnp.exp(s - m)
            p_sum = p_sum + p
            acc = acc + vbuf[slot, qi * n_sel + r] * p
        den = jnp.sum(p_sum, axis=-1, keepdims=True)
        o_col = jnp.sum(acc, axis=-1, keepdims=True) / den
        out = jnp.where(lane == qi, o_col, out)
    o_ref[0, 0] = out


def _sample_attn(phys, q_t, kn_t, vn_t, cache_kt, cache_vt, nq, n_sel, interpret=False):
    bsz = q_t.shape[0]
    small = pl.BlockSpec((1, 1, HEAD_DIM, LANES),
                         lambda g, ph: (g // N_HEADS, g % N_HEADS, 0, 0))
    hbm = pl.BlockSpec(memory_space=pl.ANY)
    return pl.pallas_call(
        functools.partial(_sample_attn_kernel, nq=nq, n_sel=n_sel),
        out_shape=jax.ShapeDtypeStruct((bsz, N_HEADS, HEAD_DIM, LANES), f32),
        grid_spec=pltpu.PrefetchScalarGridSpec(
            num_scalar_prefetch=1,
            grid=(bsz * N_HEADS,),
            in_specs=[small, small, small, hbm, hbm],
            out_specs=small,
            scratch_shapes=[pltpu.VMEM((2, nq * n_sel, HEAD_DIM, PAGE_SIZE), f32),
                            pltpu.VMEM((2, nq * n_sel, HEAD_DIM, PAGE_SIZE), f32),
                            pltpu.SemaphoreType.DMA((2, 2))]),
        compiler_params=_params(("arbitrary",)),
        name="moba_sample_attn",
        interpret=interpret,
    )(phys, q_t, kn_t, vn_t, cache_kt, cache_vt)


def _heads_first(t, bsz, seq_len):
    return t.reshape(bsz, seq_len, N_HEADS, HEAD_DIM).transpose(0, 2, 1, 3)


def _forward(x_prompt, x_sample, state_ssm, state_conv, cache_k, cache_v, page_table,
             ln_gain, ln_bias, ssm_w_in, ssm_w_conv, ssm_b_conv, ssm_dt_bias, ssm_a_log,
             ssm_d, ssm_norm_w, ssm_w_out, attn_w_q, attn_w_kv, attn_w_o,
             router_w, router_b, moe_w_gate, moe_w_up, moe_w_down, interpret=False):
    bp, lp, _ = x_prompt.shape
    bs, ls, _ = x_sample.shape
    n_p, n_s = bp * lp, bs * ls
    n = n_p + n_s
    n_pages = page_table.shape[1]
    past_len = n_pages * PAGE_SIZE
    assert n % TOKEN_TILE == 0 and lp % MOBA_BLOCK == 0 and lp % SSD_CHUNK == 0
    assert CONV_W - 1 <= ls <= SUBLANES and n_p % ls == 0
    assert past_len % MOBA_BLOCK == 0 and past_len // MOBA_BLOCK >= MOBA_TOPK
    assert ls <= MOBA_BLOCK - past_len % MOBA_BLOCK

    h = jnp.concatenate([x_prompt.reshape(n_p, D_MODEL), x_sample.reshape(n_s, D_MODEL)], axis=0)
    router_wp = jnp.pad(router_w, ((0, 0), (0, LANES - N_EXPERTS)))
    router_bc = router_b.reshape(N_EXPERTS, 1)

    def vec(v):
        return v.reshape(1, -1)

    l = 0
    w_in = ssm_w_in[l]
    wz = w_in[:, :D_INNER].astype(bf16)
    wx = w_in[:, D_INNER:D_INNER + CONV_DIM].astype(bf16)
    wdt = jnp.pad(w_in[:, D_INNER + CONV_DIM:], ((0, 0), (0, LANES - SSM_HEADS)))
    z, xbc, dt = _in_proj(h, wz, wx, wdt, interpret)

    pad_h = (0, LANES - SSM_HEADS)
    wconv = jnp.pad(ssm_w_conv[l], ((0, SUBLANES - CONV_W), (0, 0)))
    ssd_w = (wconv, vec(ssm_b_conv[l]), vec(jnp.pad(ssm_dt_bias[l], pad_h)),
             vec(jnp.pad(ssm_a_log[l], pad_h)), vec(jnp.repeat(ssm_d[l], SSM_HEAD_DIM)),
             vec(ssm_norm_w[l]))
    y_p, st_p = _ssd(z, xbc, dt, None, None, *ssd_w, n_seq=bp, seq_len=lp, row_off=0,
                     interpret=interpret)
    conv0 = jnp.pad(state_conv[l], ((0, 0), (SUBLANES - (CONV_W - 1), 0), (0, 0)))
    y_s, st_s = _ssd(z, xbc, dt, conv0, _state_to_pairs(state_ssm[l]), *ssd_w,
                     n_seq=bs, seq_len=ls, row_off=n_p, interpret=interpret)
    ssm_prompt = _pairs_to_state(st_p)[None]
    ssm_sample = _pairs_to_state(st_s)[None]
    tail = jnp.arange(-(CONV_W - 1), 0, dtype=jnp.int32)
    rows_p = ((jnp.arange(bp, dtype=jnp.int32) + 1) * lp)[:, None] + tail
    rows_s = (n_p + (jnp.arange(bs, dtype=jnp.int32) + 1) * ls)[:, None] + tail
    conv_prompt = _rows(xbc, rows_p.reshape(-1)).reshape(1, bp, CONV_W - 1, CONV_DIM)
    conv_sample = _rows(xbc, rows_s.reshape(-1)).reshape(1, bs, CONV_W - 1, CONV_DIM)

    h = _proj_ln(y_p, y_s, ssm_w_out[l].astype(bf16), h, vec(ln_gain[l, 0]), vec(ln_bias[l, 0]),
                 interpret)
    h = _moe(h, router_wp, router_bc, moe_w_gate, moe_w_up, moe_w_down, l,
             vec(ln_gain[l, 1]), vec(ln_bias[l, 1]), interpret=interpret)

    l = 1
    pos = jnp.concatenate([jnp.tile(jnp.arange(lp), bp), jnp.tile(past_len + jnp.arange(ls), bs)])
    cos_tok, sin_tok, cos_dim, sin_dim = _rope_tables(pos)
    hd_all = N_HEADS * HEAD_DIM
    q_hp, k_t, v_t, k_s, v_s = _qkv(h, attn_w_q[0].astype(bf16),
                                    attn_w_kv[:, :hd_all].T.astype(bf16),
                                    attn_w_kv[:, hd_all:].T.astype(bf16),
                                    cos_tok, sin_tok, cos_dim, sin_dim, bp, lp, interpret)
    k_prompt = jnp.swapaxes(k_t, 2, 3)
    v_prompt = jnp.swapaxes(v_t, 2, 3)
    k_sample = _heads_first(k_s, bs, ls)
    v_sample = _heads_first(v_s, bs, ls)
    q_sample = (q_hp[:, n_p:, :].reshape(HEAD_PAIRS, bs, ls, 2, HEAD_DIM)
                .transpose(1, 0, 3, 2, 4).reshape(bs, N_HEADS, ls, HEAD_DIM))

    o_p = _moba_prompt(q_hp, k_t, v_t, interpret)

    n_full = past_len // MOBA_BLOCK
    ppb = MOBA_BLOCK // PAGE_SIZE
    cache_kt = jnp.swapaxes(cache_k, 2, 3)
    cache_vt = jnp.swapaxes(cache_v, 2, 3)
    ksum_t = _page_sums(page_table, cache_kt, n_full, interpret)
    idx = _sample_topk(q_sample, ksum_t, n_full, MOBA_TOPK, interpret)[..., :MOBA_TOPK]
    lpage = idx[..., None] * ppb + jnp.arange(ppb, dtype=jnp.int32)
    phys = page_table[jnp.arange(bs)[:, None, None, None, None], lpage]

    def lanes_last(t):
        return jnp.pad(jnp.swapaxes(t, 2, 3), ((0, 0), (0, 0), (0, 0), (0, LANES - ls)))

    o_st = _sample_attn(phys.reshape(-1).astype(jnp.int32), lanes_last(q_sample),
                        lanes_last(k_sample), lanes_last(v_sample), cache_kt, cache_vt,
                        ls, MOBA_TOPK * ppb, interpret)
    o_s = o_st[..., :ls].transpose(0, 3, 1, 2).reshape(n_s, D_MODEL)

    h = _proj_ln(o_p, o_s, attn_w_o[0].astype(bf16), h, vec(ln_gain[l, 0]), vec(ln_bias[l, 0]),
                 interpret)
    h_p, h_s = _moe(h, router_wp, router_bc, moe_w_gate, moe_w_up, moe_w_down, l,
                    vec(ln_gain[l, 1]), vec(ln_bias[l, 1]), split=n_p, interpret=interpret)

    y_prompt = h_p.reshape(bp, lp, D_MODEL)
    y_sample = h_s.reshape(bs, ls, D_MODEL)
    return (y_prompt, y_sample, ssm_prompt, conv_prompt, k_prompt, v_prompt,
            ssm_sample, conv_sample, k_sample, v_sample)


def kernel(x_prompt, x_sample, state_ssm, state_conv, cache_k, cache_v, page_table, ln_gain, ln_bias, ssm_w_in, ssm_w_conv, ssm_b_conv, ssm_dt_bias, ssm_a_log, ssm_d, ssm_norm_w, ssm_w_out, attn_w_q, attn_w_kv, attn_w_o, router_w, router_b, moe_w_gate, moe_w_up, moe_w_down):
    return _forward(x_prompt, x_sample, state_ssm, state_conv, cache_k, cache_v, page_table,
                    ln_gain, ln_bias, ssm_w_in, ssm_w_conv, ssm_b_conv, ssm_dt_bias, ssm_a_log,
                    ssm_d, ssm_norm_w, ssm_w_out, attn_w_q, attn_w_kv, attn_w_o,
                    router_w, router_b, moe_w_gate, moe_w_up, moe_w_down)
```

```python
import functools
import math

import jax
import jax.numpy as jnp
from jax import lax
from jax.experimental import pallas as pl
from jax.experimental.pallas import tpu as pltpu

f32 = jnp.float32
bf16 = jnp.bfloat16
HIGHEST = lax.Precision.HIGHEST

D_MODEL = 1024
DEPTH = 2
PAGE_SIZE = 128
N_A_LAYERS = 1
SSM_HEAD_DIM = 64
D_INNER = 2048
SSM_HEADS = D_INNER // SSM_HEAD_DIM
SSM_GROUPS = 4
D_STATE = 128
CONV_W = 4
CONV_DIM = D_INNER + 2 * SSM_GROUPS * D_STATE
SSD_CHUNK = 128
N_HEADS = 16
HEAD_DIM = 64
MOBA_BLOCK = 256
MOBA_TOPK = 3
ROPE_THETA = 10000.0
N_EXPERTS = 16
N_EXPERT_GROUPS = 4
EXPERTS_PER_GROUP = 4
MOE_TOP_K = 2
D_EXPERT = 1024
DEEPNORM_ALPHA = (2.0 * DEPTH) ** 0.25
LN_EPS = 1e-5
RMS_EPS = 1e-6
MASK_BIAS = -1e30

LANES = 128
SUBLANES = 8
VMEM_LIMIT = 48 * 1024 * 1024

TOKEN_TILE = 256
EXPERT_TILE = 256
CAST_ROWS = 256
EXPERT_VMEM_LIMIT = 56 * 1024 * 1024
COL_CHUNK = 512
PAGES_PER_STEP = 16
HEAD_PAIRS = N_HEADS // 2
SSM_PAIRS = SSM_HEADS // 2


def _params(sem):
    return pltpu.CompilerParams(dimension_semantics=sem, vmem_limit_bytes=VMEM_LIMIT)


def _sigmoid(x):
    return 1.0 / (1.0 + jnp.exp(-x))


def _softplus(x):
    u = jnp.exp(-jnp.abs(x))
    w = 1.0 + u
    log1p_u = jnp.where(w == 1.0, u, jnp.log(w) * (u / (w - 1.0)))
    return jnp.maximum(x, 0.0) + log1p_u


def _layer_norm_rows(v, g, b):
    mu = jnp.mean(v, axis=-1, keepdims=True)
    d = v - mu
    var = jnp.mean(d * d, axis=-1, keepdims=True)
    return d * lax.rsqrt(var + LN_EPS) * g + b


def _in_proj_kernel(x_ref, wz_ref, wx_ref, wdt_ref, z_ref, xbc_ref, dt_ref):
    x = x_ref[...]
    xb = x.astype(bf16)
    for c in range(0, D_INNER, COL_CHUNK):
        z_ref[:, c:c + COL_CHUNK] = jnp.dot(xb, wz_ref[:, c:c + COL_CHUNK],
                                            preferred_element_type=f32)
    for c in range(0, CONV_DIM, COL_CHUNK):
        xbc_ref[:, c:c + COL_CHUNK] = jnp.dot(xb, wx_ref[:, c:c + COL_CHUNK],
                                              preferred_element_type=f32)
    dt_ref[...] = jnp.dot(x, wdt_ref[...], preferred_element_type=f32, precision=HIGHEST)


def _in_proj(x, wz, wx, wdt, interpret=False):
    n = x.shape[0]
    tm = TOKEN_TILE
    return pl.pallas_call(
        _in_proj_kernel,
        out_shape=(jax.ShapeDtypeStruct((n, D_INNER), f32),
                   jax.ShapeDtypeStruct((n, CONV_DIM), f32),
                   jax.ShapeDtypeStruct((n, LANES), f32)),
        grid=(n // tm,),
        in_specs=[pl.BlockSpec((tm, D_MODEL), lambda i: (i, 0)),
                  pl.BlockSpec((D_MODEL, D_INNER), lambda i: (0, 0)),
                  pl.BlockSpec((D_MODEL, CONV_DIM), lambda i: (0, 0)),
                  pl.BlockSpec((D_MODEL, LANES), lambda i: (0, 0))],
        out_specs=(pl.BlockSpec((tm, D_INNER), lambda i: (i, 0)),
                   pl.BlockSpec((tm, CONV_DIM), lambda i: (i, 0)),
                   pl.BlockSpec((tm, LANES), lambda i: (i, 0))),
        compiler_params=_params(("parallel",)),
        name="ssm_in_proj",
        interpret=interpret,
    )(x, wz, wx, wdt)


def _ssd_kernel(*refs, c_in, nc, has_init):
    T = SSD_CHUNK
    if has_init:
        (z_ref, xbc_ref, dt_ref, conv0_ref, st0_ref, wconv_ref, bconv_ref, dtb_ref, alog_ref,
         d_ref, nw_ref, y_ref, st_out_ref, xbuf, xc, ybuf, st) = refs
    else:
        (z_ref, xbc_ref, dt_ref, wconv_ref, bconv_ref, dtb_ref, alog_ref,
         d_ref, nw_ref, y_ref, st_out_ref, xbuf, xc, ybuf, st) = refs
    c = pl.program_id(1)

    @pl.when(c == 0)
    def _init():
        if has_init:
            xbuf[0:SUBLANES, :] = conv0_ref[0]
            st[...] = st0_ref[0]
        else:
            xbuf[0:SUBLANES, :] = jnp.zeros((SUBLANES, CONV_DIM), f32)
            st[...] = jnp.zeros(st.shape, f32)

    xbuf[SUBLANES:SUBLANES + c_in, :] = xbc_ref[...]
    if c_in < T:
        xbuf[SUBLANES + c_in:SUBLANES + T, :] = jnp.zeros((T - c_in, CONV_DIM), f32)

    for c0 in range(0, CONV_DIM, COL_CHUNK):
        cs = slice(c0, c0 + COL_CHUNK)
        xin = xbuf[0:T + SUBLANES, cs]
        acc = bconv_ref[:, cs]
        for k in range(CONV_W):
            shift = CONV_W - 1 - k
            tap = xin if shift == 0 else pltpu.roll(xin, shift, 0)
            acc = acc + tap[SUBLANES:, :] * wconv_ref[k:k + 1, cs]
        xc[:, cs] = acc * _sigmoid(acc)
    if nc > 1:
        xbuf[0:SUBLANES, :] = xbuf[T:T + SUBLANES, :]

    def pad_rows(v):
        if c_in == T:
            return v
        return jnp.concatenate([v, jnp.zeros((T - c_in, v.shape[1]), v.dtype)], axis=0)

    dtv = pad_rows(_softplus(dt_ref[...] + dtb_ref[...]))
    a = -jnp.exp(alog_ref[...])
    da = dtv * a
    row = lax.broadcasted_iota(jnp.int32, (T, T), 0)
    col = lax.broadcasted_iota(jnp.int32, (T, T), 1)
    causal = col <= row
    tri = causal.astype(f32)
    acs = jnp.dot(tri, da, preferred_element_type=f32, precision=HIGHEST)
    acs_t = acs.T
    dt_t = dtv.T
    w_all = dt_t * jnp.exp(acs_t[:, T - 1:T] - acs_t)
    lo = lax.broadcasted_iota(jnp.int32, (1, LANES), 1) < SSM_HEAD_DIM

    for g in range(SSM_GROUPS):
        bm = xc[:, D_INNER + g * D_STATE:D_INNER + (g + 1) * D_STATE]
        cm = xc[:, D_INNER + (SSM_GROUPS + g) * D_STATE:D_INNER + (SSM_GROUPS + g + 1) * D_STATE]
        cb = lax.dot_general(cm.astype(bf16), bm.astype(bf16), (((1,), (1,)), ((), ())),
                             preferred_element_type=f32)
        bm_t = bm.T
        for e in range(SSM_PAIRS // SSM_GROUPS):
            pr = g * (SSM_PAIRS // SSM_GROUPS) + e
            ls = slice(pr * LANES, (pr + 1) * LANES)
            x_pair = xc[:, ls]
            xb = x_pair.astype(bf16)
            rhs = jnp.concatenate([xb, st[pr].astype(bf16)], axis=0)
            r, u, ea_last = [], [], []
            for hh in range(2):
                h = 2 * pr + hh
                a_b = jnp.broadcast_to(acs[:, h:h + 1], (T, LANES))
                seg = a_b - acs_t[h:h + 1, :]
                dec = jnp.exp(jnp.where(causal, seg, -jnp.inf))
                m_h = cb * dec * dt_t[h:h + 1, :]
                e_a = jnp.exp(a_b)
                lhs = jnp.concatenate([m_h.astype(bf16), (cm * e_a).astype(bf16)], axis=1)
                r.append(jnp.dot(lhs, rhs, preferred_element_type=f32))
                lhs_s = (bm_t * w_all[h:h + 1, :]).astype(bf16)
                u.append(jnp.dot(lhs_s, xb, preferred_element_type=f32))
                ea_last.append(e_a[T - 1:T, :])
            st[pr] = st[pr] * jnp.where(lo, ea_last[0], ea_last[1]) + jnp.where(lo, u[0], u[1])
            yv = jnp.where(lo, r[0], r[1]) + d_ref[:, ls] * x_pair
            zz = pad_rows(z_ref[:, ls])
            ybuf[:, ls] = yv * (zz * _sigmoid(zz))
        gw = D_INNER // SSM_GROUPS
        gs = slice(g * gw, (g + 1) * gw)
        yg = ybuf[:, gs]
        ms = jnp.mean(yg * yg, axis=-1, keepdims=True)
        y_ref[:, gs] = (yg * lax.rsqrt(ms + RMS_EPS) * nw_ref[:, gs])[0:c_in]

    @pl.when(c == nc - 1)
    def _fin():
        st_out_ref[0] = st[...]


def _ssd(z, xbc, dt, conv0, st0, wconv, bconv, dtb, alog, d_exp, nw, *,
         n_seq, seq_len, row_off, interpret=False):
    T = SSD_CHUNK
    has_init = conv0 is not None
    if seq_len >= T:
        c_in, nc = T, seq_len // T
    else:
        c_in, nc = seq_len, 1
    off = row_off // c_in

    def rows(bi, ci):
        return (off + bi * nc + ci, 0)

    const = lambda bi, ci: (0, 0)
    in_specs = [pl.BlockSpec((c_in, D_INNER), rows),
                pl.BlockSpec((c_in, CONV_DIM), rows),
                pl.BlockSpec((c_in, LANES), rows)]
    args = [z, xbc, dt]
    if has_init:
        in_specs += [pl.BlockSpec((1, SUBLANES, CONV_DIM), lambda bi, ci: (bi, 0, 0)),
                     pl.BlockSpec((1, SSM_PAIRS, D_STATE, LANES), lambda bi, ci: (bi, 0, 0, 0))]
        args += [conv0, st0]
    in_specs += [pl.BlockSpec((SUBLANES, CONV_DIM), const),
                 pl.BlockSpec((1, CONV_DIM), const),
                 pl.BlockSpec((1, LANES), const),
                 pl.BlockSpec((1, LANES), const),
                 pl.BlockSpec((1, D_INNER), const),
                 pl.BlockSpec((1, D_INNER), const)]
    args += [wconv, bconv, dtb, alog, d_exp, nw]
    return pl.pallas_call(
        functools.partial(_ssd_kernel, c_in=c_in, nc=nc, has_init=has_init),
        out_shape=(jax.ShapeDtypeStruct((n_seq * seq_len, D_INNER), f32),
                   jax.ShapeDtypeStruct((n_seq, SSM_PAIRS, D_STATE, LANES), f32)),
        grid=(n_seq, nc),
        in_specs=in_specs,
        out_specs=(pl.BlockSpec((c_in, D_INNER), lambda bi, ci: (bi * nc + ci, 0)),
                   pl.BlockSpec((1, SSM_PAIRS, D_STATE, LANES), lambda bi, ci: (bi, 0, 0, 0))),
        scratch_shapes=[pltpu.VMEM((T + 2 * SUBLANES, CONV_DIM), f32),
                        pltpu.VMEM((T, CONV_DIM), f32),
                        pltpu.VMEM((T, D_INNER), f32),
                        pltpu.VMEM((SSM_PAIRS, D_STATE, LANES), f32)],
        compiler_params=_params(("parallel", "arbitrary")),
        name="ssd_scan_init" if has_init else "ssd_scan",
        interpret=interpret,
    )(*args)


def _state_to_pairs(s):
    b = s.shape[0]
    s = s.reshape(b, SSM_PAIRS, 2, SSM_HEAD_DIM, D_STATE)
    return s.transpose(0, 1, 4, 2, 3).reshape(b, SSM_PAIRS, D_STATE, 2 * SSM_HEAD_DIM)


def _pairs_to_state(s):
    b = s.shape[0]
    s = s.reshape(b, SSM_PAIRS, D_STATE, 2, SSM_HEAD_DIM)
    return s.transpose(0, 1, 3, 4, 2).reshape(b, SSM_HEADS, SSM_HEAD_DIM, D_STATE)


def _proj_ln_kernel(yp_ref, ys_ref, w_ref, x_ref, g_ref, b_ref, o_ref, *, split_blk, slabs):
    if slabs:
        y_p = jnp.concatenate([yp_ref[s] for s in range(yp_ref.shape[0])], axis=1)
    else:
        y_p = yp_ref[...]
    y = jnp.where(pl.program_id(0) < split_blk, y_p, ys_ref[...])
    yb = y.astype(bf16)
    for c in range(0, D_MODEL, COL_CHUNK):
        cs = slice(c, c + COL_CHUNK)
        o_ref[:, cs] = DEEPNORM_ALPHA * x_ref[:, cs] + jnp.dot(yb, w_ref[:, cs],
                                                               preferred_element_type=f32)
    o_ref[...] = _layer_norm_rows(o_ref[...], g_ref[...], b_ref[...])


def _proj_ln(y_p, y_s, w, x, g, b, interpret=False):
    n = x.shape[0]
    k = y_s.shape[1]
    tm = TOKEN_TILE
    slabs = y_p.ndim == 3
    rows_p = y_p.shape[1] if slabs else y_p.shape[0]
    assert rows_p % tm == 0 and y_s.shape[0] % tm == 0
    split_blk = rows_p // tm
    if slabs:
        p_spec = pl.BlockSpec((k // LANES, tm, LANES),
                              lambda i: (0, jnp.minimum(i, split_blk - 1), 0))
    else:
        p_spec = pl.BlockSpec((tm, k), lambda i: (jnp.minimum(i, split_blk - 1), 0))
    return pl.pallas_call(
        functools.partial(_proj_ln_kernel, split_blk=split_blk, slabs=slabs),
        out_shape=jax.ShapeDtypeStruct((n, D_MODEL), f32),
        grid=(n // tm,),
        in_specs=[p_spec,
                  pl.BlockSpec((tm, k), lambda i: (jnp.maximum(i - split_blk, 0), 0)),
                  pl.BlockSpec((k, D_MODEL), lambda i: (0, 0)),
                  pl.BlockSpec((tm, D_MODEL), lambda i: (i, 0)),
                  pl.BlockSpec((1, D_MODEL), lambda i: (0, 0)),
                  pl.BlockSpec((1, D_MODEL), lambda i: (0, 0))],
        out_specs=pl.BlockSpec((tm, D_MODEL), lambda i: (i, 0)),
        compiler_params=_params(("parallel",)),
        name="proj_postnorm",
        interpret=interpret,
    )(y_p, y_s, w, x, g, b)


def _router_kernel(x_ref, w_in_ref, b_ref, e_ref, w_ref, cnt_scr):
    @pl.when(pl.program_id(0) == 0)
    def _():
        cnt_scr[...] = jnp.zeros(cnt_scr.shape, f32)

    logits_tok = jnp.dot(x_ref[...], w_in_ref[...], preferred_element_type=f32,
                         precision=HIGHEST)
    logits = logits_tok.T[0:N_EXPERTS, :]
    s = _sigmoid(logits)
    sb = s + b_ref[...]
    srow = [s[i:i + 1, :] for i in range(N_EXPERTS)]
    brow = [sb[i:i + 1, :] for i in range(N_EXPERTS)]
    gscore = []
    for g in range(N_EXPERT_GROUPS):
        v = brow[g * EXPERTS_PER_GROUP:(g + 1) * EXPERTS_PER_GROUP]
        best = None
        for i in range(EXPERTS_PER_GROUP):
            for j in range(i + 1, EXPERTS_PER_GROUP):
                hi = jnp.maximum(v[i], v[j])
                lo_ = jnp.minimum(v[i], v[j])
                p = hi + lo_
                best = p if best is None else jnp.maximum(best, p)
        gscore.append(best)
    gi = jnp.zeros_like(gscore[0], dtype=jnp.int32)
    gbest = gscore[0]
    for g in range(1, N_EXPERT_GROUPS):
        upd = gscore[g] > gbest
        gi = jnp.where(upd, g, gi)
        gbest = jnp.where(upd, gscore[g], gbest)
    vb, vs = [], []
    for k in range(EXPERTS_PER_GROUP):
        tb, ts = brow[k], srow[k]
        for g in range(1, N_EXPERT_GROUPS):
            tb = jnp.where(gi == g, brow[g * EXPERTS_PER_GROUP + k], tb)
            ts = jnp.where(gi == g, srow[g * EXPERTS_PER_GROUP + k], ts)
        vb.append(tb)
        vs.append(ts)
    i1 = jnp.zeros_like(gi)
    b1, s1 = vb[0], vs[0]
    for k in range(1, EXPERTS_PER_GROUP):
        upd = vb[k] > b1
        i1 = jnp.where(upd, k, i1)
        b1 = jnp.where(upd, vb[k], b1)
        s1 = jnp.where(upd, vs[k], s1)
    i2 = jnp.full_like(gi, -1)
    b2 = jnp.full_like(b1, -jnp.inf)
    s2 = jnp.zeros_like(s1)
    for k in range(EXPERTS_PER_GROUP):
        upd = (i1 != k) & ((vb[k] > b2) | (i2 < 0))
        i2 = jnp.where(upd, k, i2)
        b2 = jnp.where(upd, vb[k], b2)
        s2 = jnp.where(upd, vs[k], s2)
    den = s1 + s2
    tm = gi.shape[1]
    e0 = gi * EXPERTS_PER_GROUP + i1
    e1 = gi * EXPERTS_PER_GROUP + i2
    eid = lax.broadcasted_iota(jnp.int32, (N_EXPERTS, tm), 0)
    hit = jnp.where((eid == e0) | (eid == e1), 1.0, 0.0)
    t_row = lax.broadcasted_iota(jnp.int32, (tm, tm), 0)
    t_col = lax.broadcasted_iota(jnp.int32, (tm, tm), 1)
    before = jnp.where(t_row < t_col, 1.0, 0.0).astype(bf16)
    rank_all = jnp.dot(hit.astype(bf16), before, preferred_element_type=f32) + cnt_scr[...]
    r0 = jnp.zeros_like(s1)
    r1 = jnp.zeros_like(s1)
    for k in range(N_EXPERTS):
        rk = rank_all[k:k + 1, :]
        r0 = jnp.where(e0 == k, rk, r0)
        r1 = jnp.where(e1 == k, rk, r1)
    cnt_scr[...] = cnt_scr[...] + jnp.sum(hit, axis=-1, keepdims=True)
    zi = jnp.zeros((SUBLANES - 4, tm), jnp.int32)
    zf = jnp.zeros((SUBLANES - 2, tm), f32)
    e_ref[...] = jnp.concatenate([e0, e1, r0.astype(jnp.int32), r1.astype(jnp.int32), zi], axis=0)
    w_ref[...] = jnp.concatenate([s1 / den, s2 / den, zf], axis=0)


def _router(x, w_pad, b, interpret=False):
    n = x.shape[0]
    tm = TOKEN_TILE
    return pl.pallas_call(
        _router_kernel,
        out_shape=(jax.ShapeDtypeStruct((SUBLANES, n), jnp.int32),
                   jax.ShapeDtypeStruct((SUBLANES, n), f32)),
        grid=(n // tm,),
        in_specs=[pl.BlockSpec((tm, D_MODEL), lambda i: (i, 0)),
                  pl.BlockSpec((D_MODEL, LANES), lambda i: (0, 0)),
                  pl.BlockSpec((N_EXPERTS, 1), lambda i: (0, 0))],
        out_specs=(pl.BlockSpec((SUBLANES, tm), lambda i: (0, i)),
                   pl.BlockSpec((SUBLANES, tm), lambda i: (0, i))),
        scratch_shapes=[pltpu.VMEM((N_EXPERTS, 1), f32)],
        compiler_params=_params(("arbitrary",)),
        name="moe_router",
        interpret=interpret,
    )(x, w_pad, b)


def _expert_kernel(blk_e_ref, n_used_ref, src_ref, dst_ref, x_hbm, wg_ref, wu_ref, wd_ref, y_hbm,
                   wg_b, wu_b, wd_b, h_scr, xb_scr, xbuf, obuf, gsem, ssem, *, trash_row0, n_trash_blk):
    i = pl.program_id(0)
    n_blk = pl.num_programs(0)
    tm = EXPERT_TILE
    n_used = n_used_ref[0]
    slot = i % 2
    e = blk_e_ref[i]
    e_prev = blk_e_ref[jnp.maximum(i - 1, 0)]

    def gather_copy(base, sl, r):
        return pltpu.make_async_copy(x_hbm.at[pl.ds(src_ref[base + r], 1)],
                                     xbuf.at[sl, pl.ds(r, 1)], gsem.at[sl])

    def scatter_copy(base, sl, r):
        return pltpu.make_async_copy(obuf.at[sl, pl.ds(r, 1)],
                                     y_hbm.at[pl.ds(dst_ref[base + tm + r], 1)], ssem.at[sl])

    def wait_gather(sl):
        for r in range(tm):
            gather_copy(0, sl, r).wait()

    def wait_scatter(sl):
        for r in range(tm):
            scatter_copy(0, sl, r).wait()

    @pl.when(i == 0)
    def _first():
        def body(r, c):
            gather_copy(0, 0, r).start()
            return c
        lax.fori_loop(0, tm, body, 0, unroll=8)
        obuf[1] = jnp.zeros((tm, D_MODEL), f32)
        fills = [pltpu.make_async_copy(obuf.at[1], y_hbm.at[pl.ds(trash_row0 + t * tm, tm)],
                                       ssem.at[1]) for t in range(n_trash_blk)]
        for f in fills:
            f.start()
        for f in fills:
            f.wait()

    @pl.when((i == 0) | (e != e_prev))
    def _cast():
        for src, dst in ((wg_ref, wg_b), (wu_ref, wu_b), (wd_ref, wd_b)):
            for r0 in range(0, src.shape[2], CAST_ROWS):
                dst[r0:r0 + CAST_ROWS, :] = src[0, 0, r0:r0 + CAST_ROWS, :].astype(bf16)

    def run_block(sl):
        ot = 1 - sl
        wait_gather(sl)

        @pl.when(i >= 1)
        def _():
            wait_scatter(sl)

        xb_scr[...] = xbuf[sl].astype(bf16)
        nxt = jnp.minimum(i + 1, n_blk - 1) * tm
        prev = (i - 1) * tm
        for r in range(tm):
            gather_copy(nxt, ot, r).start()
            scatter_copy(prev, ot, r).start()
        xb = xb_scr[...]
        for c in range(0, D_EXPERT, COL_CHUNK):
            cs = slice(c, c + COL_CHUNK)
            hg = jnp.dot(xb, wg_b[:, cs], preferred_element_type=f32)
            hu = jnp.dot(xb, wu_b[:, cs], preferred_element_type=f32)
            h_scr[:, cs] = (hg * _sigmoid(hg) * hu).astype(bf16)
        hb = h_scr[...]
        for c in range(0, D_MODEL, COL_CHUNK):
            cs = slice(c, c + COL_CHUNK)
            obuf[sl, :, cs] = jnp.dot(hb, wd_b[:, cs], preferred_element_type=f32)

        @pl.when(i == n_used - 1)
        def _drain():
            def body(r, c):
                scatter_copy(i * tm, sl, r).start()
                return c
            lax.fori_loop(0, tm, body, 0, unroll=8)
            wait_gather(ot)
            wait_scatter(ot)
            wait_scatter(sl)

    for sl in range(2):
        pl.when((i < n_used) & (slot == sl))(functools.partial(run_block, sl))


def _experts(blk_e, n_used, src_row, dst_row, x, wg, wu, wd, layer, n_out_rows, interpret=False):
    tm = EXPERT_TILE
    n_blk = blk_e.shape[0]
    dummy_dst = n_out_rows + N_EXPERTS * tm + jnp.arange(tm, dtype=jnp.int32)
    dst_row = jnp.concatenate([dummy_dst, dst_row])
    wspec = lambda i, be, nu, sr, ds: (layer, be[i], 0, 0)
    hbm = pl.BlockSpec(memory_space=pl.ANY)
    return pl.pallas_call(
        functools.partial(_expert_kernel, trash_row0=n_out_rows, n_trash_blk=N_EXPERTS),
        out_shape=jax.ShapeDtypeStruct((n_out_rows + (N_EXPERTS + 1) * tm, D_MODEL), f32),
        grid_spec=pltpu.PrefetchScalarGridSpec(
            num_scalar_prefetch=4,
            grid=(n_blk,),
            in_specs=[hbm,
                      pl.BlockSpec((1, 1, D_MODEL, D_EXPERT), wspec),
                      pl.BlockSpec((1, 1, D_MODEL, D_EXPERT), wspec),
                      pl.BlockSpec((1, 1, D_EXPERT, D_MODEL), wspec)],
            out_specs=hbm,
            scratch_shapes=[pltpu.VMEM((D_MODEL, D_EXPERT), bf16),
                            pltpu.VMEM((D_MODEL, D_EXPERT), bf16),
                            pltpu.VMEM((D_EXPERT, D_MODEL), bf16),
                            pltpu.VMEM((tm, D_EXPERT), bf16),
                            pltpu.VMEM((tm, D_MODEL), bf16),
                            pltpu.VMEM((2, tm, D_MODEL), f32),
                            pltpu.VMEM((2, tm, D_MODEL), f32),
                            pltpu.SemaphoreType.DMA((2,)),
                            pltpu.SemaphoreType.DMA((2,))]),
        compiler_params=pltpu.CompilerParams(dimension_semantics=("arbitrary",),
                                             vmem_limit_bytes=EXPERT_VMEM_LIMIT),
        name="moe_experts",
        interpret=interpret,
    )(blk_e, n_used, src_row, dst_row, x, wg, wu, wd)


def _combine_ln_kernel(x_ref, y0_ref, y1_ref, w_ref, g_ref, b_ref, *o_refs, split_blk):
    w = w_ref[...]
    v = DEEPNORM_ALPHA * x_ref[...] + (y0_ref[...] * w[:, 0:1] + y1_ref[...] * w[:, 1:2])
    res = _layer_norm_rows(v, g_ref[...], b_ref[...])
    if split_blk is None:
        o_refs[0][...] = res
    else:
        i = pl.program_id(0)

        @pl.when(i < split_blk)
        def _():
            o_refs[0][...] = res

        @pl.when(i >= split_blk)
        def _():
            o_refs[1][...] = res


def _combine_ln(x, y, w_col, g, b, split=None, interpret=False):
    n = x.shape[0]
    tm = TOKEN_TILE
    n_tiles = n // tm
    row = lambda i: (i, 0)
    const = lambda i: (0, 0)
    if split is None:
        split_blk = None
        out_shape = jax.ShapeDtypeStruct((n, D_MODEL), f32)
        out_specs = pl.BlockSpec((tm, D_MODEL), row)
    else:
        assert split % tm == 0 and (n - split) % tm == 0
        split_blk = split // tm
        out_shape = (jax.ShapeDtypeStruct((split, D_MODEL), f32),
                     jax.ShapeDtypeStruct((n - split, D_MODEL), f32))
        out_specs = (pl.BlockSpec((tm, D_MODEL), lambda i: (jnp.minimum(i, split_blk - 1), 0)),
                     pl.BlockSpec((tm, D_MODEL), lambda i: (jnp.maximum(i - split_blk, 0), 0)))
    return pl.pallas_call(
        functools.partial(_combine_ln_kernel, split_blk=split_blk),
        out_shape=out_shape,
        grid=(n // tm,),
        in_specs=[pl.BlockSpec((tm, D_MODEL), row),
                  pl.BlockSpec((tm, D_MODEL), row),
                  pl.BlockSpec((tm, D_MODEL), lambda i: (i + n_tiles, 0)),
                  pl.BlockSpec((tm, SUBLANES), row),
                  pl.BlockSpec((1, D_MODEL), const),
                  pl.BlockSpec((1, D_MODEL), const)],
        out_specs=out_specs,
        compiler_params=_params(("arbitrary",)),
        name="moe_combine_postnorm",
        interpret=interpret,
    )(x, y, y, w_col, g, b)


def _rows(x, idx):
    return x.at[idx].get(mode="promise_in_bounds")


def _moe(x, router_w_pad, router_b, wg, wu, wd, layer, g, b, split=None, interpret=False):
    n = x.shape[0]
    route, e_w = _router(x, router_w_pad, router_b, interpret)
    e_idx = route[0:MOE_TOP_K]
    rank = route[MOE_TOP_K:2 * MOE_TOP_K]
    n_assign = n * MOE_TOP_K
    experts = jnp.arange(N_EXPERTS, dtype=jnp.int32)
    onehot = e_idx[:, :, None] == experts
    counts = jnp.sum(onehot, axis=(0, 1), dtype=jnp.int32)
    tm = EXPERT_TILE
    padded = (counts + tm - 1) // tm * tm
    pad_end = jnp.cumsum(padded)
    pad_start = pad_end - padded
    dest = jnp.sum(jnp.where(onehot, pad_start, 0), axis=-1) + rank
    n_blk = (n_assign + N_EXPERTS * (tm - 1)) // tm + 1
    slot_id = jnp.arange(n_assign, dtype=jnp.int32)
    held = jnp.full((n_blk * tm,), -1, jnp.int32).at[dest.reshape(-1)].set(
        slot_id, unique_indices=True, mode="promise_in_bounds")
    blk_start = jnp.arange(n_blk, dtype=jnp.int32) * tm
    blk_e = jnp.minimum(jnp.sum(pad_end[None, :] <= blk_start[:, None], axis=1, dtype=jnp.int32),
                        N_EXPERTS - 1)
    n_used = (pad_end[-1:] // tm).astype(jnp.int32)
    row = jnp.arange(n_blk * tm, dtype=jnp.int32)
    row_e = jnp.repeat(blk_e, tm)
    trash = n_assign + row_e * tm + jnp.clip(row - (pad_start + counts)[row_e], 0, tm - 1)
    src_row = jnp.where(held < 0, 0, jnp.where(held >= n, held - n, held))
    dst_row = jnp.where(held < 0, trash, held)
    y = _experts(blk_e, n_used, src_row, dst_row, x, wg, wu, wd, layer, n_assign, interpret)
    return _combine_ln(x, y, e_w.T, g, b, split, interpret)


def _rope_slab(t, cos, sin_signed, lo_half):
    swapped = jnp.where(lo_half, pltpu.roll(t, LANES - HEAD_DIM // 2, 1),
                        pltpu.roll(t, HEAD_DIM // 2, 1))
    return t * cos + swapped * sin_signed


def _qkv_kernel(x_ref, wq_ref, wkt_ref, wvt_ref, cos_ref, sin_ref, cost_ref, sint_ref,
                q_ref, kt_ref, vt_ref, ks_ref, vs_ref, *, prompt_blk):
    i = pl.program_id(0)
    tm = x_ref.shape[0]
    xb = x_ref[...].astype(bf16)
    cos = cos_ref[...]
    sin = sin_ref[...]
    lane = lax.broadcasted_iota(jnp.int32, (1, LANES), 1)
    lo_half = (lane % HEAD_DIM) < (HEAD_DIM // 2)
    for c in range(0, D_MODEL, COL_CHUNK):
        t = jnp.dot(xb, wq_ref[:, c:c + COL_CHUNK], preferred_element_type=f32)
        for s in range(0, COL_CHUNK, LANES):
            q_ref[(c + s) // LANES] = _rope_slab(t[:, s:s + LANES], cos, sin, lo_half)

    nt_dims = (((1,), (1,)), ((), ()))
    half = HEAD_DIM // 2
    hpc = COL_CHUNK // HEAD_DIM
    cos_t = cost_ref[...][None]
    sin_t = sint_ref[...][None]
    for r0 in range(0, D_MODEL, COL_CHUNK):
        rs = slice(r0, r0 + COL_CHUNK)
        hs = slice(r0 // HEAD_DIM, r0 // HEAD_DIM + hpc)
        kt = lax.dot_general(wkt_ref[rs, :], xb, nt_dims,
                             preferred_element_type=f32).reshape(hpc, HEAD_DIM, tm)
        sw = jnp.concatenate([kt[:, half:, :], kt[:, :half, :]], axis=1)
        kr = kt * cos_t + sw * sin_t
        vr = lax.dot_general(wvt_ref[rs, :], xb, nt_dims,
                             preferred_element_type=f32).reshape(hpc, HEAD_DIM, tm)

        @pl.when(i < prompt_blk)
        def _(kr=kr, vr=vr, hs=hs):
            kt_ref[0, hs] = kr
            vt_ref[0, hs] = vr

        @pl.when(i >= prompt_blk)
        def _(kr=kr, vr=vr, rs=rs):
            ks_ref[:, rs] = kr.reshape(COL_CHUNK, tm).T
            vs_ref[:, rs] = vr.reshape(COL_CHUNK, tm).T


def _qkv(x, wq, wkt, wvt, cos, sin, cos_t, sin_t, n_seq, seq_len, interpret=False):
    n = x.shape[0]
    tm = TOKEN_TILE
    n_p = n_seq * seq_len
    assert n_p % tm == 0 and seq_len % tm == 0 and (n - n_p) % tm == 0
    prompt_blk = n_p // tm
    tps = seq_len // tm
    row = lambda i: (i, 0)
    const = lambda i: (0, 0)

    def head_major(i):
        j = jnp.minimum(i, prompt_blk - 1)
        return (j // tps, 0, 0, j % tps)

    sample_rows = lambda i: (jnp.maximum(i - prompt_blk, 0), 0)
    kv_t = jax.ShapeDtypeStruct((n_seq, N_HEADS, HEAD_DIM, seq_len), f32)
    kv_s = jax.ShapeDtypeStruct((n - n_p, D_MODEL), f32)
    return pl.pallas_call(
        functools.partial(_qkv_kernel, prompt_blk=prompt_blk),
        out_shape=(jax.ShapeDtypeStruct((HEAD_PAIRS, n, LANES), f32), kv_t, kv_t, kv_s, kv_s),
        grid=(n // tm,),
        in_specs=[pl.BlockSpec((tm, D_MODEL), row),
                  pl.BlockSpec((D_MODEL, D_MODEL), const),
                  pl.BlockSpec((D_MODEL, D_MODEL), const),
                  pl.BlockSpec((D_MODEL, D_MODEL), const),
                  pl.BlockSpec((tm, LANES), row),
                  pl.BlockSpec((tm, LANES), row),
                  pl.BlockSpec((HEAD_DIM, tm), lambda i: (0, i)),
                  pl.BlockSpec((HEAD_DIM, tm), lambda i: (0, i))],
        out_specs=(pl.BlockSpec((HEAD_PAIRS, tm, LANES), lambda i: (0, i, 0)),
                   pl.BlockSpec((1, N_HEADS, HEAD_DIM, tm), head_major),
                   pl.BlockSpec((1, N_HEADS, HEAD_DIM, tm), head_major),
                   pl.BlockSpec((tm, D_MODEL), sample_rows),
                   pl.BlockSpec((tm, D_MODEL), sample_rows)),
        compiler_params=_params(("arbitrary",)),
        name="qkv_rope",
        interpret=interpret,
    )(x, wq, wkt, wvt, cos, sin, cos_t, sin_t)


def _rope_tables(pos):
    half = HEAD_DIM // 2
    inv = ROPE_THETA ** (-jnp.arange(half, dtype=f32) / half)
    ang = pos.astype(f32)[:, None] * inv
    cos, sin = jnp.cos(ang), jnp.sin(ang)
    cos_h = jnp.concatenate([cos, cos], axis=1)
    sin_h = jnp.concatenate([-sin, sin], axis=1)
    rep = (1, LANES // HEAD_DIM)
    return jnp.tile(cos_h, rep), jnp.tile(sin_h, rep), cos_h.T, sin_h.T


def _moba_prompt_kernel(q_ref, kt_ref, vt_ref, o_ref, kaug, vaug, kmcols, *, nb):
    BLK = MOBA_BLOCK
    lane = lax.broadcasted_iota(jnp.int32, (1, LANES), 1)
    lo = lane < HEAD_DIM
    nt_dims = (((1,), (1,)), ((), ()))

    zcol = jnp.zeros((HEAD_DIM, 1), f32)
    kmc = jnp.zeros((LANES, LANES), f32)
    for j in range(nb):
        cs = slice(j * BLK, (j + 1) * BLK)
        k0 = kt_ref[0, 0, :, cs]
        k1 = kt_ref[0, 1, :, cs]
        ind = jnp.where(lax.broadcasted_iota(jnp.int32, (HEAD_DIM, BLK), 0) == j,
                        1.0, 0.0).astype(bf16)
        ones = jnp.ones((HEAD_DIM, BLK), bf16)
        kaug[0, :, cs] = jnp.concatenate([k0.astype(bf16), ind], axis=0)
        kaug[1, :, cs] = jnp.concatenate([ind, k1.astype(bf16)], axis=0)
        vaug[0, :, cs] = jnp.concatenate([vt_ref[0, 0, :, cs].astype(bf16), ones], axis=0)
        vaug[1, :, cs] = jnp.concatenate([ones, vt_ref[0, 1, :, cs].astype(bf16)], axis=0)
        m0 = jnp.concatenate([jnp.mean(k0, axis=1, keepdims=True), zcol], axis=0)
        m1 = jnp.concatenate([zcol, jnp.mean(k1, axis=1, keepdims=True)], axis=0)
        kmc = jnp.where(lane == HEAD_DIM + j, m0, jnp.where(lane == j, m1, kmc))
    kmcols[...] = kmc

    row = lax.broadcasted_iota(jnp.int32, (BLK, BLK), 0)
    col = lax.broadcasted_iota(jnp.int32, (BLK, BLK), 1)
    diag_mask = col <= row

    def attend(iv):
        qp = q_ref[0, iv * BLK:(iv + 1) * BLK, :]
        qs = qp * (HEAD_DIM ** -0.5)
        q_own = [jnp.where(lo, qs, 0.0), jnp.where(lo, 0.0, qs)]
        if iv > MOBA_TOPK:
            gate = jnp.dot(qp, kmcols[...], preferred_element_type=f32, precision=HIGHEST)
            blkid = lane & (HEAD_DIM - 1)
            past = blkid < iv
            gate = jnp.where(past, gate, -jnp.inf)
            cnt = jnp.zeros((BLK, LANES), f32)
            for j in range(iv):
                c0 = jnp.broadcast_to(gate[:, HEAD_DIM + j:HEAD_DIM + j + 1], (BLK, LANES))
                c1 = jnp.broadcast_to(gate[:, j:j + 1], (BLK, LANES))
                cj = jnp.where(lo, c1, c0)
                ahead = (cj > gate) | ((cj == gate) & (blkid > j))
                cnt = cnt + jnp.where(ahead, 1.0, 0.0)
            bias = jnp.where((cnt < MOBA_TOPK) & past, 0.0, MASK_BIAS)
            q_aug = [jnp.where(lo, qs, bias), jnp.where(lo, bias, qs)]
        else:
            q_aug = q_own
        outs = []
        own = slice(iv * BLK, (iv + 1) * BLK)
        for hh in range(2):
            s_own = jnp.dot(q_own[hh].astype(bf16), kaug[hh, :, own], preferred_element_type=f32)
            s_own = jnp.where(diag_mask, s_own, -jnp.inf)
            m = jnp.max(s_own, axis=-1, keepdims=True)
            if iv > 0:
                s_past = jnp.dot(q_aug[hh].astype(bf16), kaug[hh, :, 0:iv * BLK],
                                 preferred_element_type=f32)
                m = jnp.maximum(m, jnp.max(s_past, axis=-1, keepdims=True))
                acc = lax.dot_general(jnp.exp(s_past - m).astype(bf16), vaug[hh, :, 0:iv * BLK],
                                      nt_dims, preferred_element_type=f32)
            p_own = jnp.exp(s_own - m).astype(bf16)
            acc_own = lax.dot_general(p_own, vaug[hh, :, own], nt_dims,
                                      preferred_element_type=f32)
            acc = acc + acc_own if iv > 0 else acc_own
            outs.append(acc / pltpu.roll(acc, HEAD_DIM, 1))
        o_ref[0, iv * BLK:(iv + 1) * BLK, :] = jnp.where(lo, outs[0], outs[1])

    for iv in range(nb):
        attend(iv)


def _moba_prompt(q_hp, k_t, v_t, interpret=False):
    BLK = MOBA_BLOCK
    n_seq, _, _, seq_len = k_t.shape
    nb = seq_len // BLK
    assert nb <= HEAD_DIM and seq_len % BLK == 0
    qo_map = lambda b, hp: (hp, b, 0)
    kv_map = lambda b, hp: (b, hp, 0, 0)
    return pl.pallas_call(
        functools.partial(_moba_prompt_kernel, nb=nb),
        out_shape=jax.ShapeDtypeStruct((HEAD_PAIRS, n_seq * seq_len, LANES), f32),
        grid=(n_seq, HEAD_PAIRS),
        in_specs=[pl.BlockSpec((1, seq_len, LANES), qo_map),
                  pl.BlockSpec((1, 2, HEAD_DIM, seq_len), kv_map),
                  pl.BlockSpec((1, 2, HEAD_DIM, seq_len), kv_map)],
        out_specs=pl.BlockSpec((1, seq_len, LANES), qo_map),
        scratch_shapes=[pltpu.VMEM((2, LANES, seq_len), bf16),
                        pltpu.VMEM((2, LANES, seq_len), bf16),
                        pltpu.VMEM((LANES, LANES), f32)],
        compiler_params=_params(("parallel", "parallel")),
        name="moba_prompt_attn",
        interpret=interpret,
    )(q_hp, k_t, v_t)


def _page_sum_kernel(pt_ref, *refs):
    pages, o_ref = refs[:PAGES_PER_STEP], refs[PAGES_PER_STEP]
    ppb = MOBA_BLOCK // PAGE_SIZE
    bps = PAGES_PER_STEP // ppb
    s = pl.program_id(1)
    lane = lax.broadcasted_iota(jnp.int32, (1, LANES), 1)

    @pl.when(s == 0)
    def _():
        o_ref[...] = jnp.zeros(o_ref.shape, f32)

    for h in range(N_HEADS):
        acc = o_ref[0, h]
        for blk in range(bps):
            t = pages[blk * ppb][0, h]
            for r in range(1, ppb):
                t = t + pages[blk * ppb + r][0, h]
            col = jnp.sum(t, axis=-1, keepdims=True)
            acc = jnp.where(lane == s * bps + blk, col, acc)
        o_ref[0, h] = acc


def _page_sums(page_table, cache_kt, n_blocks, interpret=False):
    bsz, n_pages = page_table.shape
    ppb = MOBA_BLOCK // PAGE_SIZE
    assert n_blocks <= LANES and (n_blocks * ppb) % PAGES_PER_STEP == 0
    steps = n_blocks * ppb // PAGES_PER_STEP
    pt_flat = page_table.reshape(-1)

    def page_map(r):
        return lambda b, s, pt: (pt[b * n_pages + s * PAGES_PER_STEP + r], 0, 0, 0)

    return pl.pallas_call(
        _page_sum_kernel,
        out_shape=jax.ShapeDtypeStruct((bsz, N_HEADS, HEAD_DIM, LANES), f32),
        grid_spec=pltpu.PrefetchScalarGridSpec(
            num_scalar_prefetch=1,
            grid=(bsz, steps),
            in_specs=[pl.BlockSpec((1, N_HEADS, HEAD_DIM, PAGE_SIZE), page_map(r))
                      for r in range(PAGES_PER_STEP)],
            out_specs=pl.BlockSpec((1, N_HEADS, HEAD_DIM, LANES), lambda b, s, pt: (b, 0, 0, 0))),
        compiler_params=_params(("parallel", "arbitrary")),
        name="moba_page_sums",
        interpret=interpret,
    )(pt_flat, *([cache_kt] * PAGES_PER_STEP))


def _sample_topk_kernel(q_ref, ks_ref, idx_ref, *, n_blocks, topk):
    lane = lax.broadcasted_iota(jnp.int32, (1, LANES), 1)
    lane_f = lane.astype(f32)
    for h in range(N_HEADS):
        g = jnp.dot(q_ref[0, h], ks_ref[0, h], preferred_element_type=f32,
                    precision=HIGHEST) * (1.0 / MOBA_BLOCK)
        g = jnp.where(lane < n_blocks, g, -jnp.inf)
        out = jnp.zeros(g.shape, f32)
        for t in range(topk):
            m = jnp.max(g, axis=-1, keepdims=True)
            idx = jnp.min(jnp.where(g == m, lane_f, float(LANES)), axis=-1, keepdims=True)
            out = jnp.where(lane == t, idx, out)
            g = jnp.where(lane_f == idx, -jnp.inf, g)
        idx_ref[0, h] = out.astype(jnp.int32)


def _sample_topk(q4, ksum_t, n_blocks, topk, interpret=False):
    bsz, _, nq, _ = q4.shape
    return pl.pallas_call(
        functools.partial(_sample_topk_kernel, n_blocks=n_blocks, topk=topk),
        out_shape=jax.ShapeDtypeStruct((bsz, N_HEADS, nq, LANES), jnp.int32),
        grid=(bsz,),
        in_specs=[pl.BlockSpec((1, N_HEADS, nq, HEAD_DIM), lambda b: (b, 0, 0, 0)),
                  pl.BlockSpec((1, N_HEADS, HEAD_DIM, LANES), lambda b: (b, 0, 0, 0))],
        out_specs=pl.BlockSpec((1, N_HEADS, nq, LANES), lambda b: (b, 0, 0, 0)),
        compiler_params=_params(("parallel",)),
        name="moba_sample_topk",
        interpret=interpret,
    )(q4, ksum_t)


def _sample_attn_kernel(phys_ref, q_ref, kn_ref, vn_ref, ck_hbm, cv_hbm, o_ref,
                        kbuf, vbuf, sem, *, nq, n_sel):
    g = pl.program_id(0)
    ng = pl.num_programs(0)
    slot = g % 2
    n_pg = nq * n_sel

    def page_copies(gi, sl, r):
        page = phys_ref[gi * n_pg + r]
        head = gi % N_HEADS
        return (pltpu.make_async_copy(ck_hbm.at[page, head], kbuf.at[sl, r], sem.at[0, sl]),
                pltpu.make_async_copy(cv_hbm.at[page, head], vbuf.at[sl, r], sem.at[1, sl]))

    def fetch(gi, sl):
        for r in range(n_pg):
            ck, cv = page_copies(gi, sl, r)
            ck.start()
            cv.start()

    @pl.when(g == 0)
    def _():
        fetch(0, 0)

    @pl.when(g + 1 < ng)
    def _():
        fetch(g + 1, 1 - slot)

    for r in range(n_pg):
        ck, cv = page_copies(g, slot, r)
        ck.wait()
        cv.wait()

    lane = lax.broadcasted_iota(jnp.int32, (1, LANES), 1)
    q_t = q_ref[0, 0] * (HEAD_DIM ** -0.5)
    kn = kn_ref[0, 0]
    vn = vn_ref[0, 0]
    out = jnp.zeros((HEAD_DIM, LANES), f32)
    for qi in range(nq):
        qb = jnp.broadcast_to(q_t[:, qi:qi + 1], (HEAD_DIM, LANES))
        s_own = jnp.sum(kn * qb, axis=0, keepdims=True)
        s_own = jnp.where(lane <= qi, s_own, -jnp.inf)
        s_sel = [jnp.sum(kbuf[slot, qi * n_sel + r] * qb, axis=0, keepdims=True)
                 for r in range(n_sel)]
        m_lane = s_own
        for s in s_sel:
            m_lane = jnp.maximum(m_lane, s)
        m = jnp.max(m_lane, axis=-1, keepdims=True)
        p_own = jnp.exp(s_own - m)
        p_sum = p_own
        acc = vn * p_own
        for r, s in enumerate(s_sel):
            p = jnp.exp(s - m)
            p_sum = p_sum + p
            acc = acc + vbuf[slot, qi * n_sel + r] * p
        den = jnp.sum(p_sum, axis=-1, keepdims=True)
        o_col = jnp.sum(acc, axis=-1, keepdims=True) / den
        out = jnp.where(lane == qi, o_col, out)
    o_ref[0, 0] = out


def _sample_attn(phys, q_t, kn_t, vn_t, cache_kt, cache_vt, nq, n_sel, interpret=False):
    bsz = q_t.shape[0]
    small = pl.BlockSpec((1, 1, HEAD_DIM, LANES),
                         lambda g, ph: (g // N_HEADS, g % N_HEADS, 0, 0))
    hbm = pl.BlockSpec(memory_space=pl.ANY)
    return pl.pallas_call(
        functools.partial(_sample_attn_kernel, nq=nq, n_sel=n_sel),
        out_shape=jax.ShapeDtypeStruct((bsz, N_HEADS, HEAD_DIM, LANES), f32),
        grid_spec=pltpu.PrefetchScalarGridSpec(
            num_scalar_prefetch=1,
            grid=(bsz * N_HEADS,),
            in_specs=[small, small, small, hbm, hbm],
            out_specs=small,
            scratch_shapes=[pltpu.VMEM((2, nq * n_sel, HEAD_DIM, PAGE_SIZE), f32),
                            pltpu.VMEM((2, nq * n_sel, HEAD_DIM, PAGE_SIZE), f32),
                            pltpu.SemaphoreType.DMA((2, 2))]),
        compiler_params=_params(("arbitrary",)),
        name="moba_sample_attn",
        interpret=interpret,
    )(phys, q_t, kn_t, vn_t, cache_kt, cache_vt)


def _heads_first(t, bsz, seq_len):
    return t.reshape(bsz, seq_len, N_HEADS, HEAD_DIM).transpose(0, 2, 1, 3)


def _forward(x_prompt, x_sample, state_ssm, state_conv, cache_k, cache_v, page_table,
             ln_gain, ln_bias, ssm_w_in, ssm_w_conv, ssm_b_conv, ssm_dt_bias, ssm_a_log,
             ssm_d, ssm_norm_w, ssm_w_out, attn_w_q, attn_w_kv, attn_w_o,
             router_w, router_b, moe_w_gate, moe_w_up, moe_w_down, interpret=False):
    bp, lp, _ = x_prompt.shape
    bs, ls, _ = x_sample.shape
    n_p, n_s = bp * lp, bs * ls
    n = n_p + n_s
    n_pages = page_table.shape[1]
    past_len = n_pages * PAGE_SIZE
    assert n % TOKEN_TILE == 0 and lp % MOBA_BLOCK == 0 and lp % SSD_CHUNK == 0
    assert CONV_W - 1 <= ls <= SUBLANES and n_p % ls == 0
    assert past_len % MOBA_BLOCK == 0 and past_len // MOBA_BLOCK >= MOBA_TOPK
    assert ls <= MOBA_BLOCK - past_len % MOBA_BLOCK

    h = jnp.concatenate([x_prompt.reshape(n_p, D_MODEL), x_sample.reshape(n_s, D_MODEL)], axis=0)
    router_wp = jnp.pad(router_w, ((0, 0), (0, LANES - N_EXPERTS)))
    router_bc = router_b.reshape(N_EXPERTS, 1)

    def vec(v):
        return v.reshape(1, -1)

    l = 0
    w_in = ssm_w_in[l]
    wz = w_in[:, :D_INNER].astype(bf16)
    wx = w_in[:, D_INNER:D_INNER + CONV_DIM].astype(bf16)
    wdt = jnp.pad(w_in[:, D_INNER + CONV_DIM:], ((0, 0), (0, LANES - SSM_HEADS)))
    z, xbc, dt = _in_proj(h, wz, wx, wdt, interpret)

    pad_h = (0, LANES - SSM_HEADS)
    wconv = jnp.pad(ssm_w_conv[l], ((0, SUBLANES - CONV_W), (0, 0)))
    ssd_w = (wconv, vec(ssm_b_conv[l]), vec(jnp.pad(ssm_dt_bias[l], pad_h)),
             vec(jnp.pad(ssm_a_log[l], pad_h)), vec(jnp.repeat(ssm_d[l], SSM_HEAD_DIM)),
             vec(ssm_norm_w[l]))
    y_p, st_p = _ssd(z, xbc, dt, None, None, *ssd_w, n_seq=bp, seq_len=lp, row_off=0,
                     interpret=interpret)
    conv0 = jnp.pad(state_conv[l], ((0, 0), (SUBLANES - (CONV_W - 1), 0), (0, 0)))
    y_s, st_s = _ssd(z, xbc, dt, conv0, _state_to_pairs(state_ssm[l]), *ssd_w,
                     n_seq=bs, seq_len=ls, row_off=n_p, interpret=interpret)
    ssm_prompt = _pairs_to_state(st_p)[None]
    ssm_sample = _pairs_to_state(st_s)[None]
    tail = jnp.arange(-(CONV_W - 1), 0, dtype=jnp.int32)
    rows_p = ((jnp.arange(bp, dtype=jnp.int32) + 1) * lp)[:, None] + tail
    rows_s = (n_p + (jnp.arange(bs, dtype=jnp.int32) + 1) * ls)[:, None] + tail
    conv_prompt = _rows(xbc, rows_p.reshape(-1)).reshape(1, bp, CONV_W - 1, CONV_DIM)
    conv_sample = _rows(xbc, rows_s.reshape(-1)).reshape(1, bs, CONV_W - 1, CONV_DIM)

    h = _proj_ln(y_p, y_s, ssm_w_out[l].astype(bf16), h, vec(ln_gain[l, 0]), vec(ln_bias[l, 0]),
                 interpret)
    h = _moe(h, router_wp, router_bc, moe_w_gate, moe_w_up, moe_w_down, l,
             vec(ln_gain[l, 1]), vec(ln_bias[l, 1]), interpret=interpret)

    l = 1
    pos = jnp.concatenate([jnp.tile(jnp.arange(lp), bp), jnp.tile(past_len + jnp.arange(ls), bs)])
    cos_tok, sin_tok, cos_dim, sin_dim = _rope_tables(pos)
    hd_all = N_HEADS * HEAD_DIM
    q_hp, k_t, v_t, k_s, v_s = _qkv(h, attn_w_q[0].astype(bf16),
                                    attn_w_kv[:, :hd_all].T.astype(bf16),
                                    attn_w_kv[:, hd_all:].T.astype(bf16),
                                    cos_tok, sin_tok, cos_dim, sin_dim, bp, lp, interpret)
    k_prompt = jnp.swapaxes(k_t, 2, 3)
    v_prompt = jnp.swapaxes(v_t, 2, 3)
    k_sample = _heads_first(k_s, bs, ls)
    v_sample = _heads_first(v_s, bs, ls)
    q_sample = (q_hp[:, n_p:, :].reshape(HEAD_PAIRS, bs, ls, 2, HEAD_DIM)
                .transpose(1, 0, 3, 2, 4).reshape(bs, N_HEADS, ls, HEAD_DIM))

    o_p = _moba_prompt(q_hp, k_t, v_t, interpret)

    n_full = past_len // MOBA_BLOCK
    ppb = MOBA_BLOCK // PAGE_SIZE
    cache_kt = jnp.swapaxes(cache_k, 2, 3)
    cache_vt = jnp.swapaxes(cache_v, 2, 3)
    ksum_t = _page_sums(page_table, cache_kt, n_full, interpret)
    idx = _sample_topk(q_sample, ksum_t, n_full, MOBA_TOPK, interpret)[..., :MOBA_TOPK]
    lpage = idx[..., None] * ppb + jnp.arange(ppb, dtype=jnp.int32)
    hit = lpage[..., None] == jnp.arange(n_pages, dtype=jnp.int32)
    phys = jnp.sum(jnp.where(hit, page_table[:, None, None, None, None, :], 0), axis=-1)

    def lanes_last(t):
        return jnp.pad(jnp.swapaxes(t, 2, 3), ((0, 0), (0, 0), (0, 0), (0, LANES - ls)))

    o_st = _sample_attn(phys.reshape(-1).astype(jnp.int32), lanes_last(q_sample),
                        lanes_last(k_sample), lanes_last(v_sample), cache_kt, cache_vt,
                        ls, MOBA_TOPK * ppb, interpret)
    o_s = o_st[..., :ls].transpose(0, 3, 1, 2).reshape(n_s, D_MODEL)

    h = _proj_ln(o_p, o_s, attn_w_o[0].astype(bf16), h, vec(ln_gain[l, 0]), vec(ln_bias[l, 0]),
                 interpret)
    h_p, h_s = _moe(h, router_wp, router_bc, moe_w_gate, moe_w_up, moe_w_down, l,
                    vec(ln_gain[l, 1]), vec(ln_bias[l, 1]), split=n_p, interpret=interpret)

    y_prompt = h_p.reshape(bp, lp, D_MODEL)
    y_sample = h_s.reshape(bs, ls, D_MODEL)
    return (y_prompt, y_sample, ssm_prompt, conv_prompt, k_prompt, v_prompt,
            ssm_sample, conv_sample, k_sample, v_sample)


def kernel(x_prompt, x_sample, state_ssm, state_conv, cache_k, cache_v, page_table, ln_gain, ln_bias, ssm_w_in, ssm_w_conv, ssm_b_conv, ssm_dt_bias, ssm_a_log, ssm_d, ssm_norm_w, ssm_w_out, attn_w_q, attn_w_kv, attn_w_o, router_w, router_b, moe_w_gate, moe_w_up, moe_w_down):
    return _forward(x_prompt, x_sample, state_ssm, state_conv, cache_k, cache_v, page_table,
                    ln_gain, ln_bias, ssm_w_in, ssm_w_conv, ssm_b_conv, ssm_dt_bias, ssm_a_log,
                    ssm_d, ssm_norm_w, ssm_w_out, attn_w_q, attn_w_kv, attn_w_o,
                    router_w, router_b, moe_w_gate, moe_w_up, moe_w_down)
```

```python
import functools
import math

import jax
import jax.numpy as jnp
from jax import lax
from jax.experimental import pallas as pl
from jax.experimental.pallas import tpu as pltpu

f32 = jnp.float32
bf16 = jnp.bfloat16
HIGHEST = lax.Precision.HIGHEST

D_MODEL = 1024
DEPTH = 2
PAGE_SIZE = 128
N_A_LAYERS = 1
SSM_HEAD_DIM = 64
D_INNER = 2048
SSM_HEADS = D_INNER // SSM_HEAD_DIM
SSM_GROUPS = 4
D_STATE = 128
CONV_W = 4
CONV_DIM = D_INNER + 2 * SSM_GROUPS * D_STATE
SSD_CHUNK = 128
N_HEADS = 16
HEAD_DIM = 64
MOBA_BLOCK = 256
MOBA_TOPK = 3
ROPE_THETA = 10000.0
N_EXPERTS = 16
N_EXPERT_GROUPS = 4
EXPERTS_PER_GROUP = 4
MOE_TOP_K = 2
D_EXPERT = 1024
DEEPNORM_ALPHA = (2.0 * DEPTH) ** 0.25
LN_EPS = 1e-5
RMS_EPS = 1e-6
MASK_BIAS = -1e30

LANES = 128
SUBLANES = 8
VMEM_LIMIT = 48 * 1024 * 1024

TOKEN_TILE = 256
EXPERT_TILE = 256
CAST_ROWS = 256
EXPERT_VMEM_LIMIT = 56 * 1024 * 1024
COL_CHUNK = 512
PAGES_PER_STEP = 16
HEAD_PAIRS = N_HEADS // 2
SSM_PAIRS = SSM_HEADS // 2


def _params(sem):
    return pltpu.CompilerParams(dimension_semantics=sem, vmem_limit_bytes=VMEM_LIMIT)


def _sigmoid(x):
    return 1.0 / (1.0 + jnp.exp(-x))


def _softplus(x):
    u = jnp.exp(-jnp.abs(x))
    w = 1.0 + u
    log1p_u = jnp.where(w == 1.0, u, jnp.log(w) * (u / (w - 1.0)))
    return jnp.maximum(x, 0.0) + log1p_u


def _layer_norm_rows(v, g, b):
    mu = jnp.mean(v, axis=-1, keepdims=True)
    d = v - mu
    var = jnp.mean(d * d, axis=-1, keepdims=True)
    return d * lax.rsqrt(var + LN_EPS) * g + b


def _in_proj_kernel(x_ref, wz_ref, wx_ref, wdt_ref, z_ref, xbc_ref, dt_ref):
    x = x_ref[...]
    xb = x.astype(bf16)
    for c in range(0, D_INNER, COL_CHUNK):
        z_ref[:, c:c + COL_CHUNK] = jnp.dot(xb, wz_ref[:, c:c + COL_CHUNK],
                                            preferred_element_type=f32)
    for c in range(0, CONV_DIM, COL_CHUNK):
        xbc_ref[:, c:c + COL_CHUNK] = jnp.dot(xb, wx_ref[:, c:c + COL_CHUNK],
                                              preferred_element_type=f32)
    dt_ref[...] = jnp.dot(x, wdt_ref[...], preferred_element_type=f32, precision=HIGHEST)


def _in_proj(x, wz, wx, wdt, interpret=False):
    n = x.shape[0]
    tm = TOKEN_TILE
    return pl.pallas_call(
        _in_proj_kernel,
        out_shape=(jax.ShapeDtypeStruct((n, D_INNER), f32),
                   jax.ShapeDtypeStruct((n, CONV_DIM), f32),
                   jax.ShapeDtypeStruct((n, LANES), f32)),
        grid=(n // tm,),
        in_specs=[pl.BlockSpec((tm, D_MODEL), lambda i: (i, 0)),
                  pl.BlockSpec((D_MODEL, D_INNER), lambda i: (0, 0)),
                  pl.BlockSpec((D_MODEL, CONV_DIM), lambda i: (0, 0)),
                  pl.BlockSpec((D_MODEL, LANES), lambda i: (0, 0))],
        out_specs=(pl.BlockSpec((tm, D_INNER), lambda i: (i, 0)),
                   pl.BlockSpec((tm, CONV_DIM), lambda i: (i, 0)),
                   pl.BlockSpec((tm, LANES), lambda i: (i, 0))),
        compiler_params=_params(("parallel",)),
        name="ssm_in_proj",
        interpret=interpret,
    )(x, wz, wx, wdt)


def _ssd_kernel(*refs, c_in, nc, has_init):
    T = SSD_CHUNK
    if has_init:
        (z_ref, xbc_ref, dt_ref, conv0_ref, st0_ref, wconv_ref, bconv_ref, dtb_ref, alog_ref,
         d_ref, nw_ref, y_ref, st_out_ref, xbuf, xc, ybuf, st) = refs
    else:
        (z_ref, xbc_ref, dt_ref, wconv_ref, bconv_ref, dtb_ref, alog_ref,
         d_ref, nw_ref, y_ref, st_out_ref, xbuf, xc, ybuf, st) = refs
    c = pl.program_id(1)

    @pl.when(c == 0)
    def _init():
        if has_init:
            xbuf[0:SUBLANES, :] = conv0_ref[0]
            st[...] = st0_ref[0]
        else:
            xbuf[0:SUBLANES, :] = jnp.zeros((SUBLANES, CONV_DIM), f32)
            st[...] = jnp.zeros(st.shape, f32)

    xbuf[SUBLANES:SUBLANES + c_in, :] = xbc_ref[...]
    if c_in < T:
        xbuf[SUBLANES + c_in:SUBLANES + T, :] = jnp.zeros((T - c_in, CONV_DIM), f32)

    for c0 in range(0, CONV_DIM, COL_CHUNK):
        cs = slice(c0, c0 + COL_CHUNK)
        xin = xbuf[0:T + SUBLANES, cs]
        acc = bconv_ref[:, cs]
        for k in range(CONV_W):
            shift = CONV_W - 1 - k
            tap = xin if shift == 0 else pltpu.roll(xin, shift, 0)
            acc = acc + tap[SUBLANES:, :] * wconv_ref[k:k + 1, cs]
        xc[:, cs] = acc * _sigmoid(acc)
    if nc > 1:
        xbuf[0:SUBLANES, :] = xbuf[T:T + SUBLANES, :]

    def pad_rows(v):
        if c_in == T:
            return v
        return jnp.concatenate([v, jnp.zeros((T - c_in, v.shape[1]), v.dtype)], axis=0)

    dtv = pad_rows(_softplus(dt_ref[...] + dtb_ref[...]))
    a = -jnp.exp(alog_ref[...])
    da = dtv * a
    row = lax.broadcasted_iota(jnp.int32, (T, T), 0)
    col = lax.broadcasted_iota(jnp.int32, (T, T), 1)
    causal = col <= row
    tri = causal.astype(f32)
    acs = jnp.dot(tri, da, preferred_element_type=f32, precision=HIGHEST)
    acs_t = acs.T
    dt_t = dtv.T
    w_all = dt_t * jnp.exp(acs_t[:, T - 1:T] - acs_t)
    lo = lax.broadcasted_iota(jnp.int32, (1, LANES), 1) < SSM_HEAD_DIM

    for g in range(SSM_GROUPS):
        bm = xc[:, D_INNER + g * D_STATE:D_INNER + (g + 1) * D_STATE]
        cm = xc[:, D_INNER + (SSM_GROUPS + g) * D_STATE:D_INNER + (SSM_GROUPS + g + 1) * D_STATE]
        cb = lax.dot_general(cm.astype(bf16), bm.astype(bf16), (((1,), (1,)), ((), ())),
                             preferred_element_type=f32)
        bm_t = bm.T
        for e in range(SSM_PAIRS // SSM_GROUPS):
            pr = g * (SSM_PAIRS // SSM_GROUPS) + e
            ls = slice(pr * LANES, (pr + 1) * LANES)
            x_pair = xc[:, ls]
            xb = x_pair.astype(bf16)
            rhs = jnp.concatenate([xb, st[pr].astype(bf16)], axis=0)
            r, u, ea_last = [], [], []
            for hh in range(2):
                h = 2 * pr + hh
                a_b = jnp.broadcast_to(acs[:, h:h + 1], (T, LANES))
                seg = a_b - acs_t[h:h + 1, :]
                dec = jnp.exp(jnp.where(causal, seg, -jnp.inf))
                m_h = cb * dec * dt_t[h:h + 1, :]
                e_a = jnp.exp(a_b)
                lhs = jnp.concatenate([m_h.astype(bf16), (cm * e_a).astype(bf16)], axis=1)
                r.append(jnp.dot(lhs, rhs, preferred_element_type=f32))
                lhs_s = (bm_t * w_all[h:h + 1, :]).astype(bf16)
                u.append(jnp.dot(lhs_s, xb, preferred_element_type=f32))
                ea_last.append(e_a[T - 1:T, :])
            st[pr] = st[pr] * jnp.where(lo, ea_last[0], ea_last[1]) + jnp.where(lo, u[0], u[1])
            yv = jnp.where(lo, r[0], r[1]) + d_ref[:, ls] * x_pair
            zz = pad_rows(z_ref[:, ls])
            ybuf[:, ls] = yv * (zz * _sigmoid(zz))
        gw = D_INNER // SSM_GROUPS
        gs = slice(g * gw, (g + 1) * gw)
        yg = ybuf[:, gs]
        ms = jnp.mean(yg * yg, axis=-1, keepdims=True)
        y_ref[:, gs] = (yg * lax.rsqrt(ms + RMS_EPS) * nw_ref[:, gs])[0:c_in]

    @pl.when(c == nc - 1)
    def _fin():
        st_out_ref[0] = st[...]


def _ssd(z, xbc, dt, conv0, st0, wconv, bconv, dtb, alog, d_exp, nw, *,
         n_seq, seq_len, row_off, interpret=False):
    T = SSD_CHUNK
    has_init = conv0 is not None
    if seq_len >= T:
        c_in, nc = T, seq_len // T
    else:
        c_in, nc = seq_len, 1
    off = row_off // c_in

    def rows(bi, ci):
        return (off + bi * nc + ci, 0)

    const = lambda bi, ci: (0, 0)
    in_specs = [pl.BlockSpec((c_in, D_INNER), rows),
                pl.BlockSpec((c_in, CONV_DIM), rows),
                pl.BlockSpec((c_in, LANES), rows)]
    args = [z, xbc, dt]
    if has_init:
        in_specs += [pl.BlockSpec((1, SUBLANES, CONV_DIM), lambda bi, ci: (bi, 0, 0)),
                     pl.BlockSpec((1, SSM_PAIRS, D_STATE, LANES), lambda bi, ci: (bi, 0, 0, 0))]
        args += [conv0, st0]
    in_specs += [pl.BlockSpec((SUBLANES, CONV_DIM), const),
                 pl.BlockSpec((1, CONV_DIM), const),
                 pl.BlockSpec((1, LANES), const),
                 pl.BlockSpec((1, LANES), const),
                 pl.BlockSpec((1, D_INNER), const),
                 pl.BlockSpec((1, D_INNER), const)]
    args += [wconv, bconv, dtb, alog, d_exp, nw]
    return pl.pallas_call(
        functools.partial(_ssd_kernel, c_in=c_in, nc=nc, has_init=has_init),
        out_shape=(jax.ShapeDtypeStruct((n_seq * seq_len, D_INNER), f32),
                   jax.ShapeDtypeStruct((n_seq, SSM_PAIRS, D_STATE, LANES), f32)),
        grid=(n_seq, nc),
        in_specs=in_specs,
        out_specs=(pl.BlockSpec((c_in, D_INNER), lambda bi, ci: (bi * nc + ci, 0)),
                   pl.BlockSpec((1, SSM_PAIRS, D_STATE, LANES), lambda bi, ci: (bi, 0, 0, 0))),
        scratch_shapes=[pltpu.VMEM((T + 2 * SUBLANES, CONV_DIM), f32),
                        pltpu.VMEM((T, CONV_DIM), f32),
                        pltpu.VMEM((T, D_INNER), f32),
                        pltpu.VMEM((SSM_PAIRS, D_STATE, LANES), f32)],
        compiler_params=_params(("parallel", "arbitrary")),
        name="ssd_scan_init" if has_init else "ssd_scan",
        interpret=interpret,
    )(*args)


def _state_to_pairs(s):
    b = s.shape[0]
    s = s.reshape(b, SSM_PAIRS, 2, SSM_HEAD_DIM, D_STATE)
    return s.transpose(0, 1, 4, 2, 3).reshape(b, SSM_PAIRS, D_STATE, 2 * SSM_HEAD_DIM)


def _pairs_to_state(s):
    b = s.shape[0]
    s = s.reshape(b, SSM_PAIRS, D_STATE, 2, SSM_HEAD_DIM)
    return s.transpose(0, 1, 3, 4, 2).reshape(b, SSM_HEADS, SSM_HEAD_DIM, D_STATE)


def _proj_ln_kernel(yp_ref, ys_ref, w_ref, x_ref, g_ref, b_ref, o_ref, *, split_blk, slabs):
    if slabs:
        y_p = jnp.concatenate([yp_ref[s] for s in range(yp_ref.shape[0])], axis=1)
    else:
        y_p = yp_ref[...]
    y = jnp.where(pl.program_id(0) < split_blk, y_p, ys_ref[...])
    yb = y.astype(bf16)
    for c in range(0, D_MODEL, COL_CHUNK):
        cs = slice(c, c + COL_CHUNK)
        o_ref[:, cs] = DEEPNORM_ALPHA * x_ref[:, cs] + jnp.dot(yb, w_ref[:, cs],
                                                               preferred_element_type=f32)
    o_ref[...] = _layer_norm_rows(o_ref[...], g_ref[...], b_ref[...])


def _proj_ln(y_p, y_s, w, x, g, b, interpret=False):
    n = x.shape[0]
    k = y_s.shape[1]
    tm = TOKEN_TILE
    slabs = y_p.ndim == 3
    rows_p = y_p.shape[1] if slabs else y_p.shape[0]
    assert rows_p % tm == 0 and y_s.shape[0] % tm == 0
    split_blk = rows_p // tm
    if slabs:
        p_spec = pl.BlockSpec((k // LANES, tm, LANES),
                              lambda i: (0, jnp.minimum(i, split_blk - 1), 0))
    else:
        p_spec = pl.BlockSpec((tm, k), lambda i: (jnp.minimum(i, split_blk - 1), 0))
    return pl.pallas_call(
        functools.partial(_proj_ln_kernel, split_blk=split_blk, slabs=slabs),
        out_shape=jax.ShapeDtypeStruct((n, D_MODEL), f32),
        grid=(n // tm,),
        in_specs=[p_spec,
                  pl.BlockSpec((tm, k), lambda i: (jnp.maximum(i - split_blk, 0), 0)),
                  pl.BlockSpec((k, D_MODEL), lambda i: (0, 0)),
                  pl.BlockSpec((tm, D_MODEL), lambda i: (i, 0)),
                  pl.BlockSpec((1, D_MODEL), lambda i: (0, 0)),
                  pl.BlockSpec((1, D_MODEL), lambda i: (0, 0))],
        out_specs=pl.BlockSpec((tm, D_MODEL), lambda i: (i, 0)),
        compiler_params=_params(("parallel",)),
        name="proj_postnorm",
        interpret=interpret,
    )(y_p, y_s, w, x, g, b)


def _router_kernel(x_ref, w_in_ref, b_ref, e_ref, w_ref, cnt_scr):
    @pl.when(pl.program_id(0) == 0)
    def _():
        cnt_scr[...] = jnp.zeros(cnt_scr.shape, f32)

    logits_tok = jnp.dot(x_ref[...], w_in_ref[...], preferred_element_type=f32,
                         precision=HIGHEST)
    logits = logits_tok.T[0:N_EXPERTS, :]
    s = _sigmoid(logits)
    sb = s + b_ref[...]
    srow = [s[i:i + 1, :] for i in range(N_EXPERTS)]
    brow = [sb[i:i + 1, :] for i in range(N_EXPERTS)]
    gscore = []
    for g in range(N_EXPERT_GROUPS):
        v = brow[g * EXPERTS_PER_GROUP:(g + 1) * EXPERTS_PER_GROUP]
        best = None
        for i in range(EXPERTS_PER_GROUP):
            for j in range(i + 1, EXPERTS_PER_GROUP):
                hi = jnp.maximum(v[i], v[j])
                lo_ = jnp.minimum(v[i], v[j])
                p = hi + lo_
                best = p if best is None else jnp.maximum(best, p)
        gscore.append(best)
    gi = jnp.zeros_like(gscore[0], dtype=jnp.int32)
    gbest = gscore[0]
    for g in range(1, N_EXPERT_GROUPS):
        upd = gscore[g] > gbest
        gi = jnp.where(upd, g, gi)
        gbest = jnp.where(upd, gscore[g], gbest)
    vb, vs = [], []
    for k in range(EXPERTS_PER_GROUP):
        tb, ts = brow[k], srow[k]
        for g in range(1, N_EXPERT_GROUPS):
            tb = jnp.where(gi == g, brow[g * EXPERTS_PER_GROUP + k], tb)
            ts = jnp.where(gi == g, srow[g * EXPERTS_PER_GROUP + k], ts)
        vb.append(tb)
        vs.append(ts)
    i1 = jnp.zeros_like(gi)
    b1, s1 = vb[0], vs[0]
    for k in range(1, EXPERTS_PER_GROUP):
        upd = vb[k] > b1
        i1 = jnp.where(upd, k, i1)
        b1 = jnp.where(upd, vb[k], b1)
        s1 = jnp.where(upd, vs[k], s1)
    i2 = jnp.full_like(gi, -1)
    b2 = jnp.full_like(b1, -jnp.inf)
    s2 = jnp.zeros_like(s1)
    for k in range(EXPERTS_PER_GROUP):
        upd = (i1 != k) & ((vb[k] > b2) | (i2 < 0))
        i2 = jnp.where(upd, k, i2)
        b2 = jnp.where(upd, vb[k], b2)
        s2 = jnp.where(upd, vs[k], s2)
    den = s1 + s2
    tm = gi.shape[1]
    e0 = gi * EXPERTS_PER_GROUP + i1
    e1 = gi * EXPERTS_PER_GROUP + i2
    eid = lax.broadcasted_iota(jnp.int32, (N_EXPERTS, tm), 0)
    hit = jnp.where((eid == e0) | (eid == e1), 1.0, 0.0)
    t_row = lax.broadcasted_iota(jnp.int32, (tm, tm), 0)
    t_col = lax.broadcasted_iota(jnp.int32, (tm, tm), 1)
    before = jnp.where(t_row < t_col, 1.0, 0.0).astype(bf16)
    rank_all = jnp.dot(hit.astype(bf16), before, preferred_element_type=f32) + cnt_scr[...]
    r0 = jnp.zeros_like(s1)
    r1 = jnp.zeros_like(s1)
    for k in range(N_EXPERTS):
        rk = rank_all[k:k + 1, :]
        r0 = jnp.where(e0 == k, rk, r0)
        r1 = jnp.where(e1 == k, rk, r1)
    cnt_scr[...] = cnt_scr[...] + jnp.sum(hit, axis=-1, keepdims=True)
    zi = jnp.zeros((SUBLANES - 4, tm), jnp.int32)
    zf = jnp.zeros((SUBLANES - 2, tm), f32)
    e_ref[...] = jnp.concatenate([e0, e1, r0.astype(jnp.int32), r1.astype(jnp.int32), zi], axis=0)
    w_ref[...] = jnp.concatenate([s1 / den, s2 / den, zf], axis=0)


def _router(x, w_pad, b, interpret=False):
    n = x.shape[0]
    tm = TOKEN_TILE
    return pl.pallas_call(
        _router_kernel,
        out_shape=(jax.ShapeDtypeStruct((SUBLANES, n), jnp.int32),
                   jax.ShapeDtypeStruct((SUBLANES, n), f32)),
        grid=(n // tm,),
        in_specs=[pl.BlockSpec((tm, D_MODEL), lambda i: (i, 0)),
                  pl.BlockSpec((D_MODEL, LANES), lambda i: (0, 0)),
                  pl.BlockSpec((N_EXPERTS, 1), lambda i: (0, 0))],
        out_specs=(pl.BlockSpec((SUBLANES, tm), lambda i: (0, i)),
                   pl.BlockSpec((SUBLANES, tm), lambda i: (0, i))),
        scratch_shapes=[pltpu.VMEM((N_EXPERTS, 1), f32)],
        compiler_params=_params(("arbitrary",)),
        name="moe_router",
        interpret=interpret,
    )(x, w_pad, b)


def _expert_kernel(blk_e_ref, n_used_ref, src_ref, dst_ref, x_hbm, wg_ref, wu_ref, wd_ref, y_hbm,
                   wg_b, wu_b, wd_b, h_scr, xb_scr, xbuf, obuf, gsem, ssem, *, trash_row0, n_trash_blk):
    i = pl.program_id(0)
    n_blk = pl.num_programs(0)
    tm = EXPERT_TILE
    n_used = n_used_ref[0]
    slot = i % 2
    e = blk_e_ref[i]
    e_prev = blk_e_ref[jnp.maximum(i - 1, 0)]

    def gather_copy(base, sl, r):
        return pltpu.make_async_copy(x_hbm.at[pl.ds(src_ref[base + r], 1)],
                                     xbuf.at[sl, pl.ds(r, 1)], gsem.at[sl])

    def scatter_copy(base, sl, r):
        return pltpu.make_async_copy(obuf.at[sl, pl.ds(r, 1)],
                                     y_hbm.at[pl.ds(dst_ref[base + tm + r], 1)], ssem.at[sl])

    def wait_gather(sl):
        for r in range(tm):
            gather_copy(0, sl, r).wait()

    def wait_scatter(sl):
        for r in range(tm):
            scatter_copy(0, sl, r).wait()

    @pl.when(i == 0)
    def _first():
        def body(r, c):
            gather_copy(0, 0, r).start()
            return c
        lax.fori_loop(0, tm, body, 0, unroll=8)
        obuf[1] = jnp.zeros((tm, D_MODEL), f32)
        fills = [pltpu.make_async_copy(obuf.at[1], y_hbm.at[pl.ds(trash_row0 + t * tm, tm)],
                                       ssem.at[1]) for t in range(n_trash_blk)]
        for f in fills:
            f.start()
        for f in fills:
            f.wait()

    @pl.when((i == 0) | (e != e_prev))
    def _cast():
        for src, dst in ((wg_ref, wg_b), (wu_ref, wu_b), (wd_ref, wd_b)):
            for r0 in range(0, src.shape[2], CAST_ROWS):
                dst[r0:r0 + CAST_ROWS, :] = src[0, 0, r0:r0 + CAST_ROWS, :].astype(bf16)

    def run_block(sl):
        ot = 1 - sl
        wait_gather(sl)

        @pl.when(i >= 1)
        def _():
            wait_scatter(sl)

        xb_scr[...] = xbuf[sl].astype(bf16)
        nxt = jnp.minimum(i + 1, n_blk - 1) * tm
        prev = (i - 1) * tm
        for r in range(tm):
            gather_copy(nxt, ot, r).start()
            scatter_copy(prev, ot, r).start(priority=1)
        xb = xb_scr[...]
        for c in range(0, D_EXPERT, COL_CHUNK):
            cs = slice(c, c + COL_CHUNK)
            hg = jnp.dot(xb, wg_b[:, cs], preferred_element_type=f32)
            hu = jnp.dot(xb, wu_b[:, cs], preferred_element_type=f32)
            h_scr[:, cs] = (hg * _sigmoid(hg) * hu).astype(bf16)
        hb = h_scr[...]
        for c in range(0, D_MODEL, COL_CHUNK):
            cs = slice(c, c + COL_CHUNK)
            obuf[sl, :, cs] = jnp.dot(hb, wd_b[:, cs], preferred_element_type=f32)

        @pl.when(i == n_used - 1)
        def _drain():
            def body(r, c):
                scatter_copy(i * tm, sl, r).start()
                return c
            lax.fori_loop(0, tm, body, 0, unroll=8)
            wait_gather(ot)
            wait_scatter(ot)
            wait_scatter(sl)

    for sl in range(2):
        pl.when((i < n_used) & (slot == sl))(functools.partial(run_block, sl))


def _experts(blk_e, n_used, src_row, dst_row, x, wg, wu, wd, layer, n_out_rows, interpret=False):
    tm = EXPERT_TILE
    n_blk = blk_e.shape[0]
    dummy_dst = n_out_rows + N_EXPERTS * tm + jnp.arange(tm, dtype=jnp.int32)
    dst_row = jnp.concatenate([dummy_dst, dst_row])
    wspec = lambda i, be, nu, sr, ds: (layer, be[i], 0, 0)
    hbm = pl.BlockSpec(memory_space=pl.ANY)
    return pl.pallas_call(
        functools.partial(_expert_kernel, trash_row0=n_out_rows, n_trash_blk=N_EXPERTS),
        out_shape=jax.ShapeDtypeStruct((n_out_rows + (N_EXPERTS + 1) * tm, D_MODEL), f32),
        grid_spec=pltpu.PrefetchScalarGridSpec(
            num_scalar_prefetch=4,
            grid=(n_blk,),
            in_specs=[hbm,
                      pl.BlockSpec((1, 1, D_MODEL, D_EXPERT), wspec),
                      pl.BlockSpec((1, 1, D_MODEL, D_EXPERT), wspec),
                      pl.BlockSpec((1, 1, D_EXPERT, D_MODEL), wspec)],
            out_specs=hbm,
            scratch_shapes=[pltpu.VMEM((D_MODEL, D_EXPERT), bf16),
                            pltpu.VMEM((D_MODEL, D_EXPERT), bf16),
                            pltpu.VMEM((D_EXPERT, D_MODEL), bf16),
                            pltpu.VMEM((tm, D_EXPERT), bf16),
                            pltpu.VMEM((tm, D_MODEL), bf16),
                            pltpu.VMEM((2, tm, D_MODEL), f32),
                            pltpu.VMEM((2, tm, D_MODEL), f32),
                            pltpu.SemaphoreType.DMA((2,)),
                            pltpu.SemaphoreType.DMA((2,))]),
        compiler_params=pltpu.CompilerParams(dimension_semantics=("arbitrary",),
                                             vmem_limit_bytes=EXPERT_VMEM_LIMIT),
        name="moe_experts",
        interpret=interpret,
    )(blk_e, n_used, src_row, dst_row, x, wg, wu, wd)


def _combine_ln_kernel(x_ref, y0_ref, y1_ref, w_ref, g_ref, b_ref, *o_refs, split_blk):
    w = w_ref[...]
    v = DEEPNORM_ALPHA * x_ref[...] + (y0_ref[...] * w[:, 0:1] + y1_ref[...] * w[:, 1:2])
    res = _layer_norm_rows(v, g_ref[...], b_ref[...])
    if split_blk is None:
        o_refs[0][...] = res
    else:
        i = pl.program_id(0)

        @pl.when(i < split_blk)
        def _():
            o_refs[0][...] = res

        @pl.when(i >= split_blk)
        def _():
            o_refs[1][...] = res


def _combine_ln(x, y, w_col, g, b, split=None, interpret=False):
    n = x.shape[0]
    tm = TOKEN_TILE
    n_tiles = n // tm
    row = lambda i: (i, 0)
    const = lambda i: (0, 0)
    if split is None:
        split_blk = None
        out_shape = jax.ShapeDtypeStruct((n, D_MODEL), f32)
        out_specs = pl.BlockSpec((tm, D_MODEL), row)
    else:
        assert split % tm == 0 and (n - split) % tm == 0
        split_blk = split // tm
        out_shape = (jax.ShapeDtypeStruct((split, D_MODEL), f32),
                     jax.ShapeDtypeStruct((n - split, D_MODEL), f32))
        out_specs = (pl.BlockSpec((tm, D_MODEL), lambda i: (jnp.minimum(i, split_blk - 1), 0)),
                     pl.BlockSpec((tm, D_MODEL), lambda i: (jnp.maximum(i - split_blk, 0), 0)))
    return pl.pallas_call(
        functools.partial(_combine_ln_kernel, split_blk=split_blk),
        out_shape=out_shape,
        grid=(n // tm,),
        in_specs=[pl.BlockSpec((tm, D_MODEL), row),
                  pl.BlockSpec((tm, D_MODEL), row),
                  pl.BlockSpec((tm, D_MODEL), lambda i: (i + n_tiles, 0)),
                  pl.BlockSpec((tm, SUBLANES), row),
                  pl.BlockSpec((1, D_MODEL), const),
                  pl.BlockSpec((1, D_MODEL), const)],
        out_specs=out_specs,
        compiler_params=_params(("arbitrary",)),
        name="moe_combine_postnorm",
        interpret=interpret,
    )(x, y, y, w_col, g, b)


def _rows(x, idx):
    return x.at[idx].get(mode="promise_in_bounds")


def _moe(x, router_w_pad, router_b, wg, wu, wd, layer, g, b, split=None, interpret=False):
    n = x.shape[0]
    route, e_w = _router(x, router_w_pad, router_b, interpret)
    e_idx = route[0:MOE_TOP_K]
    rank = route[MOE_TOP_K:2 * MOE_TOP_K]
    n_assign = n * MOE_TOP_K
    experts = jnp.arange(N_EXPERTS, dtype=jnp.int32)
    onehot = e_idx[:, :, None] == experts
    counts = jnp.sum(onehot, axis=(0, 1), dtype=jnp.int32)
    tm = EXPERT_TILE
    padded = (counts + tm - 1) // tm * tm
    pad_end = jnp.cumsum(padded)
    pad_start = pad_end - padded
    dest = jnp.sum(jnp.where(onehot, pad_start, 0), axis=-1) + rank
    n_blk = (n_assign + N_EXPERTS * (tm - 1)) // tm + 1
    slot_id = jnp.arange(n_assign, dtype=jnp.int32)
    held = jnp.full((n_blk * tm,), -1, jnp.int32).at[dest.reshape(-1)].set(
        slot_id, unique_indices=True, mode="promise_in_bounds")
    blk_start = jnp.arange(n_blk, dtype=jnp.int32) * tm
    blk_e = jnp.minimum(jnp.sum(pad_end[None, :] <= blk_start[:, None], axis=1, dtype=jnp.int32),
                        N_EXPERTS - 1)
    n_used = (pad_end[-1:] // tm).astype(jnp.int32)
    row = jnp.arange(n_blk * tm, dtype=jnp.int32)
    row_e = jnp.repeat(blk_e, tm)
    trash = n_assign + row_e * tm + jnp.clip(row - (pad_start + counts)[row_e], 0, tm - 1)
    src_row = jnp.where(held < 0, 0, jnp.where(held >= n, held - n, held))
    dst_row = jnp.where(held < 0, trash, held)
    y = _experts(blk_e, n_used, src_row, dst_row, x, wg, wu, wd, layer, n_assign, interpret)
    return _combine_ln(x, y, e_w.T, g, b, split, interpret)


def _rope_slab(t, cos, sin_signed, lo_half):
    swapped = jnp.where(lo_half, pltpu.roll(t, LANES - HEAD_DIM // 2, 1),
                        pltpu.roll(t, HEAD_DIM // 2, 1))
    return t * cos + swapped * sin_signed


def _qkv_kernel(x_ref, wq_ref, wkt_ref, wvt_ref, cos_ref, sin_ref, cost_ref, sint_ref,
                q_ref, kt_ref, vt_ref, ks_ref, vs_ref, *, prompt_blk):
    i = pl.program_id(0)
    tm = x_ref.shape[0]
    xb = x_ref[...].astype(bf16)
    cos = cos_ref[...]
    sin = sin_ref[...]
    lane = lax.broadcasted_iota(jnp.int32, (1, LANES), 1)
    lo_half = (lane % HEAD_DIM) < (HEAD_DIM // 2)
    for c in range(0, D_MODEL, COL_CHUNK):
        t = jnp.dot(xb, wq_ref[:, c:c + COL_CHUNK], preferred_element_type=f32)
        for s in range(0, COL_CHUNK, LANES):
            q_ref[(c + s) // LANES] = _rope_slab(t[:, s:s + LANES], cos, sin, lo_half)

    nt_dims = (((1,), (1,)), ((), ()))
    half = HEAD_DIM // 2
    hpc = COL_CHUNK // HEAD_DIM
    cos_t = cost_ref[...][None]
    sin_t = sint_ref[...][None]
    for r0 in range(0, D_MODEL, COL_CHUNK):
        rs = slice(r0, r0 + COL_CHUNK)
        hs = slice(r0 // HEAD_DIM, r0 // HEAD_DIM + hpc)
        kt = lax.dot_general(wkt_ref[rs, :], xb, nt_dims,
                             preferred_element_type=f32).reshape(hpc, HEAD_DIM, tm)
        sw = jnp.concatenate([kt[:, half:, :], kt[:, :half, :]], axis=1)
        kr = kt * cos_t + sw * sin_t
        vr = lax.dot_general(wvt_ref[rs, :], xb, nt_dims,
                             preferred_element_type=f32).reshape(hpc, HEAD_DIM, tm)

        @pl.when(i < prompt_blk)
        def _(kr=kr, vr=vr, hs=hs):
            kt_ref[0, hs] = kr
            vt_ref[0, hs] = vr

        @pl.when(i >= prompt_blk)
        def _(kr=kr, vr=vr, rs=rs):
            ks_ref[:, rs] = kr.reshape(COL_CHUNK, tm).T
            vs_ref[:, rs] = vr.reshape(COL_CHUNK, tm).T


def _qkv(x, wq, wkt, wvt, cos, sin, cos_t, sin_t, n_seq, seq_len, interpret=False):
    n = x.shape[0]
    tm = TOKEN_TILE
    n_p = n_seq * seq_len
    assert n_p % tm == 0 and seq_len % tm == 0 and (n - n_p) % tm == 0
    prompt_blk = n_p // tm
    tps = seq_len // tm
    row = lambda i: (i, 0)
    const = lambda i: (0, 0)

    def head_major(i):
        j = jnp.minimum(i, prompt_blk - 1)
        return (j // tps, 0, 0, j % tps)

    sample_rows = lambda i: (jnp.maximum(i - prompt_blk, 0), 0)
    kv_t = jax.ShapeDtypeStruct((n_seq, N_HEADS, HEAD_DIM, seq_len), f32)
    kv_s = jax.ShapeDtypeStruct((n - n_p, D_MODEL), f32)
    return pl.pallas_call(
        functools.partial(_qkv_kernel, prompt_blk=prompt_blk),
        out_shape=(jax.ShapeDtypeStruct((HEAD_PAIRS, n, LANES), f32), kv_t, kv_t, kv_s, kv_s),
        grid=(n // tm,),
        in_specs=[pl.BlockSpec((tm, D_MODEL), row),
                  pl.BlockSpec((D_MODEL, D_MODEL), const),
                  pl.BlockSpec((D_MODEL, D_MODEL), const),
                  pl.BlockSpec((D_MODEL, D_MODEL), const),
                  pl.BlockSpec((tm, LANES), row),
                  pl.BlockSpec((tm, LANES), row),
                  pl.BlockSpec((HEAD_DIM, tm), lambda i: (0, i)),
                  pl.BlockSpec((HEAD_DIM, tm), lambda i: (0, i))],
        out_specs=(pl.BlockSpec((HEAD_PAIRS, tm, LANES), lambda i: (0, i, 0)),
                   pl.BlockSpec((1, N_HEADS, HEAD_DIM, tm), head_major),
                   pl.BlockSpec((1, N_HEADS, HEAD_DIM, tm), head_major),
                   pl.BlockSpec((tm, D_MODEL), sample_rows),
                   pl.BlockSpec((tm, D_MODEL), sample_rows)),
        compiler_params=_params(("arbitrary",)),
        name="qkv_rope",
        interpret=interpret,
    )(x, wq, wkt, wvt, cos, sin, cos_t, sin_t)


def _rope_tables(pos):
    half = HEAD_DIM // 2
    inv = ROPE_THETA ** (-jnp.arange(half, dtype=f32) / half)
    ang = pos.astype(f32)[:, None] * inv
    cos, sin = jnp.cos(ang), jnp.sin(ang)
    cos_h = jnp.concatenate([cos, cos], axis=1)
    sin_h = jnp.concatenate([-sin, sin], axis=1)
    rep = (1, LANES // HEAD_DIM)
    return jnp.tile(cos_h, rep), jnp.tile(sin_h, rep), cos_h.T, sin_h.T


def _moba_prompt_kernel(q_ref, kt_ref, vt_ref, o_ref, kaug, vaug, kmcols, *, nb):
    BLK = MOBA_BLOCK
    lane = lax.broadcasted_iota(jnp.int32, (1, LANES), 1)
    lo = lane < HEAD_DIM
    nt_dims = (((1,), (1,)), ((), ()))

    zcol = jnp.zeros((HEAD_DIM, 1), f32)
    kmc = jnp.zeros((LANES, LANES), f32)
    for j in range(nb):
        cs = slice(j * BLK, (j + 1) * BLK)
        k0 = kt_ref[0, 0, :, cs]
        k1 = kt_ref[0, 1, :, cs]
        ind = jnp.where(lax.broadcasted_iota(jnp.int32, (HEAD_DIM, BLK), 0) == j,
                        1.0, 0.0).astype(bf16)
        ones = jnp.ones((HEAD_DIM, BLK), bf16)
        kaug[0, :, cs] = jnp.concatenate([k0.astype(bf16), ind], axis=0)
        kaug[1, :, cs] = jnp.concatenate([ind, k1.astype(bf16)], axis=0)
        vaug[0, :, cs] = jnp.concatenate([vt_ref[0, 0, :, cs].astype(bf16), ones], axis=0)
        vaug[1, :, cs] = jnp.concatenate([ones, vt_ref[0, 1, :, cs].astype(bf16)], axis=0)
        m0 = jnp.concatenate([jnp.mean(k0, axis=1, keepdims=True), zcol], axis=0)
        m1 = jnp.concatenate([zcol, jnp.mean(k1, axis=1, keepdims=True)], axis=0)
        kmc = jnp.where(lane == HEAD_DIM + j, m0, jnp.where(lane == j, m1, kmc))
    kmcols[...] = kmc

    row = lax.broadcasted_iota(jnp.int32, (BLK, BLK), 0)
    col = lax.broadcasted_iota(jnp.int32, (BLK, BLK), 1)
    diag_mask = col <= row

    def attend(iv):
        qp = q_ref[0, iv * BLK:(iv + 1) * BLK, :]
        qs = qp * (HEAD_DIM ** -0.5)
        q_own = [jnp.where(lo, qs, 0.0), jnp.where(lo, 0.0, qs)]
        if iv > MOBA_TOPK:
            gate = jnp.dot(qp, kmcols[...], preferred_element_type=f32, precision=HIGHEST)
            blkid = lane & (HEAD_DIM - 1)
            past = blkid < iv
            gate = jnp.where(past, gate, -jnp.inf)
            cnt = jnp.zeros((BLK, LANES), f32)
            for j in range(iv):
                c0 = jnp.broadcast_to(gate[:, HEAD_DIM + j:HEAD_DIM + j + 1], (BLK, LANES))
                c1 = jnp.broadcast_to(gate[:, j:j + 1], (BLK, LANES))
                cj = jnp.where(lo, c1, c0)
                ahead = (cj > gate) | ((cj == gate) & (blkid > j))
                cnt = cnt + jnp.where(ahead, 1.0, 0.0)
            bias = jnp.where((cnt < MOBA_TOPK) & past, 0.0, MASK_BIAS)
            q_aug = [jnp.where(lo, qs, bias), jnp.where(lo, bias, qs)]
        else:
            q_aug = q_own
        outs = []
        own = slice(iv * BLK, (iv + 1) * BLK)
        for hh in range(2):
            s_own = jnp.dot(q_own[hh].astype(bf16), kaug[hh, :, own], preferred_element_type=f32)
            s_own = jnp.where(diag_mask, s_own, -jnp.inf)
            m = jnp.max(s_own, axis=-1, keepdims=True)
            if iv > 0:
                s_past = jnp.dot(q_aug[hh].astype(bf16), kaug[hh, :, 0:iv * BLK],
                                 preferred_element_type=f32)
                m = jnp.maximum(m, jnp.max(s_past, axis=-1, keepdims=True))
                acc = lax.dot_general(jnp.exp(s_past - m).astype(bf16), vaug[hh, :, 0:iv * BLK],
                                      nt_dims, preferred_element_type=f32)
            p_own = jnp.exp(s_own - m).astype(bf16)
            acc_own = lax.dot_general(p_own, vaug[hh, :, own], nt_dims,
                                      preferred_element_type=f32)
            acc = acc + acc_own if iv > 0 else acc_own
            outs.append(acc / pltpu.roll(acc, HEAD_DIM, 1))
        o_ref[0, iv * BLK:(iv + 1) * BLK, :] = jnp.where(lo, outs[0], outs[1])

    for iv in range(nb):
        attend(iv)


def _moba_prompt(q_hp, k_t, v_t, interpret=False):
    BLK = MOBA_BLOCK
    n_seq, _, _, seq_len = k_t.shape
    nb = seq_len // BLK
    assert nb <= HEAD_DIM and seq_len % BLK == 0
    qo_map = lambda b, hp: (hp, b, 0)
    kv_map = lambda b, hp: (b, hp, 0, 0)
    return pl.pallas_call(
        functools.partial(_moba_prompt_kernel, nb=nb),
        out_shape=jax.ShapeDtypeStruct((HEAD_PAIRS, n_seq * seq_len, LANES), f32),
        grid=(n_seq, HEAD_PAIRS),
        in_specs=[pl.BlockSpec((1, seq_len, LANES), qo_map),
                  pl.BlockSpec((1, 2, HEAD_DIM, seq_len), kv_map),
                  pl.BlockSpec((1, 2, HEAD_DIM, seq_len), kv_map)],
        out_specs=pl.BlockSpec((1, seq_len, LANES), qo_map),
        scratch_shapes=[pltpu.VMEM((2, LANES, seq_len), bf16),
                        pltpu.VMEM((2, LANES, seq_len), bf16),
                        pltpu.VMEM((LANES, LANES), f32)],
        compiler_params=_params(("parallel", "parallel")),
        name="moba_prompt_attn",
        interpret=interpret,
    )(q_hp, k_t, v_t)


def _page_sum_kernel(pt_ref, *refs):
    pages, o_ref = refs[:PAGES_PER_STEP], refs[PAGES_PER_STEP]
    ppb = MOBA_BLOCK // PAGE_SIZE
    bps = PAGES_PER_STEP // ppb
    s = pl.program_id(1)
    lane = lax.broadcasted_iota(jnp.int32, (1, LANES), 1)

    @pl.when(s == 0)
    def _():
        o_ref[...] = jnp.zeros(o_ref.shape, f32)

    for h in range(N_HEADS):
        acc = o_ref[0, h]
        for blk in range(bps):
            t = pages[blk * ppb][0, h]
            for r in range(1, ppb):
                t = t + pages[blk * ppb + r][0, h]
            col = jnp.sum(t, axis=-1, keepdims=True)
            acc = jnp.where(lane == s * bps + blk, col, acc)
        o_ref[0, h] = acc


def _page_sums(page_table, cache_kt, n_blocks, interpret=False):
    bsz, n_pages = page_table.shape
    ppb = MOBA_BLOCK // PAGE_SIZE
    assert n_blocks <= LANES and (n_blocks * ppb) % PAGES_PER_STEP == 0
    steps = n_blocks * ppb // PAGES_PER_STEP
    pt_flat = page_table.reshape(-1)

    def page_map(r):
        return lambda b, s, pt: (pt[b * n_pages + s * PAGES_PER_STEP + r], 0, 0, 0)

    return pl.pallas_call(
        _page_sum_kernel,
        out_shape=jax.ShapeDtypeStruct((bsz, N_HEADS, HEAD_DIM, LANES), f32),
        grid_spec=pltpu.PrefetchScalarGridSpec(
            num_scalar_prefetch=1,
            grid=(bsz, steps),
            in_specs=[pl.BlockSpec((1, N_HEADS, HEAD_DIM, PAGE_SIZE), page_map(r))
                      for r in range(PAGES_PER_STEP)],
            out_specs=pl.BlockSpec((1, N_HEADS, HEAD_DIM, LANES), lambda b, s, pt: (b, 0, 0, 0))),
        compiler_params=_params(("parallel", "arbitrary")),
        name="moba_page_sums",
        interpret=interpret,
    )(pt_flat, *([cache_kt] * PAGES_PER_STEP))


def _sample_topk_kernel(q_ref, ks_ref, idx_ref, *, n_blocks, topk):
    lane = lax.broadcasted_iota(jnp.int32, (1, LANES), 1)
    lane_f = lane.astype(f32)
    for h in range(N_HEADS):
        g = jnp.dot(q_ref[0, h], ks_ref[0, h], preferred_element_type=f32,
                    precision=HIGHEST) * (1.0 / MOBA_BLOCK)
        g = jnp.where(lane < n_blocks, g, -jnp.inf)
        out = jnp.zeros(g.shape, f32)
        for t in range(topk):
            m = jnp.max(g, axis=-1, keepdims=True)
            idx = jnp.min(jnp.where(g == m, lane_f, float(LANES)), axis=-1, keepdims=True)
            out = jnp.where(lane == t, idx, out)
            g = jnp.where(lane_f == idx, -jnp.inf, g)
        idx_ref[0, h] = out.astype(jnp.int32)


def _sample_topk(q4, ksum_t, n_blocks, topk, interpret=False):
    bsz, _, nq, _ = q4.shape
    return pl.pallas_call(
        functools.partial(_sample_topk_kernel, n_blocks=n_blocks, topk=topk),
        out_shape=jax.ShapeDtypeStruct((bsz, N_HEADS, nq, LANES), jnp.int32),
        grid=(bsz,),
        in_specs=[pl.BlockSpec((1, N_HEADS, nq, HEAD_DIM), lambda b: (b, 0, 0, 0)),
                  pl.BlockSpec((1, N_HEADS, HEAD_DIM, LANES), lambda b: (b, 0, 0, 0))],
        out_specs=pl.BlockSpec((1, N_HEADS, nq, LANES), lambda b: (b, 0, 0, 0)),
        compiler_params=_params(("parallel",)),
        name="moba_sample_topk",
        interpret=interpret,
    )(q4, ksum_t)


def _sample_attn_kernel(phys_ref, q_ref, kn_ref, vn_ref, ck_hbm, cv_hbm, o_ref,
                        kbuf, vbuf, sem, *, nq, n_sel):
    g = pl.program_id(0)
    ng = pl.num_programs(0)
    slot = g % 2
    n_pg = nq * n_sel

    def page_copies(gi, sl, r):
        page = phys_ref[gi * n_pg + r]
        head = gi % N_HEADS
        return (pltpu.make_async_copy(ck_hbm.at[page, head], kbuf.at[sl, r], sem.at[0, sl]),
                pltpu.make_async_copy(cv_hbm.at[page, head], vbuf.at[sl, r], sem.at[1, sl]))

    def fetch(gi, sl):
        for r in range(n_pg):
            ck, cv = page_copies(gi, sl, r)
            ck.start()
            cv.start()

    @pl.when(g == 0)
    def _():
        fetch(0, 0)

    @pl.when(g + 1 < ng)
    def _():
        fetch(g + 1, 1 - slot)

    for r in range(n_pg):
        ck, cv = page_copies(g, slot, r)
        ck.wait()
        cv.wait()

    lane = lax.broadcasted_iota(jnp.int32, (1, LANES), 1)
    q_t = q_ref[0, 0] * (HEAD_DIM ** -0.5)
    kn = kn_ref[0, 0]
    vn = vn_ref[0, 0]
    out = jnp.zeros((HEAD_DIM, LANES), f32)
    for qi in range(nq):
        qb = jnp.broadcast_to(q_t[:, qi:qi + 1], (HEAD_DIM, LANES))
        s_own = jnp.sum(kn * qb, axis=0, keepdims=True)
        s_own = jnp.where(lane <= qi, s_own, -jnp.inf)
        s_sel = [jnp.sum(kbuf[slot, qi * n_sel + r] * qb, axis=0, keepdims=True)
                 for r in range(n_sel)]
        m_lane = s_own
        for s in s_sel:
            m_lane = jnp.maximum(m_lane, s)
        m = jnp.max(m_lane, axis=-1, keepdims=True)
        p_own = jnp.exp(s_own - m)
        p_sum = p_own
        acc = vn * p_own
        for r, s in enumerate(s_sel):
            p = jnp.exp(s - m)
            p_sum = p_sum + p
            acc = acc + vbuf[slot, qi * n_sel + r] * p
        den = jnp.sum(p_sum, axis=-1, keepdims=True)
        o_col = jnp.sum(acc, axis=-1, keepdims=True) / den
        out = jnp.where(lane == qi, o_col, out)
    o_ref[0, 0] = out


def _sample_attn(phys, q_t, kn_t, vn_t, cache_kt, cache_vt, nq, n_sel, interpret=False):
    bsz = q_t.shape[0]
    small = pl.BlockSpec((1, 1, HEAD_DIM, LANES),
                         lambda g, ph: (g // N_HEADS, g % N_HEADS, 0, 0))
    hbm = pl.BlockSpec(memory_space=pl.ANY)
    return pl.pallas_call(
        functools.partial(_sample_attn_kernel, nq=nq, n_sel=n_sel),
        out_shape=jax.ShapeDtypeStruct((bsz, N_HEADS, HEAD_DIM, LANES), f32),
        grid_spec=pltpu.PrefetchScalarGridSpec(
            num_scalar_prefetch=1,
            grid=(bsz * N_HEADS,),
            in_specs=[small, small, small, hbm, hbm],
            out_specs=small,
            scratch_shapes=[pltpu.VMEM((2, nq * n_sel, HEAD_DIM, PAGE_SIZE), f32),
                            pltpu.VMEM((2, nq * n_sel, HEAD_DIM, PAGE_SIZE), f32),
                            pltpu.SemaphoreType.DMA((2, 2))]),
        compiler_params=_params(("arbitrary",)),
        name="moba_sample_attn",
        interpret=interpret,
    )(phys, q_t, kn_t, vn_t, cache_kt, cache_vt)


def _heads_first(t, bsz, seq_len):
    return t.reshape(bsz, seq_len, N_HEADS, HEAD_DIM).transpose(0, 2, 1, 3)


def _forward(x_prompt, x_sample, state_ssm, state_conv, cache_k, cache_v, page_table,
             ln_gain, ln_bias, ssm_w_in, ssm_w_conv, ssm_b_conv, ssm_dt_bias, ssm_a_log,
             ssm_d, ssm_norm_w, ssm_w_out, attn_w_q, attn_w_kv, attn_w_o,
             router_w, router_b, moe_w_gate, moe_w_up, moe_w_down, interpret=False):
    bp, lp, _ = x_prompt.shape
    bs, ls, _ = x_sample.shape
    n_p, n_s = bp * lp, bs * ls
    n = n_p + n_s
    n_pages = page_table.shape[1]
    past_len = n_pages * PAGE_SIZE
    assert n % TOKEN_TILE == 0 and lp % MOBA_BLOCK == 0 and lp % SSD_CHUNK == 0
    assert CONV_W - 1 <= ls <= SUBLANES and n_p % ls == 0
    assert past_len % MOBA_BLOCK == 0 and past_len // MOBA_BLOCK >= MOBA_TOPK
    assert ls <= MOBA_BLOCK - past_len % MOBA_BLOCK

    h = jnp.concatenate([x_prompt.reshape(n_p, D_MODEL), x_sample.reshape(n_s, D_MODEL)], axis=0)
    router_wp = jnp.pad(router_w, ((0, 0), (0, LANES - N_EXPERTS)))
    router_bc = router_b.reshape(N_EXPERTS, 1)

    def vec(v):
        return v.reshape(1, -1)

    l = 0
    w_in = ssm_w_in[l]
    wz = w_in[:, :D_INNER].astype(bf16)
    wx = w_in[:, D_INNER:D_INNER + CONV_DIM].astype(bf16)
    wdt = jnp.pad(w_in[:, D_INNER + CONV_DIM:], ((0, 0), (0, LANES - SSM_HEADS)))
    z, xbc, dt = _in_proj(h, wz, wx, wdt, interpret)

    pad_h = (0, LANES - SSM_HEADS)
    wconv = jnp.pad(ssm_w_conv[l], ((0, SUBLANES - CONV_W), (0, 0)))
    ssd_w = (wconv, vec(ssm_b_conv[l]), vec(jnp.pad(ssm_dt_bias[l], pad_h)),
             vec(jnp.pad(ssm_a_log[l], pad_h)), vec(jnp.repeat(ssm_d[l], SSM_HEAD_DIM)),
             vec(ssm_norm_w[l]))
    y_p, st_p = _ssd(z, xbc, dt, None, None, *ssd_w, n_seq=bp, seq_len=lp, row_off=0,
                     interpret=interpret)
    conv0 = jnp.pad(state_conv[l], ((0, 0), (SUBLANES - (CONV_W - 1), 0), (0, 0)))
    y_s, st_s = _ssd(z, xbc, dt, conv0, _state_to_pairs(state_ssm[l]), *ssd_w,
                     n_seq=bs, seq_len=ls, row_off=n_p, interpret=interpret)
    ssm_prompt = _pairs_to_state(st_p)[None]
    ssm_sample = _pairs_to_state(st_s)[None]
    tail = jnp.arange(-(CONV_W - 1), 0, dtype=jnp.int32)
    rows_p = ((jnp.arange(bp, dtype=jnp.int32) + 1) * lp)[:, None] + tail
    rows_s = (n_p + (jnp.arange(bs, dtype=jnp.int32) + 1) * ls)[:, None] + tail
    conv_prompt = _rows(xbc, rows_p.reshape(-1)).reshape(1, bp, CONV_W - 1, CONV_DIM)
    conv_sample = _rows(xbc, rows_s.reshape(-1)).reshape(1, bs, CONV_W - 1, CONV_DIM)

    h = _proj_ln(y_p, y_s, ssm_w_out[l].astype(bf16), h, vec(ln_gain[l, 0]), vec(ln_bias[l, 0]),
                 interpret)
    h = _moe(h, router_wp, router_bc, moe_w_gate, moe_w_up, moe_w_down, l,
             vec(ln_gain[l, 1]), vec(ln_bias[l, 1]), interpret=interpret)

    l = 1
    pos = jnp.concatenate([jnp.tile(jnp.arange(lp), bp), jnp.tile(past_len + jnp.arange(ls), bs)])
    cos_tok, sin_tok, cos_dim, sin_dim = _rope_tables(pos)
    hd_all = N_HEADS * HEAD_DIM
    q_hp, k_t, v_t, k_s, v_s = _qkv(h, attn_w_q[0].astype(bf16),
                                    attn_w_kv[:, :hd_all].T.astype(bf16),
                                    attn_w_kv[:, hd_all:].T.astype(bf16),
                                    cos_tok, sin_tok, cos_dim, sin_dim, bp, lp, interpret)
    k_prompt = jnp.swapaxes(k_t, 2, 3)
    v_prompt = jnp.swapaxes(v_t, 2, 3)
    k_sample = _heads_first(k_s, bs, ls)
    v_sample = _heads_first(v_s, bs, ls)
    q_sample = (q_hp[:, n_p:, :].reshape(HEAD_PAIRS, bs, ls, 2, HEAD_DIM)
                .transpose(1, 0, 3, 2, 4).reshape(bs, N_HEADS, ls, HEAD_DIM))

    o_p = _moba_prompt(q_hp, k_t, v_t, interpret)

    n_full = past_len // MOBA_BLOCK
    ppb = MOBA_BLOCK // PAGE_SIZE
    cache_kt = jnp.swapaxes(cache_k, 2, 3)
    cache_vt = jnp.swapaxes(cache_v, 2, 3)
    ksum_t = _page_sums(page_table, cache_kt, n_full, interpret)
    idx = _sample_topk(q_sample, ksum_t, n_full, MOBA_TOPK, interpret)[..., :MOBA_TOPK]
    lpage = idx[..., None] * ppb + jnp.arange(ppb, dtype=jnp.int32)
    hit = lpage[..., None] == jnp.arange(n_pages, dtype=jnp.int32)
    phys = jnp.sum(jnp.where(hit, page_table[:, None, None, None, None, :], 0), axis=-1)

    def lanes_last(t):
        return jnp.pad(jnp.swapaxes(t, 2, 3), ((0, 0), (0, 0), (0, 0), (0, LANES - ls)))

    o_st = _sample_attn(phys.reshape(-1).astype(jnp.int32), lanes_last(q_sample),
                        lanes_last(k_sample), lanes_last(v_sample), cache_kt, cache_vt,
                        ls, MOBA_TOPK * ppb, interpret)
    o_s = o_st[..., :ls].transpose(0, 3, 1, 2).reshape(n_s, D_MODEL)

    h = _proj_ln(o_p, o_s, attn_w_o[0].astype(bf16), h, vec(ln_gain[l, 0]), vec(ln_bias[l, 0]),
                 interpret)
    h_p, h_s = _moe(h, router_wp, router_bc, moe_w_gate, moe_w_up, moe_w_down, l,
                    vec(ln_gain[l, 1]), vec(ln_bias[l, 1]), split=n_p, interpret=interpret)

    y_prompt = h_p.reshape(bp, lp, D_MODEL)
    y_sample = h_s.reshape(bs, ls, D_MODEL)
    return (y_prompt, y_sample, ssm_prompt, conv_prompt, k_prompt, v_prompt,
            ssm_sample, conv_sample, k_sample, v_sample)


def kernel(x_prompt, x_sample, state_ssm, state_conv, cache_k, cache_v, page_table, ln_gain, ln_bias, ssm_w_in, ssm_w_conv, ssm_b_conv, ssm_dt_bias, ssm_a_log, ssm_d, ssm_norm_w, ssm_w_out, attn_w_q, attn_w_kv, attn_w_o, router_w, router_b, moe_w_gate, moe_w_up, moe_w_down):
    return _forward(x_prompt, x_sample, state_ssm, state_conv, cache_k, cache_v, page_table,
                    ln_gain, ln_bias, ssm_w_in, ssm_w_conv, ssm_b_conv, ssm_dt_bias, ssm_a_log,
                    ssm_d, ssm_norm_w, ssm_w_out, attn_w_q, attn_w_kv, attn_w_o,
                    router_w, router_b, moe_w_gate, moe_w_up, moe_w_down)
```
